```python
import math
import jax, jax.numpy as jnp
from jax import lax
import numpy as np

D_MODEL = 1024
BATCH = 4
SEQ = 4096
DEPTH = 2

D_FF = 2816
NORM_EPS = 1e-6
Q_BLOCK = 128
ROPE_THETA = 500000.0
MIX_WIDTH = D_MODEL

DIFF_HEADS = 4
DIFF_HEAD_DIM = 64
DIFF_V_DIM = 2 * DIFF_HEAD_DIM
DIFF_WIDTH = DIFF_HEADS * DIFF_V_DIM
DIFF_QK_COLS = DIFF_HEADS * 2 * DIFF_HEAD_DIM
ROT_DIM = DIFF_HEAD_DIM // 4
DIFF_LAMBDA_INIT = 0.8 - 0.6 * math.exp(-0.3 * 0)

S5_WIDTH = MIX_WIDTH - DIFF_WIDTH
S5_GROUP = 16
S5_GROUPS = S5_WIDTH // S5_GROUP
S5_STATE = 64
L0_IN = 2 * DIFF_QK_COLS + DIFF_WIDTH + S5_WIDTH

MLSTM_HEADS = 4
MLSTM_QK_DIM = 64
MLSTM_V_DIM = 128
MLSTM_WIDTH = MLSTM_HEADS * MLSTM_V_DIM
MLSTM_QK_COLS = 2 * MLSTM_HEADS * MLSTM_QK_DIM
MLSTM_CHUNK = 64
MLSTM_CONV = 4

FOX_HEADS = 8
FOX_HEAD_DIM = 64
FOX_WIDTH = FOX_HEADS * FOX_HEAD_DIM
L1_IN = MLSTM_QK_COLS + MLSTM_WIDTH + 2 * MLSTM_HEADS + MLSTM_WIDTH + 3 * FOX_WIDTH + FOX_HEADS

kernel_name = "hybrid_diffattn_s5_mlstm_fox_macaron"


def _rmsnorm(x, g):
    x32 = x.astype(jnp.float32)
    y = x32 * lax.rsqrt(jnp.mean(x32 * x32, axis=-1, keepdims=True) + NORM_EPS)
    return (y * g.astype(jnp.float32)).astype(x.dtype)


def _swiglu(h, w_gate, w_up, w_down):
    return (jax.nn.silu(h @ w_gate) * (h @ w_up)) @ w_down


def _rope_tables(S):
    pos = jnp.arange(S, dtype=jnp.float32)
    inv = ROPE_THETA ** (-jnp.arange(0, ROT_DIM, 2, dtype=jnp.float32) / ROT_DIM)
    ang = pos[:, None] * inv[None, :]
    return jnp.cos(ang), jnp.sin(ang)


def _partial_rope(x, cos, sin):
    half = ROT_DIM // 2
    x1 = x[..., :half]
    x2 = x[..., half:ROT_DIM]
    c = cos.astype(x.dtype)
    s = sin.astype(x.dtype)
    return jnp.concatenate([x1 * c - x2 * s, x2 * c + x1 * s, x[..., ROT_DIM:]], axis=-1)


def _causal_conv(x, w, b):
    K, C = w.shape
    y = lax.conv_general_dilated(x, w[:, None, :].astype(x.dtype), window_strides=(1,),
                                 padding=[(K - 1, 0)], dimension_numbers=('NWC', 'WIO', 'NWC'),
                                 feature_group_count=C)
    return y + b.astype(x.dtype)


def _diff_attention(q, k, v, lam, subln):
    B, H, _, S, d = q.shape
    scale = d ** -0.5
    kpos = jnp.arange(S)

    def block(i):
        qb = lax.dynamic_slice_in_dim(q, i * Q_BLOCK, Q_BLOCK, axis=3)
        s = jnp.einsum('bhcqd,bhckd->bhcqk', qb, k).astype(jnp.float32) * scale
        qpos = i * Q_BLOCK + jnp.arange(Q_BLOCK)
        s = jnp.where(kpos[None, :] <= qpos[:, None], s, -jnp.inf)
        p = jax.nn.softmax(s, axis=-1)
        p = p[:, :, 0] - lam * p[:, :, 1]
        return jnp.einsum('bhqk,bhkd->bhqd', p.astype(v.dtype), v)

    o = lax.map(block, jnp.arange(S // Q_BLOCK))
    o = jnp.moveaxis(o, 0, 2).reshape(B, H, S, v.shape[-1])
    o = _rmsnorm(o, subln) * (1.0 - DIFF_LAMBDA_INIT)
    return o.transpose(0, 2, 1, 3).reshape(B, S, H * v.shape[-1])


def _s5(u, a_re, a_im, log_dt, b_re, b_im, c_re, c_im, d_skip, w_glu, b_glu):
    B, S, W = u.shape
    f32 = jnp.float32
    u32 = u.astype(f32).reshape(B, S, S5_GROUPS, S5_GROUP)
    lr, li = a_re.astype(f32), a_im.astype(f32)
    dt = jnp.exp(log_dt.astype(f32))[:, None]
    mag = jnp.exp(lr * dt)
    ar, ai = mag * jnp.cos(li * dt), mag * jnp.sin(li * dt)
    den = lr * lr + li * li
    gr = ((ar - 1.0) * lr + ai * li) / den
    gi = (ai * lr - (ar - 1.0) * li) / den
    br, bi = b_re.astype(f32), b_im.astype(f32)
    bbr = gr[..., None] * br - gi[..., None] * bi
    bbi = gr[..., None] * bi + gi[..., None] * br
    xr = jnp.einsum('bsgh,gph->sbgp', u32, bbr)
    xi = jnp.einsum('bsgh,gph->sbgp', u32, bbi)
    are = jnp.broadcast_to(ar[None, None], (S, 1, S5_GROUPS, S5_STATE))
    aim = jnp.broadcast_to(ai[None, None], (S, 1, S5_GROUPS, S5_STATE))

    def combine(e1, e2):
        a1r, a1i, b1r, b1i = e1
        a2r, a2i, b2r, b2i = e2
        return (a2r * a1r - a2i * a1i, a2r * a1i + a2i * a1r,
                a2r * b1r - a2i * b1i + b2r, a2r * b1i + a2i * b1r + b2i)

    _, _, hr, hi = lax.associative_scan(combine, (are, aim, xr, xi), axis=0)
    y = (jnp.einsum('ghp,sbgp->bsgh', c_re.astype(f32), hr)
         - jnp.einsum('ghp,sbgp->bsgh', c_im.astype(f32), hi)).reshape(B, S, W)
    y = y + d_skip.astype(f32) * u32.reshape(B, S, W)
    z = jax.nn.gelu(y)
    out = z * jax.nn.sigmoid(z @ w_glu.astype(f32) + b_glu.astype(f32))
    return out.astype(u.dtype)


def _mlstm_chunkwise(q, k, v, ig, lf):
    B, H, S, dk = q.shape
    dv = v.shape[-1]
    T = MLSTM_CHUNK
    NC = S // T
    f32 = jnp.float32

    def chunks(a):
        return jnp.moveaxis(a.astype(f32).reshape(a.shape[:2] + (NC, T) + a.shape[3:]), 2, 0)

    xs = (chunks(q), chunks(k), chunks(v), chunks(ig), chunks(lf))
    tri = jnp.tril(jnp.ones((T, T), dtype=bool))

    def step(carry, inp):
        C, n, m = carry
        qc, kc, vc, ic, fc = inp
        bcum = jnp.cumsum(fc, axis=-1)
        Dm = bcum[..., :, None] - bcum[..., None, :] + ic[..., None, :]
        Dm = jnp.where(tri, Dm, -jnp.inf)
        g = bcum + m[..., None]
        mt = jnp.maximum(g, jnp.max(Dm, axis=-1))
        w_intra = jnp.exp(Dm - mt[..., None])
        w_inter = jnp.exp(g - mt)
        s = jnp.einsum('bhtd,bhsd->bhts', qc, kc) * w_intra
        num = w_inter[..., None] * jnp.einsum('bhtd,bhde->bhte', qc, C) + jnp.einsum('bhts,bhse->bhte', s, vc)
        den = w_inter * jnp.einsum('bhtd,bhd->bht', qc, n) + jnp.sum(s, axis=-1)
        h = num / jnp.maximum(jnp.abs(den), jnp.exp(-mt))[..., None]
        bT = bcum[..., -1]
        dec = bT[..., None] - bcum + ic
        m_new = jnp.maximum(bT + m, jnp.max(dec, axis=-1))
        w_s = jnp.exp(dec - m_new[..., None])
        carry_w = jnp.exp(bT + m - m_new)
        C_new = carry_w[..., None, None] * C + jnp.einsum('bhs,bhsd,bhse->bhde', w_s, kc, vc)
        n_new = carry_w[..., None] * n + jnp.einsum('bhs,bhsd->bhd', w_s, kc)
        return (C_new, n_new, m_new), h

    init = (jnp.zeros((B, H, dk, dv), f32), jnp.zeros((B, H, dk), f32), jnp.zeros((B, H), f32))
    _, hs = lax.scan(step, init, xs)
    return jnp.moveaxis(hs, 0, 2).reshape(B, H, S, dv)


def _forgetting_attention(q, k, v, logf):
    B, H, S, d = q.shape
    scale = d ** -0.5
    F = jnp.cumsum(logf, axis=-1)
    kpos = jnp.arange(S)

    def block(i):
        qb = lax.dynamic_slice_in_dim(q, i * Q_BLOCK, Q_BLOCK, axis=2)
        Fq = lax.dynamic_slice_in_dim(F, i * Q_BLOCK, Q_BLOCK, axis=2)
        s = jnp.einsum('bhqd,bhkd->bhqk', qb, k).astype(jnp.float32) * scale
        s = s + Fq[..., :, None] - F[..., None, :]
        qpos = i * Q_BLOCK + jnp.arange(Q_BLOCK)
        s = jnp.where(kpos[None, :] <= qpos[:, None], s, -jnp.inf)
        p = jax.nn.softmax(s, axis=-1)
        return jnp.einsum('bhqk,bhkd->bhqd', p.astype(v.dtype), v)

    o = lax.map(block, jnp.arange(S // Q_BLOCK))
    o = jnp.moveaxis(o, 0, 2).reshape(B, H, S, d)
    return o.transpose(0, 2, 1, 3).reshape(B, S, H * d)


def _mixer_diff_s5(h, w_in, lq1, lk1, lq2, lk2, subln, a_re, a_im, log_dt, b_re, b_im,
                   c_re, c_im, d_skip, w_glu, b_glu, w_out):
    B, S, _ = h.shape
    proj = h @ w_in
    q, k, v, u = jnp.split(proj, [DIFF_QK_COLS, 2 * DIFF_QK_COLS, 2 * DIFF_QK_COLS + DIFF_WIDTH], axis=-1)
    q = q.reshape(B, S, DIFF_HEADS, 2, DIFF_HEAD_DIM).transpose(0, 2, 3, 1, 4)
    k = k.reshape(B, S, DIFF_HEADS, 2, DIFF_HEAD_DIM).transpose(0, 2, 3, 1, 4)
    v = v.reshape(B, S, DIFF_HEADS, DIFF_V_DIM).transpose(0, 2, 1, 3)
    cos, sin = _rope_tables(S)
    q = _partial_rope(q, cos, sin)
    k = _partial_rope(k, cos, sin)
    f32 = jnp.float32
    lam = (jnp.exp(jnp.sum(lq1.astype(f32) * lk1.astype(f32)))
           - jnp.exp(jnp.sum(lq2.astype(f32) * lk2.astype(f32))) + DIFF_LAMBDA_INIT)
    ya = _diff_attention(q, k, v, lam, subln)
    yb = _s5(u, a_re, a_im, log_dt, b_re, b_im, c_re, c_im, d_skip, w_glu, b_glu)
    return jnp.concatenate([ya, yb], axis=-1) @ w_out


def _mixer_mlstm_fox(h, w_in, conv_w, conv_b, b_i, b_f, out_norm, fox_b_f, w_out):
    B, S, _ = h.shape
    proj = h @ w_in
    sizes = [MLSTM_QK_COLS, MLSTM_WIDTH, MLSTM_HEADS, MLSTM_HEADS, MLSTM_WIDTH,
             FOX_WIDTH, FOX_WIDTH, FOX_WIDTH]
    idx, acc = [], 0
    for sz in sizes:
        acc += sz
        idx.append(acc)
    qk_m, v_m, i_m, f_m, o_m, q_f, k_f, v_f, f_f = jnp.split(proj, idx, axis=-1)
    f32 = jnp.float32
    qk_m = jax.nn.silu(_causal_conv(qk_m, conv_w, conv_b))
    q_m, k_m = jnp.split(qk_m, 2, axis=-1)
    q_m = q_m.reshape(B, S, MLSTM_HEADS, MLSTM_QK_DIM).transpose(0, 2, 1, 3) * (MLSTM_QK_DIM ** -0.5)
    k_m = k_m.reshape(B, S, MLSTM_HEADS, MLSTM_QK_DIM).transpose(0, 2, 1, 3)
    v_m = v_m.reshape(B, S, MLSTM_HEADS, MLSTM_V_DIM).transpose(0, 2, 1, 3)
    ig = (i_m.astype(f32) + b_i.astype(f32)).transpose(0, 2, 1)
    lf = jax.nn.log_sigmoid(f_m.astype(f32) + b_f.astype(f32)).transpose(0, 2, 1)
    hm = _mlstm_chunkwise(q_m, k_m, v_m, ig, lf).astype(h.dtype)
    hm = _rmsnorm(hm.transpose(0, 2, 1, 3), out_norm).reshape(B, S, MLSTM_WIDTH)
    hm = jax.nn.sigmoid(o_m) * hm
    q_f = q_f.reshape(B, S, FOX_HEADS, FOX_HEAD_DIM).transpose(0, 2, 1, 3)
    k_f = k_f.reshape(B, S, FOX_HEADS, FOX_HEAD_DIM).transpose(0, 2, 1, 3)
    v_f = v_f.reshape(B, S, FOX_HEADS, FOX_HEAD_DIM).transpose(0, 2, 1, 3)
    logf = jax.nn.log_sigmoid(f_f.astype(f32) + fox_b_f.astype(f32)).transpose(0, 2, 1)
    hf = _forgetting_attention(q_f, k_f, v_f, logf)
    return jnp.concatenate([hm, hf], axis=-1) @ w_out


def setup_inputs(seed: int = 0) -> dict:
    key = jax.random.key(seed)
    keys = iter(jax.random.split(key, 64))
    f32 = jnp.float32

    def nrm(shape, scale):
        return jax.random.normal(next(keys), shape, f32) * scale

    def gain(shape):
        return 1.0 + nrm(shape, 0.02)

    def ffn():
        return (gain((D_MODEL,)), nrm((D_MODEL, D_FF), D_MODEL ** -0.5),
                nrm((D_MODEL, D_FF), D_MODEL ** -0.5), nrm((D_FF, D_MODEL), D_FF ** -0.5))

    inp = {}
    inp['x'] = nrm((BATCH, SEQ, D_MODEL), 1.0)
    g, wg, wu, wd = ffn()
    inp['l0_ffn1_norm'], inp['l0_ffn1_w_gate'], inp['l0_ffn1_w_up'], inp['l0_ffn1_w_down'] = g, wg, wu, wd
    inp['l0_mix_norm'] = gain((D_MODEL,))
    inp['l0_w_in'] = nrm((D_MODEL, L0_IN), D_MODEL ** -0.5)
    inp['l0_diff_lambda_q1'] = nrm((DIFF_HEAD_DIM,), 0.1)
    inp['l0_diff_lambda_k1'] = nrm((DIFF_HEAD_DIM,), 0.1)
    inp['l0_diff_lambda_q2'] = nrm((DIFF_HEAD_DIM,), 0.1)
    inp['l0_diff_lambda_k2'] = nrm((DIFF_HEAD_DIM,), 0.1)
    inp['l0_diff_subln'] = gain((DIFF_V_DIM,))
    inp['l0_s5_a_re'] = -0.5 + nrm((S5_GROUPS, S5_STATE), 0.01)
    inp['l0_s5_a_im'] = math.pi * jnp.arange(S5_STATE, dtype=f32)[None, :] + nrm((S5_GROUPS, S5_STATE), 0.01)
    inp['l0_s5_log_dt'] = jax.random.uniform(next(keys), (S5_GROUPS,), f32, math.log(1e-3), math.log(1e-1))
    inp['l0_s5_b_re'] = nrm((S5_GROUPS, S5_STATE, S5_GROUP), (2 * S5_GROUP) ** -0.5)
    inp['l0_s5_b_im'] = nrm((S5_GROUPS, S5_STATE, S5_GROUP), (2 * S5_GROUP) ** -0.5)
    inp['l0_s5_c_re'] = nrm((S5_GROUPS, S5_GROUP, S5_STATE), (2 * S5_STATE) ** -0.5)
    inp['l0_s5_c_im'] = nrm((S5_GROUPS, S5_GROUP, S5_STATE), (2 * S5_STATE) ** -0.5)
    inp['l0_s5_d'] = nrm((S5_WIDTH,), 1.0)
    inp['l0_s5_w_glu'] = nrm((S5_WIDTH, S5_WIDTH), S5_WIDTH ** -0.5)
    inp['l0_s5_b_glu'] = nrm((S5_WIDTH,), 0.01)
    inp['l0_w_out'] = nrm((MIX_WIDTH, D_MODEL), MIX_WIDTH ** -0.5)
    g, wg, wu, wd = ffn()
    inp['l0_ffn2_norm'], inp['l0_ffn2_w_gate'], inp['l0_ffn2_w_up'], inp['l0_ffn2_w_down'] = g, wg, wu, wd
    g, wg, wu, wd = ffn()
    inp['l1_ffn1_norm'], inp['l1_ffn1_w_gate'], inp['l1_ffn1_w_up'], inp['l1_ffn1_w_down'] = g, wg, wu, wd
    inp['l1_mix_norm'] = gain((D_MODEL,))
    inp['l1_w_in'] = nrm((D_MODEL, L1_IN), D_MODEL ** -0.5)
    inp['l1_mlstm_conv_w'] = nrm((MLSTM_CONV, MLSTM_QK_COLS), MLSTM_CONV ** -0.5)
    inp['l1_mlstm_conv_b'] = nrm((MLSTM_QK_COLS,), 0.01)
    inp['l1_mlstm_b_i'] = nrm((MLSTM_HEADS,), 0.1)
    inp['l1_mlstm_b_f'] = jnp.linspace(3.0, 6.0, MLSTM_HEADS, dtype=f32) + nrm((MLSTM_HEADS,), 0.1)
    inp['l1_mlstm_out_norm'] = gain((MLSTM_HEADS, MLSTM_V_DIM))
    inp['l1_fox_b_f'] = jnp.linspace(1.0, 4.0, FOX_HEADS, dtype=f32) + nrm((FOX_HEADS,), 0.1)
    inp['l1_w_out'] = nrm((MIX_WIDTH, D_MODEL), MIX_WIDTH ** -0.5)
    g, wg, wu, wd = ffn()
    inp['l1_ffn2_norm'], inp['l1_ffn2_w_gate'], inp['l1_ffn2_w_up'], inp['l1_ffn2_w_down'] = g, wg, wu, wd
    inp['final_norm'] = gain((D_MODEL,))
    return inp


def reference(x,
              l0_ffn1_norm, l0_ffn1_w_gate, l0_ffn1_w_up, l0_ffn1_w_down,
              l0_mix_norm, l0_w_in,
              l0_diff_lambda_q1, l0_diff_lambda_k1, l0_diff_lambda_q2, l0_diff_lambda_k2, l0_diff_subln,
              l0_s5_a_re, l0_s5_a_im, l0_s5_log_dt, l0_s5_b_re, l0_s5_b_im, l0_s5_c_re, l0_s5_c_im,
              l0_s5_d, l0_s5_w_glu, l0_s5_b_glu,
              l0_w_out,
              l0_ffn2_norm, l0_ffn2_w_gate, l0_ffn2_w_up, l0_ffn2_w_down,
              l1_ffn1_norm, l1_ffn1_w_gate, l1_ffn1_w_up, l1_ffn1_w_down,
              l1_mix_norm, l1_w_in,
              l1_mlstm_conv_w, l1_mlstm_conv_b, l1_mlstm_b_i, l1_mlstm_b_f, l1_mlstm_out_norm,
              l1_fox_b_f,
              l1_w_out,
              l1_ffn2_norm, l1_ffn2_w_gate, l1_ffn2_w_up, l1_ffn2_w_down,
              final_norm):
    ffn1 = [(l0_ffn1_norm, l0_ffn1_w_gate, l0_ffn1_w_up, l0_ffn1_w_down),
            (l1_ffn1_norm, l1_ffn1_w_gate, l1_ffn1_w_up, l1_ffn1_w_down)]
    ffn2 = [(l0_ffn2_norm, l0_ffn2_w_gate, l0_ffn2_w_up, l0_ffn2_w_down),
            (l1_ffn2_norm, l1_ffn2_w_gate, l1_ffn2_w_up, l1_ffn2_w_down)]
    mix_norms = [l0_mix_norm, l1_mix_norm]

    def mixer_even(h):
        return _mixer_diff_s5(h, l0_w_in, l0_diff_lambda_q1, l0_diff_lambda_k1, l0_diff_lambda_q2,
                              l0_diff_lambda_k2, l0_diff_subln, l0_s5_a_re, l0_s5_a_im, l0_s5_log_dt,
                              l0_s5_b_re, l0_s5_b_im, l0_s5_c_re, l0_s5_c_im, l0_s5_d, l0_s5_w_glu,
                              l0_s5_b_glu, l0_w_out)

    def mixer_odd(h):
        return _mixer_mlstm_fox(h, l1_w_in, l1_mlstm_conv_w, l1_mlstm_conv_b, l1_mlstm_b_i,
                                l1_mlstm_b_f, l1_mlstm_out_norm, l1_fox_b_f, l1_w_out)

    mixers = [mixer_even, mixer_odd]
    for layer in range(DEPTH):
        g1, wg1, wu1, wd1 = ffn1[layer]
        x = x + 0.5 * _swiglu(_rmsnorm(x, g1), wg1, wu1, wd1)
        x = x + mixers[layer](_rmsnorm(x, mix_norms[layer]))
        g2, wg2, wu2, wd2 = ffn2[layer]
        x = x + 0.5 * _swiglu(_rmsnorm(x, g2), wg2, wu2, wd2)
    return _rmsnorm(x, final_norm)
```

```python
import functools
import math

import jax
import jax.numpy as jnp
from jax import lax
from jax.experimental import pallas as pl
from jax.experimental.pallas import tpu as pltpu

F32 = jnp.float32
BF16 = jnp.bfloat16

D_MODEL = 1024
D_FF = 2816
NORM_EPS = 1e-6
ROPE_THETA = 500000.0
DIFF_HEADS = 4
DIFF_HEAD_DIM = 64
ROT_DIM = DIFF_HEAD_DIM // 4
DIFF_LAMBDA_INIT = 0.8 - 0.6 * math.exp(-0.3 * 0)
S5_GROUPS = 32
S5_GROUP = 16
S5_STATE = 64
S5_NSTATE = S5_GROUPS * S5_STATE
MLSTM_HEADS = 4
MLSTM_QK_DIM = 64
MLSTM_V_DIM = 128
MLSTM_CONV = 4
FOX_HEADS = 8
FOX_HEAD_DIM = 64
HALF = 512

LANES = 128
SUBLANES = 8
VMEM_LIMIT_BYTES = 60 * 1024 * 1024

TOKEN_TILE = 512
FF_CHUNK = 256
ATTN_TILE = 512
S5_TILE = 256
MLSTM_CHUNK = 256
GATE_COLS = LANES

NT_DIMS = (((1,), (1,)), ((), ()))


def _params(*semantics):
    return pltpu.CompilerParams(dimension_semantics=semantics, vmem_limit_bytes=VMEM_LIMIT_BYTES)


def _resident(shape):
    return pl.BlockSpec(shape, lambda *_: (0,) * len(shape), pipeline_mode=pl.Buffered(1))


def _rms(x, g):
    return x * lax.rsqrt(jnp.mean(x * x, axis=-1, keepdims=True) + NORM_EPS) * g


def _log_sigmoid(x):
    return jnp.minimum(x, 0.0) - jnp.log1p(jnp.exp(-jnp.abs(x)))


def _cumsum_rows(x):
    n = x.shape[0]
    row = lax.broadcasted_iota(jnp.int32, x.shape, 0)
    k = 1
    while k < n:
        x = x + jnp.where(row >= k, pltpu.roll(x, k, 0), 0.0)
        k *= 2
    return x


def _causal_pairs(n_blocks):
    ii, jj = [], []
    for i in range(n_blocks):
        for j in range(i + 1):
            ii.append(i)
            jj.append(j)
    return jnp.asarray(ii, jnp.int32), jnp.asarray(jj, jnp.int32)


def _ffn_body(*refs, has_mix, has_final):
    refs = list(refs)
    x_ref = refs.pop(0)
    if has_mix:
        a_ref, b_ref, wa_ref, wb_ref = refs[:4]
        refs = refs[4:]
    g_ref, wg_ref, wu_ref, wd_ref = refs[:4]
    refs = refs[4:]
    if has_final:
        gf_ref = refs.pop(0)
    o_ref = refs.pop(0)

    x = x_ref[...]
    if has_mix:
        x = x + jnp.dot(a_ref[...], wa_ref[...], preferred_element_type=F32)
        x = x + jnp.dot(b_ref[...], wb_ref[...], preferred_element_type=F32)
    h = _rms(x, g_ref[...]).astype(BF16)
    acc = jnp.zeros_like(x)
    for c in range(D_FF // FF_CHUNK):
        sl = slice(c * FF_CHUNK, (c + 1) * FF_CHUNK)
        gate = jnp.dot(h, wg_ref[:, sl], preferred_element_type=F32)
        up = jnp.dot(h, wu_ref[:, sl], preferred_element_type=F32)
        act = (gate * jax.nn.sigmoid(gate) * up).astype(BF16)
        acc = acc + jnp.dot(act, wd_ref[sl, :], preferred_element_type=F32)
    y = x + 0.5 * acc
    if has_final:
        y = _rms(y, gf_ref[...])
    o_ref[...] = y


def _ffn(x, norm, wg, wu, wd, mix=None, final_norm=None):
    n, d = x.shape
    tm = min(TOKEN_TILE, n)
    row = lambda i: (i, 0)
    args = [x]
    specs = [pl.BlockSpec((tm, d), row)]
    if mix is not None:
        a, b, wa, wb = mix
        args += [a, b, wa.astype(BF16), wb.astype(BF16)]
        specs += [pl.BlockSpec((tm, HALF), row), pl.BlockSpec((tm, HALF), row),
                  _resident((HALF, d)), _resident((HALF, d))]
    args += [norm.reshape(1, d), wg.astype(BF16), wu.astype(BF16), wd.astype(BF16)]
    specs += [_resident((1, d)), _resident((d, D_FF)), _resident((d, D_FF)), _resident((D_FF, d))]
    if final_norm is not None:
        args.append(final_norm.reshape(1, d))
        specs.append(_resident((1, d)))
    body = functools.partial(_ffn_body, has_mix=mix is not None, has_final=final_norm is not None)
    return pl.pallas_call(
        body,
        grid=(n // tm,),
        in_specs=specs,
        out_specs=pl.BlockSpec((tm, d), row),
        out_shape=jax.ShapeDtypeStruct((n, d), F32),
        compiler_params=_params("parallel"),
        name="ffn",
    )(*args)


def _rope_lane_tables(seq):
    half = ROT_DIM // 2
    pos = jnp.arange(seq, dtype=F32)
    inv = ROPE_THETA ** (-jnp.arange(0, ROT_DIM, 2, dtype=F32) / ROT_DIM)
    ang = pos[:, None] * inv[None, :]
    cos, sin = jnp.cos(ang), jnp.sin(ang)
    one = jnp.ones((seq, DIFF_HEAD_DIM - ROT_DIM), F32)
    zero8 = jnp.zeros((seq, half), F32)
    zero48 = jnp.zeros((seq, DIFF_HEAD_DIM - ROT_DIM), F32)
    cos_t = jnp.concatenate([cos, cos, one] * 2, axis=1)
    sa_t = jnp.concatenate([-sin, zero8, zero48] * 2, axis=1)
    sb_t = jnp.concatenate([zero8, sin, zero48] * 2, axis=1)
    return cos_t, sa_t, sb_t


def _proj0_body(x_ref, g_ref, w_ref, cos_ref, sa_ref, sb_ref, q_ref, k_ref, v_ref, u_ref):
    h = _rms(x_ref[...], g_ref[...]).astype(BF16)
    p = jnp.dot(h, w_ref[...], preferred_element_type=F32)
    cos_t, sa_t, sb_t = cos_ref[...], sa_ref[...], sb_ref[...]
    half = ROT_DIM // 2

    def rope(t):
        return t * cos_t + pltpu.roll(t, LANES - half, 1) * sa_t + pltpu.roll(t, half, 1) * sb_t

    scale = DIFF_HEAD_DIM ** -0.5
    for hd in range(DIFF_HEADS):
        sl = slice(hd * LANES, (hd + 1) * LANES)
        q_ref[:, sl] = (rope(p[:, sl]) * scale).astype(BF16)
        k_ref[:, sl] = rope(p[:, HALF + hd * LANES:HALF + (hd + 1) * LANES]).astype(BF16)
    v_ref[...] = p[:, 2 * HALF:3 * HALF].astype(BF16)
    u_ref[...] = p[:, 3 * HALF:4 * HALF]


def _proj0(x, norm, w_in, seq):
    n, d = x.shape
    tm = min(TOKEN_TILE, seq)
    tiles_per_seq = seq // tm
    row = lambda i: (i, 0)
    pos = lambda i: (i % tiles_per_seq, 0)
    tables = _rope_lane_tables(seq)
    half_bf = jax.ShapeDtypeStruct((n, HALF), BF16)
    return pl.pallas_call(
        _proj0_body,
        grid=(n // tm,),
        in_specs=[pl.BlockSpec((tm, d), row), _resident((1, d)), _resident((d, 4 * HALF)),
                  pl.BlockSpec((tm, LANES), pos), pl.BlockSpec((tm, LANES), pos),
                  pl.BlockSpec((tm, LANES), pos)],
        out_specs=[pl.BlockSpec((tm, HALF), row)] * 4,
        out_shape=[half_bf, half_bf, half_bf, jax.ShapeDtypeStruct((n, HALF), F32)],
        compiler_params=_params("parallel"),
        name="proj0",
    )(x, norm.reshape(1, d), w_in.astype(BF16), *tables)


def _diff_attn_body(ii_ref, jj_ref, lam_ref, subln_ref, q_ref, k_ref, v_ref, o_ref,
                    m_ref, l_ref, acc_ref, *, tile):
    p_idx = pl.program_id(2)
    i = ii_ref[p_idx]
    j = jj_ref[p_idx]

    @pl.when(j == 0)
    def _():
        m_ref[...] = jnp.full_like(m_ref, -jnp.inf)
        l_ref[...] = jnp.zeros_like(l_ref)
        acc_ref[...] = jnp.zeros_like(acc_ref)

    q = q_ref[0]
    k = k_ref[0]
    v = v_ref[0]
    lane = lax.broadcasted_iota(jnp.int32, (1, LANES), 1)
    row = i * tile + lax.broadcasted_iota(jnp.int32, (tile, tile), 0)
    col = j * tile + lax.broadcasted_iota(jnp.int32, (tile, tile), 1)
    keep = col <= row
    for c in range(2):
        sel = (lane < DIFF_HEAD_DIM) if c == 0 else (lane >= DIFF_HEAD_DIM)
        qc = jnp.where(sel, q, jnp.zeros_like(q))
        s = lax.dot_general(qc, k, NT_DIMS, preferred_element_type=F32)
        s = jnp.where(keep, s, -jnp.inf)
        m_old = m_ref[c]
        m_new = jnp.maximum(m_old, jnp.max(s, axis=-1, keepdims=True))
        alpha = jnp.exp(m_old - m_new)
        p = jnp.exp(s - m_new)
        l_ref[c] = alpha * l_ref[c] + jnp.sum(p, axis=-1, keepdims=True)
        acc_ref[c] = alpha * acc_ref[c] + jnp.dot(p.astype(BF16), v, preferred_element_type=F32)
        m_ref[c] = m_new

    @pl.when(j == i)
    def _():
        lam_v = lam_ref[...]
        lam = (jnp.exp(jnp.sum(lam_v[0:1] * lam_v[1:2], keepdims=True))
               - jnp.exp(jnp.sum(lam_v[2:3] * lam_v[3:4], keepdims=True)) + DIFF_LAMBDA_INIT)
        o = acc_ref[0] / l_ref[0] - lam * (acc_ref[1] / l_ref[1])
        o = _rms(o, subln_ref[...]) * (1.0 - DIFF_LAMBDA_INIT)
        o_ref[0] = o.astype(BF16)


def _diff_attention(q, k, v, lam_vecs, subln):
    b, s, _ = q.shape
    t = min(ATTN_TILE, s)
    ii, jj = _causal_pairs(s // t)
    q_map = lambda bi, h, p, ii, jj: (bi, ii[p], h)
    k_map = lambda bi, h, p, ii, jj: (bi, jj[p], h)
    const = lambda bi, h, p, ii, jj: (0, 0)
    blk = (1, t, LANES)
    grid_spec = pltpu.PrefetchScalarGridSpec(
        num_scalar_prefetch=2,
        grid=(b, DIFF_HEADS, int(ii.shape[0])),
        in_specs=[pl.BlockSpec((4, DIFF_HEAD_DIM), const), pl.BlockSpec((1, LANES), const),
                  pl.BlockSpec(blk, q_map), pl.BlockSpec(blk, k_map), pl.BlockSpec(blk, k_map)],
        out_specs=pl.BlockSpec(blk, q_map),
        scratch_shapes=[pltpu.VMEM((2, t, 1), F32), pltpu.VMEM((2, t, 1), F32),
                        pltpu.VMEM((2, t, LANES), F32)],
    )
    return pl.pallas_call(
        functools.partial(_diff_attn_body, tile=t),
        grid_spec=grid_spec,
        out_shape=jax.ShapeDtypeStruct((b, s, HALF), BF16),
        compiler_params=_params("parallel", "parallel", "arbitrary"),
        name="diff_attn",
    )(ii, jj, lam_vecs, subln.reshape(1, LANES), q, k, v)


def _s5_discretize_body(lr_ref, li_ref, ldt_ref, br_ref, bi_ref, ar_ref, ai_ref, bbr_ref, bbi_ref):
    lr, li = lr_ref[...], li_ref[...]
    dt = jnp.exp(ldt_ref[...])
    mag = jnp.exp(lr * dt)
    ar, ai = mag * jnp.cos(li * dt), mag * jnp.sin(li * dt)
    den = lr * lr + li * li
    gr = ((ar - 1.0) * lr + ai * li) / den
    gi = (ai * lr - (ar - 1.0) * li) / den
    br, bi = br_ref[...], bi_ref[...]
    ar_ref[...] = ar
    ai_ref[...] = ai
    bbr_ref[...] = gr * br - gi * bi
    bbi_ref[...] = gr * bi + gi * br


def _s5_discretize(a_re, a_im, log_dt, b_re, b_im):
    col = lambda a: a.reshape(S5_NSTATE, 1)
    ldt = jnp.broadcast_to(log_dt[:, None], (S5_GROUPS, S5_STATE))
    cshape = jax.ShapeDtypeStruct((S5_NSTATE, 1), F32)
    bshape = jax.ShapeDtypeStruct((S5_NSTATE, S5_GROUP), F32)
    return pl.pallas_call(
        _s5_discretize_body,
        out_shape=[cshape, cshape, bshape, bshape],
        name="s5_discretize",
    )(col(a_re), col(a_im), col(ldt), b_re.reshape(S5_NSTATE, S5_GROUP), b_im.reshape(S5_NSTATE, S5_GROUP))


def _s5_body(u_ref, ar_ref, ai_ref, bbr_ref, bbi_ref, cr_ref, ci_ref, d_ref, wglu_ref, bglu_ref,
             o_ref, xr_ref, xi_ref, h_ref, *, tile):
    @pl.when(pl.program_id(1) == 0)
    def _():
        h_ref[...] = jnp.zeros_like(h_ref)

    u = u_ref[0]
    ub = u.astype(BF16)
    xr_ref[...] = jnp.dot(ub, bbr_ref[...], preferred_element_type=F32)
    xi_ref[...] = jnp.dot(ub, bbi_ref[...], preferred_element_type=F32)
    ar, ai = ar_ref[...], ai_ref[...]

    def step(t, carry):
        hr, hi = carry
        nhr = ar * hr - ai * hi + xr_ref[pl.ds(t, 1), :]
        nhi = ar * hi + ai * hr + xi_ref[pl.ds(t, 1), :]
        xr_ref[pl.ds(t, 1), :] = nhr
        xi_ref[pl.ds(t, 1), :] = nhi
        return nhr, nhi

    hr, hi = lax.fori_loop(0, tile, step, (h_ref[0:1, :], h_ref[1:2, :]), unroll=8)
    h_ref[0:1, :] = hr
    h_ref[1:2, :] = hi

    y = (jnp.dot(xr_ref[...].astype(BF16), cr_ref[...], preferred_element_type=F32)
         - jnp.dot(xi_ref[...].astype(BF16), ci_ref[...], preferred_element_type=F32))
    y = y + d_ref[...] * u
    z = jax.nn.gelu(y, approximate=True)
    gate = jnp.dot(z.astype(BF16), wglu_ref[...], preferred_element_type=F32) + bglu_ref[...]
    o_ref[0] = (z * jax.nn.sigmoid(gate)).astype(BF16)


def _s5(u, a_re, a_im, log_dt, b_re, b_im, c_re, c_im, d_skip, w_glu, b_glu):
    b, s, w = u.shape
    t = min(S5_TILE, s)
    ar, ai, bbr, bbi = _s5_discretize(a_re, a_im, log_dt, b_re, b_im)
    eye = jnp.eye(S5_GROUPS, dtype=F32)

    def block_in(m):
        m = m.reshape(S5_GROUPS, S5_STATE, S5_GROUP)
        return jnp.einsum('gpc,gh->gchp', m, eye).reshape(w, S5_NSTATE).astype(BF16)

    def block_out(m):
        return jnp.einsum('gcp,gh->gphc', m, eye).reshape(S5_NSTATE, w).astype(BF16)

    seq_blk = lambda bi, ti: (bi, ti, 0)
    return pl.pallas_call(
        functools.partial(_s5_body, tile=t),
        grid=(b, s // t),
        in_specs=[pl.BlockSpec((1, t, w), seq_blk),
                  _resident((1, S5_NSTATE)), _resident((1, S5_NSTATE)),
                  _resident((w, S5_NSTATE)), _resident((w, S5_NSTATE)),
                  _resident((S5_NSTATE, w)), _resident((S5_NSTATE, w)),
                  _resident((1, w)), _resident((w, w)), _resident((1, w))],
        out_specs=pl.BlockSpec((1, t, w), seq_blk),
        out_shape=jax.ShapeDtypeStruct((b, s, w), BF16),
        scratch_shapes=[pltpu.VMEM((t, S5_NSTATE), F32), pltpu.VMEM((t, S5_NSTATE), F32),
                        pltpu.VMEM((2, S5_NSTATE), F32)],
        compiler_params=_params("parallel", "arbitrary"),
        name="s5",
    )(u, ar.reshape(1, S5_NSTATE), ai.reshape(1, S5_NSTATE), block_in(bbr), block_in(bbi),
      block_out(c_re), block_out(c_im), d_skip.reshape(1, w), w_glu.astype(BF16), b_glu.reshape(1, w))


def _proj1_body(x_ref, g_ref, w_ref, qk_ref, vm_ref, om_ref, qf_ref, kf_ref, vf_ref, gt_ref):
    h = _rms(x_ref[...], g_ref[...]).astype(BF16)
    p = jnp.dot(h, w_ref[...], preferred_element_type=F32)
    qk_ref[...] = p[:, 0:HALF]
    vm_ref[...] = p[:, HALF:2 * HALF].astype(BF16)
    om_ref[...] = p[:, 2 * HALF:3 * HALF]
    qf_ref[...] = (p[:, 3 * HALF:4 * HALF] * FOX_HEAD_DIM ** -0.5).astype(BF16)
    kf_ref[...] = p[:, 4 * HALF:5 * HALF].astype(BF16)
    vf_ref[...] = p[:, 5 * HALF:6 * HALF].astype(BF16)
    gt_ref[...] = p[:, 6 * HALF:6 * HALF + GATE_COLS]


def _proj1(x, norm, w_in):
    n, d = x.shape
    tm = min(TOKEN_TILE, n)
    row = lambda i: (i, 0)
    o = 2 * HALF
    g = 2 * MLSTM_HEADS
    w = jnp.concatenate(
        [w_in[:, :o], w_in[:, o + g:o + g + 4 * HALF], w_in[:, o:o + g], w_in[:, o + g + 4 * HALF:],
         jnp.zeros((d, GATE_COLS - g - FOX_HEADS), w_in.dtype)], axis=1).astype(BF16)
    f32_half = jax.ShapeDtypeStruct((n, HALF), F32)
    bf_half = jax.ShapeDtypeStruct((n, HALF), BF16)
    half_spec = pl.BlockSpec((tm, HALF), row)
    return pl.pallas_call(
        _proj1_body,
        grid=(n // tm,),
        in_specs=[pl.BlockSpec((tm, d), row), _resident((1, d)), _resident((d, 6 * HALF + GATE_COLS))],
        out_specs=[half_spec] * 6 + [pl.BlockSpec((tm, GATE_COLS), row)],
        out_shape=[f32_half, bf_half, f32_half, bf_half, bf_half, bf_half,
                   jax.ShapeDtypeStruct((n, GATE_COLS), F32)],
        compiler_params=_params("parallel"),
        name="proj1",
    )(x, norm.reshape(1, d), w)


def _mlstm_body(qk_ref, v_ref, og_ref, gt_ref, cw_ref, cb_ref, gb_ref, on_ref, out_ref,
                xbuf_ref, cn_ref, m_ref, *, chunk):
    pad = SUBLANES
    dk, dv = MLSTM_QK_DIM, MLSTM_V_DIM

    @pl.when(pl.program_id(1) == 0)
    def _():
        xbuf_ref[0:pad, :] = jnp.zeros((pad, HALF), F32)
        cn_ref[...] = jnp.zeros_like(cn_ref)
        m_ref[...] = jnp.zeros_like(m_ref)

    x = qk_ref[0]
    xbuf_ref[pad:pad + chunk, :] = x
    a = jnp.zeros_like(x) + cb_ref[...]
    for tap in range(MLSTM_CONV):
        a = a + cw_ref[tap:tap + 1, :] * xbuf_ref[pl.ds(pad - (MLSTM_CONV - 1) + tap, chunk), :]
    xbuf_ref[0:pad, :] = x[chunk - pad:chunk, :]
    a = a * jax.nn.sigmoid(a)
    q = a[:, :MLSTM_HEADS * dk] * dk ** -0.5
    k = a[:, MLSTM_HEADS * dk:]
    k_t = k.T

    gi = gt_ref[0] + gb_ref[...]
    bcum = _cumsum_rows(_log_sigmoid(gi))
    gi_t = gi.T
    bcum_t = bcum.T
    tri = (lax.broadcasted_iota(jnp.int32, (chunk, chunk), 1)
           <= lax.broadcasted_iota(jnp.int32, (chunk, chunk), 0))
    lane = lax.broadcasted_iota(jnp.int32, (1, LANES), 1)
    v_all = v_ref[0]
    o_gate = og_ref[0]
    ones = jnp.ones((chunk, dv), BF16)

    for hd in range(MLSTM_HEADS):
        pair, odd = hd // 2, hd % 2
        f_lane = MLSTM_HEADS + hd
        b_col = bcum[:, f_lane:f_lane + 1]
        b_row = bcum_t[f_lane:f_lane + 1, :]
        i_row = gi_t[hd:hd + 1, :]
        m_prev = m_ref[hd]
        dm = jnp.where(tri, b_col - b_row + i_row, -jnp.inf)
        g = b_col + m_prev
        mt = jnp.maximum(g, jnp.max(dm, axis=-1, keepdims=True))
        w_intra = jnp.exp(dm - mt)
        w_inter = jnp.exp(g - mt)

        sel = (lane >= dk) if odd else (lane < dk)
        q2 = q[:, pair * LANES:(pair + 1) * LANES]
        q2 = jnp.where(sel, q2, 0.0).astype(BF16)
        k2 = k[:, pair * LANES:(pair + 1) * LANES].astype(BF16)
        s = lax.dot_general(q2, k2, NT_DIMS, preferred_element_type=F32) * w_intra
        v_aug = jnp.concatenate([v_all[:, hd * dv:(hd + 1) * dv], ones], axis=1)
        state = cn_ref[pair]
        q_state = jnp.dot(q2, state.astype(BF16), preferred_element_type=F32)
        s_v = jnp.dot(s.astype(BF16), v_aug, preferred_element_type=F32)
        num = w_inter * q_state[:, :dv] + s_v[:, :dv]
        den = w_inter * q_state[:, dv:] + s_v[:, dv:]
        h_out = num / jnp.maximum(jnp.abs(den), jnp.exp(-mt))

        b_last = bcum[chunk - 1:chunk, f_lane:f_lane + 1]
        dec = b_last - b_row + i_row
        m_new = jnp.maximum(b_last + m_prev, jnp.max(dec, axis=-1, keepdims=True))
        w_s = jnp.exp(dec - m_new)
        carry_w = jnp.exp(b_last + m_prev - m_new)
        k_w = (k_t[hd * dk:(hd + 1) * dk, :] * w_s).astype(BF16)
        rows = slice(odd * dk, (odd + 1) * dk)
        cn_ref[pair, rows, :] = carry_w * state[rows, :] + jnp.dot(k_w, v_aug, preferred_element_type=F32)
        m_ref[hd] = m_new

        cols = slice(hd * dv, (hd + 1) * dv)
        h_norm = _rms(h_out, on_ref[hd:hd + 1, :])
        out_ref[0, :, cols] = (jax.nn.sigmoid(o_gate[:, cols]) * h_norm).astype(BF16)


def _mlstm(qk, v, o_gate, gates, conv_w, conv_b, b_i, b_f, out_norm):
    b, s, _ = qk.shape
    c = min(MLSTM_CHUNK, s)
    gate_bias = jnp.concatenate([b_i, b_f, jnp.zeros((GATE_COLS - 2 * MLSTM_HEADS,), F32)]).reshape(1, GATE_COLS)
    seq_blk = lambda bi, ci: (bi, ci, 0)
    half_spec = pl.BlockSpec((1, c, HALF), seq_blk)
    return pl.pallas_call(
        functools.partial(_mlstm_body, chunk=c),
        grid=(b, s // c),
        in_specs=[half_spec, half_spec, half_spec, pl.BlockSpec((1, c, GATE_COLS), seq_blk),
                  _resident((MLSTM_CONV, HALF)), _resident((1, HALF)), _resident((1, GATE_COLS)),
                  _resident((MLSTM_HEADS, MLSTM_V_DIM))],
        out_specs=half_spec,
        out_shape=jax.ShapeDtypeStruct((b, s, HALF), BF16),
        scratch_shapes=[pltpu.VMEM((c + SUBLANES, HALF), F32),
                        pltpu.VMEM((MLSTM_HEADS // 2, 2 * MLSTM_QK_DIM, 2 * MLSTM_V_DIM), F32),
                        pltpu.VMEM((MLSTM_HEADS, 1, 1), F32)],
        compiler_params=_params("parallel", "arbitrary"),
        name="mlstm",
    )(qk, v, o_gate, gates, conv_w, conv_b.reshape(1, HALF), gate_bias, out_norm)


FOX_GATE_LANE = 2 * MLSTM_HEADS


def _fox_cumsum_body(gt_ref, gb_ref, fcol_ref, frow_ref):
    f = _cumsum_rows(_log_sigmoid(gt_ref[0] + gb_ref[...]))
    fcol_ref[0] = f
    frow_ref[0] = f.T[FOX_GATE_LANE:FOX_GATE_LANE + FOX_HEADS, :]


def _fox_cumsum(gates, fox_b_f):
    b, s, _ = gates.shape
    bias = jnp.concatenate([jnp.zeros((FOX_GATE_LANE,), F32), fox_b_f,
                            jnp.zeros((GATE_COLS - FOX_GATE_LANE - FOX_HEADS,), F32)]).reshape(1, GATE_COLS)
    return pl.pallas_call(
        _fox_cumsum_body,
        grid=(b,),
        in_specs=[pl.BlockSpec((1, s, GATE_COLS), lambda bi: (bi, 0, 0)), _resident((1, GATE_COLS))],
        out_specs=[pl.BlockSpec((1, s, GATE_COLS), lambda bi: (bi, 0, 0)),
                   pl.BlockSpec((1, FOX_HEADS, s), lambda bi: (bi, 0, 0))],
        out_shape=[jax.ShapeDtypeStruct((b, s, GATE_COLS), F32),
                   jax.ShapeDtypeStruct((b, FOX_HEADS, s), F32)],
        compiler_params=_params("parallel"),
        name="fox_cumsum",
    )(gates, bias)


def _fox_attn_body(ii_ref, jj_ref, q_ref, k_ref, v_ref, fcol_ref, frow_ref, o_ref,
                   m_ref, l_ref, fq_ref, acc_ref, *, tile):
    pair = pl.program_id(1)
    p_idx = pl.program_id(2)
    i = ii_ref[p_idx]
    j = jj_ref[p_idx]
    lane = lax.broadcasted_iota(jnp.int32, (1, LANES), 1)

    @pl.when(j == 0)
    def _():
        m_ref[...] = jnp.full_like(m_ref, -jnp.inf)
        l_ref[...] = jnp.zeros_like(l_ref)
        acc_ref[...] = jnp.zeros_like(acc_ref)
        fcol = fcol_ref[0]
        for c in range(2):
            pick = lane == FOX_GATE_LANE + 2 * pair + c
            fq_ref[c] = jnp.sum(jnp.where(pick, fcol, 0.0), axis=1, keepdims=True)

    q = q_ref[0]
    k = k_ref[0]
    v = v_ref[0]
    frow = frow_ref[0]
    head_row = lax.broadcasted_iota(jnp.int32, (FOX_HEADS, 1), 0)
    row = i * tile + lax.broadcasted_iota(jnp.int32, (tile, tile), 0)
    col = j * tile + lax.broadcasted_iota(jnp.int32, (tile, tile), 1)
    keep = col <= row
    low = lane < FOX_HEAD_DIM
    alphas, pvs = [], []
    for c in range(2):
        sel = low if c == 0 else jnp.logical_not(low)
        qc = jnp.where(sel, q, jnp.zeros_like(q))
        fk = jnp.sum(jnp.where(head_row == 2 * pair + c, frow, 0.0), axis=0, keepdims=True)
        s = lax.dot_general(qc, k, NT_DIMS, preferred_element_type=F32) + fq_ref[c] - fk
        s = jnp.where(keep, s, -jnp.inf)
        m_old = m_ref[c]
        m_new = jnp.maximum(m_old, jnp.max(s, axis=-1, keepdims=True))
        alpha = jnp.exp(m_old - m_new)
        p = jnp.exp(s - m_new)
        l_ref[c] = alpha * l_ref[c] + jnp.sum(p, axis=-1, keepdims=True)
        m_ref[c] = m_new
        alphas.append(alpha)
        pvs.append(jnp.dot(p.astype(BF16), v, preferred_element_type=F32))
    acc_ref[...] = jnp.where(low, alphas[0], alphas[1]) * acc_ref[...] + jnp.where(low, pvs[0], pvs[1])

    @pl.when(j == i)
    def _():
        o_ref[0] = (acc_ref[...] / jnp.where(low, l_ref[0], l_ref[1])).astype(BF16)


def _fox_attention(q, k, v, fcol, frow):
    b, s, _ = q.shape
    t = min(ATTN_TILE, s)
    ii, jj = _causal_pairs(s // t)
    q_map = lambda bi, h, p, ii, jj: (bi, ii[p], h)
    k_map = lambda bi, h, p, ii, jj: (bi, jj[p], h)
    blk = (1, t, LANES)
    grid_spec = pltpu.PrefetchScalarGridSpec(
        num_scalar_prefetch=2,
        grid=(b, FOX_HEADS // 2, int(ii.shape[0])),
        in_specs=[pl.BlockSpec(blk, q_map), pl.BlockSpec(blk, k_map), pl.BlockSpec(blk, k_map),
                  pl.BlockSpec((1, t, GATE_COLS), lambda bi, h, p, ii, jj: (bi, ii[p], 0)),
                  pl.BlockSpec((1, FOX_HEADS, t), lambda bi, h, p, ii, jj: (bi, 0, jj[p]))],
        out_specs=pl.BlockSpec(blk, q_map),
        scratch_shapes=[pltpu.VMEM((2, t, 1), F32), pltpu.VMEM((2, t, 1), F32),
                        pltpu.VMEM((2, t, 1), F32), pltpu.VMEM((t, LANES), F32)],
    )
    return pl.pallas_call(
        functools.partial(_fox_attn_body, tile=t),
        grid_spec=grid_spec,
        out_shape=jax.ShapeDtypeStruct((b, s, HALF), BF16),
        compiler_params=_params("parallel", "parallel", "arbitrary"),
        name="fox_attn",
    )(ii, jj, q, k, v, fcol, frow)


def kernel(x, l0_ffn1_norm, l0_ffn1_w_gate, l0_ffn1_w_up, l0_ffn1_w_down, l0_mix_norm, l0_w_in, l0_diff_lambda_q1, l0_diff_lambda_k1, l0_diff_lambda_q2, l0_diff_lambda_k2, l0_diff_subln, l0_s5_a_re, l0_s5_a_im, l0_s5_log_dt, l0_s5_b_re, l0_s5_b_im, l0_s5_c_re, l0_s5_c_im, l0_s5_d, l0_s5_w_glu, l0_s5_b_glu, l0_w_out, l0_ffn2_norm, l0_ffn2_w_gate, l0_ffn2_w_up, l0_ffn2_w_down, l1_ffn1_norm, l1_ffn1_w_gate, l1_ffn1_w_up, l1_ffn1_w_down, l1_mix_norm, l1_w_in, l1_mlstm_conv_w, l1_mlstm_conv_b, l1_mlstm_b_i, l1_mlstm_b_f, l1_mlstm_out_norm, l1_fox_b_f, l1_w_out, l1_ffn2_norm, l1_ffn2_w_gate, l1_ffn2_w_up, l1_ffn2_w_down, final_norm):
    b, s, d = x.shape
    n = b * s
    seq = lambda a: a.reshape(b, s, a.shape[-1])
    flat = lambda a: a.reshape(n, a.shape[-1])

    x = _ffn(x.reshape(n, d), l0_ffn1_norm, l0_ffn1_w_gate, l0_ffn1_w_up, l0_ffn1_w_down)
    q, k, v, u = _proj0(x, l0_mix_norm, l0_w_in, s)
    lam_vecs = jnp.stack([l0_diff_lambda_q1, l0_diff_lambda_k1, l0_diff_lambda_q2, l0_diff_lambda_k2])
    ya = _diff_attention(seq(q), seq(k), seq(v), lam_vecs, l0_diff_subln)
    yb = _s5(seq(u), l0_s5_a_re, l0_s5_a_im, l0_s5_log_dt, l0_s5_b_re, l0_s5_b_im,
             l0_s5_c_re, l0_s5_c_im, l0_s5_d, l0_s5_w_glu, l0_s5_b_glu)
    x = _ffn(x, l0_ffn2_norm, l0_ffn2_w_gate, l0_ffn2_w_up, l0_ffn2_w_down,
             mix=(flat(ya), flat(yb), l0_w_out[:HALF], l0_w_out[HALF:]))

    x = _ffn(x, l1_ffn1_norm, l1_ffn1_w_gate, l1_ffn1_w_up, l1_ffn1_w_down)
    qk_m, v_m, o_m, q_f, k_f, v_f, gates = _proj1(x, l1_mix_norm, l1_w_in)
    hm = _mlstm(seq(qk_m), seq(v_m), seq(o_m), seq(gates), l1_mlstm_conv_w, l1_mlstm_conv_b,
                l1_mlstm_b_i, l1_mlstm_b_f, l1_mlstm_out_norm)
    fcol, frow = _fox_cumsum(seq(gates), l1_fox_b_f)
    hf = _fox_attention(seq(q_f), seq(k_f), seq(v_f), fcol, frow)
    x = _ffn(x, l1_ffn2_norm, l1_ffn2_w_gate, l1_ffn2_w_up, l1_ffn2_w_down,
             mix=(flat(hm), flat(hf), l1_w_out[:HALF], l1_w_out[HALF:]), final_norm=final_norm)
    return x.reshape(b, s, d)
```

```python
import functools
import math

import jax
import jax.numpy as jnp
from jax import lax
from jax.experimental import pallas as pl
from jax.experimental.pallas import tpu as pltpu

F32 = jnp.float32
BF16 = jnp.bfloat16

D_MODEL = 1024
D_FF = 2816
NORM_EPS = 1e-6
ROPE_THETA = 500000.0
DIFF_HEADS = 4
DIFF_HEAD_DIM = 64
ROT_DIM = DIFF_HEAD_DIM // 4
DIFF_LAMBDA_INIT = 0.8 - 0.6 * math.exp(-0.3 * 0)
S5_GROUPS = 32
S5_GROUP = 16
S5_STATE = 64
S5_NSTATE = S5_GROUPS * S5_STATE
MLSTM_HEADS = 4
MLSTM_QK_DIM = 64
MLSTM_V_DIM = 128
MLSTM_CONV = 4
FOX_HEADS = 8
FOX_HEAD_DIM = 64
HALF = 512

LANES = 128
SUBLANES = 8
VMEM_LIMIT_BYTES = 60 * 1024 * 1024

TOKEN_TILE = 512
FF_CHUNK = 256
ATTN_TILE = 512
S5_TILE = 256
MLSTM_CHUNK = 256
GATE_COLS = LANES

NT_DIMS = (((1,), (1,)), ((), ()))


def _params(*semantics):
    return pltpu.CompilerParams(dimension_semantics=semantics, vmem_limit_bytes=VMEM_LIMIT_BYTES)


def _resident(shape):
    return pl.BlockSpec(shape, lambda *_: (0,) * len(shape), pipeline_mode=pl.Buffered(1))


def _rms(x, g):
    return x * lax.rsqrt(jnp.mean(x * x, axis=-1, keepdims=True) + NORM_EPS) * g


def _log_sigmoid(x):
    return jnp.minimum(x, 0.0) - jnp.log1p(jnp.exp(-jnp.abs(x)))


def _cumsum_rows(x):
    n = x.shape[0]
    row = lax.broadcasted_iota(jnp.int32, x.shape, 0)
    k = 1
    while k < n:
        x = x + jnp.where(row >= k, pltpu.roll(x, k, 0), 0.0)
        k *= 2
    return x


def _keep_half_rows(x, half):
    rows = x.shape[0] // 2
    zeros = jnp.zeros((rows,) + x.shape[1:], x.dtype)
    return jnp.concatenate([x[:rows], zeros] if half == 0 else [zeros, x[rows:]], axis=0)


def _causal_pairs(n_blocks):
    ii, jj = [], []
    for i in range(n_blocks):
        for j in range(i + 1):
            ii.append(i)
            jj.append(j)
    return jnp.asarray(ii, jnp.int32), jnp.asarray(jj, jnp.int32)


def _ffn_body(*refs, has_mix, has_final):
    refs = list(refs)
    x_ref = refs.pop(0)
    if has_mix:
        a_ref, b_ref, wa_ref, wb_ref = refs[:4]
        refs = refs[4:]
    g_ref, wg_ref, wu_ref, wd_ref = refs[:4]
    refs = refs[4:]
    if has_final:
        gf_ref = refs.pop(0)
    o_ref = refs.pop(0)

    x = x_ref[...]
    if has_mix:
        x = x + jnp.dot(a_ref[...], wa_ref[...], preferred_element_type=F32)
        x = x + jnp.dot(b_ref[...], wb_ref[...], preferred_element_type=F32)
    h = _rms(x, g_ref[...]).astype(BF16)
    acc = jnp.zeros_like(x)
    for c in range(D_FF // FF_CHUNK):
        sl = slice(c * FF_CHUNK, (c + 1) * FF_CHUNK)
        gate = jnp.dot(h, wg_ref[:, sl], preferred_element_type=F32)
        up = jnp.dot(h, wu_ref[:, sl], preferred_element_type=F32)
        act = (gate * jax.nn.sigmoid(gate) * up).astype(BF16)
        acc = acc + jnp.dot(act, wd_ref[sl, :], preferred_element_type=F32)
    y = x + 0.5 * acc
    if has_final:
        y = _rms(y, gf_ref[...])
    o_ref[...] = y


def _ffn(x, norm, wg, wu, wd, mix=None, final_norm=None):
    n, d = x.shape
    tm = min(TOKEN_TILE, n)
    row = lambda i: (i, 0)
    args = [x]
    specs = [pl.BlockSpec((tm, d), row)]
    if mix is not None:
        a, b, wa, wb = mix
        args += [a, b, wa.astype(BF16), wb.astype(BF16)]
        specs += [pl.BlockSpec((tm, HALF), row), pl.BlockSpec((tm, HALF), row),
                  _resident((HALF, d)), _resident((HALF, d))]
    args += [norm.reshape(1, d), wg.astype(BF16), wu.astype(BF16), wd.astype(BF16)]
    specs += [_resident((1, d)), _resident((d, D_FF)), _resident((d, D_FF)), _resident((D_FF, d))]
    if final_norm is not None:
        args.append(final_norm.reshape(1, d))
        specs.append(_resident((1, d)))
    body = functools.partial(_ffn_body, has_mix=mix is not None, has_final=final_norm is not None)
    return pl.pallas_call(
        body,
        grid=(n // tm,),
        in_specs=specs,
        out_specs=pl.BlockSpec((tm, d), row),
        out_shape=jax.ShapeDtypeStruct((n, d), F32),
        compiler_params=_params("parallel"),
        name="ffn",
    )(*args)


def _rope_angles(seq):
    pos = jnp.arange(seq, dtype=F32)
    inv = ROPE_THETA ** (-jnp.arange(0, ROT_DIM, 2, dtype=F32) / ROT_DIM)
    ang = pos[:, None] * inv[None, :]
    return jnp.cos(ang), jnp.sin(ang)


def _rope_lane_tables(seq):
    half = ROT_DIM // 2
    cos, sin = _rope_angles(seq)
    one = jnp.ones((seq, DIFF_HEAD_DIM - ROT_DIM), F32)
    zero8 = jnp.zeros((seq, half), F32)
    zero48 = jnp.zeros((seq, DIFF_HEAD_DIM - ROT_DIM), F32)
    cos_t = jnp.concatenate([cos, cos, one] * 2, axis=1)
    sa_t = jnp.concatenate([-sin, zero8, zero48] * 2, axis=1)
    sb_t = jnp.concatenate([zero8, sin, zero48] * 2, axis=1)
    return cos_t, sa_t, sb_t


def _proj0_body(x_ref, g_ref, wt_ref, w_ref, cos_ref, sa_ref, sb_ref, cosr_ref, sinr_ref,
                qt_ref, k_ref, vt_ref, u_ref):
    h = _rms(x_ref[...], g_ref[...]).astype(BF16)
    pt = lax.dot_general(wt_ref[...], h, NT_DIMS, preferred_element_type=F32)
    p = jnp.dot(h, w_ref[...], preferred_element_type=F32)
    half = ROT_DIM // 2
    scale = DIFF_HEAD_DIM ** -0.5

    cos_r, sin_r = cosr_ref[...], sinr_ref[...]
    for comp in range(2 * DIFF_HEADS):
        r0 = comp * DIFF_HEAD_DIM
        x1, x2 = pt[r0:r0 + half], pt[r0 + half:r0 + ROT_DIM]
        qt_ref[0, r0:r0 + half, :] = ((x1 * cos_r - x2 * sin_r) * scale).astype(BF16)
        qt_ref[0, r0 + half:r0 + ROT_DIM, :] = ((x2 * cos_r + x1 * sin_r) * scale).astype(BF16)
        qt_ref[0, r0 + ROT_DIM:r0 + DIFF_HEAD_DIM, :] = (pt[r0 + ROT_DIM:r0 + DIFF_HEAD_DIM] * scale).astype(BF16)
    vt_ref[0] = pt[HALF:2 * HALF].astype(BF16)

    cos_t, sa_t, sb_t = cos_ref[...], sa_ref[...], sb_ref[...]
    for hd in range(DIFF_HEADS):
        sl = slice(hd * LANES, (hd + 1) * LANES)
        t = p[:, sl]
        k_ref[:, sl] = (t * cos_t + pltpu.roll(t, LANES - half, 1) * sa_t
                        + pltpu.roll(t, half, 1) * sb_t).astype(BF16)
    u_ref[...] = p[:, HALF:2 * HALF]


def _proj0(x, norm, w_in, seq):
    n, d = x.shape
    b = n // seq
    tm = min(TOKEN_TILE, seq)
    tps = seq // tm
    row = lambda i: (i, 0)
    pos = lambda i: (i % tps, 0)
    pos_t = lambda i: (0, i % tps)
    seq_t = lambda i: (i // tps, 0, i % tps)
    cos, sin = _rope_angles(seq)
    w_t = jnp.concatenate([w_in[:, :HALF], w_in[:, 2 * HALF:3 * HALF]], axis=1).T.astype(BF16)
    w_n = jnp.concatenate([w_in[:, HALF:2 * HALF], w_in[:, 3 * HALF:]], axis=1).astype(BF16)
    t_shape = jax.ShapeDtypeStruct((b, HALF, seq), BF16)
    return pl.pallas_call(
        _proj0_body,
        grid=(n // tm,),
        in_specs=[pl.BlockSpec((tm, d), row), _resident((1, d)), _resident((2 * HALF, d)),
                  _resident((d, 2 * HALF)),
                  pl.BlockSpec((tm, LANES), pos), pl.BlockSpec((tm, LANES), pos),
                  pl.BlockSpec((tm, LANES), pos),
                  pl.BlockSpec((ROT_DIM // 2, tm), pos_t), pl.BlockSpec((ROT_DIM // 2, tm), pos_t)],
        out_specs=[pl.BlockSpec((1, HALF, tm), seq_t), pl.BlockSpec((tm, HALF), row),
                   pl.BlockSpec((1, HALF, tm), seq_t), pl.BlockSpec((tm, HALF), row)],
        out_shape=[t_shape, jax.ShapeDtypeStruct((n, HALF), BF16), t_shape,
                   jax.ShapeDtypeStruct((n, HALF), F32)],
        compiler_params=_params("parallel"),
        name="proj0",
    )(x, norm.reshape(1, d), w_t, w_n, *_rope_lane_tables(seq), cos.T, sin.T)


def _diff_attn_body(ii_ref, jj_ref, lam_ref, subln_ref, qt_ref, k_ref, vt_ref, o_ref,
                    m_ref, l_ref, acc_ref, *, tile):
    p_idx = pl.program_id(2)
    i = ii_ref[p_idx]
    j = jj_ref[p_idx]

    @pl.when(j == 0)
    def _():
        m_ref[...] = jnp.full_like(m_ref, -jnp.inf)
        l_ref[...] = jnp.zeros_like(l_ref)
        acc_ref[...] = jnp.zeros_like(acc_ref)

    def sweep(diagonal):
        q_t = qt_ref[0]
        k = k_ref[0]
        v_t = vt_ref[0]
        if diagonal:
            keep = (lax.broadcasted_iota(jnp.int32, (tile, tile), 0)
                    <= lax.broadcasted_iota(jnp.int32, (tile, tile), 1))
        for c in range(2):
            s = jnp.dot(k, _keep_half_rows(q_t, c), preferred_element_type=F32)
            if diagonal:
                s = jnp.where(keep, s, -jnp.inf)
            m_old = m_ref[c]
            m_new = jnp.maximum(m_old, jnp.max(s, axis=0, keepdims=True))
            alpha = jnp.exp(m_old - m_new)
            p = jnp.exp(s - m_new)
            l_ref[c] = alpha * l_ref[c] + jnp.sum(p, axis=0, keepdims=True)
            acc_ref[c] = alpha * acc_ref[c] + jnp.dot(v_t, p.astype(BF16), preferred_element_type=F32)
            m_ref[c] = m_new

    @pl.when(j < i)
    def _():
        sweep(False)

    @pl.when(j == i)
    def _():
        sweep(True)
        lam_v = lam_ref[...]
        lam = (jnp.exp(jnp.sum(lam_v[0:1] * lam_v[1:2], keepdims=True))
               - jnp.exp(jnp.sum(lam_v[2:3] * lam_v[3:4], keepdims=True)) + DIFF_LAMBDA_INIT)
        o_t = acc_ref[0] / l_ref[0] - lam * (acc_ref[1] / l_ref[1])
        o_t = o_t * lax.rsqrt(jnp.mean(o_t * o_t, axis=0, keepdims=True) + NORM_EPS)
        o_t = o_t * subln_ref[...] * (1.0 - DIFF_LAMBDA_INIT)
        o_ref[0] = o_t.T.astype(BF16)


def _diff_attention(q_t, k, v_t, lam_vecs, subln):
    b, s, _ = k.shape
    t = min(ATTN_TILE, s)
    ii, jj = _causal_pairs(s // t)
    q_map = lambda bi, h, p, ii, jj: (bi, ii[p], h)
    k_map = lambda bi, h, p, ii, jj: (bi, jj[p], h)
    qt_map = lambda bi, h, p, ii, jj: (bi, h, ii[p])
    kt_map = lambda bi, h, p, ii, jj: (bi, h, jj[p])
    const = lambda bi, h, p, ii, jj: (0, 0)
    grid_spec = pltpu.PrefetchScalarGridSpec(
        num_scalar_prefetch=2,
        grid=(b, DIFF_HEADS, int(ii.shape[0])),
        in_specs=[pl.BlockSpec((4, DIFF_HEAD_DIM), const), pl.BlockSpec((LANES, 1), const),
                  pl.BlockSpec((1, LANES, t), qt_map), pl.BlockSpec((1, t, LANES), k_map),
                  pl.BlockSpec((1, LANES, t), kt_map)],
        out_specs=pl.BlockSpec((1, t, LANES), q_map),
        scratch_shapes=[pltpu.VMEM((2, 1, t), F32), pltpu.VMEM((2, 1, t), F32),
                        pltpu.VMEM((2, LANES, t), F32)],
    )
    return pl.pallas_call(
        functools.partial(_diff_attn_body, tile=t),
        grid_spec=grid_spec,
        out_shape=jax.ShapeDtypeStruct((b, s, HALF), BF16),
        compiler_params=_params("parallel", "parallel", "arbitrary"),
        name="diff_attn",
    )(ii, jj, lam_vecs, subln.reshape(LANES, 1), q_t, k, v_t)


def _s5_discretize_body(lr_ref, li_ref, ldt_ref, br_ref, bi_ref, ar_ref, ai_ref, bbr_ref, bbi_ref):
    lr, li = lr_ref[...], li_ref[...]
    dt = jnp.exp(ldt_ref[...])
    mag = jnp.exp(lr * dt)
    ar, ai = mag * jnp.cos(li * dt), mag * jnp.sin(li * dt)
    den = lr * lr + li * li
    gr = ((ar - 1.0) * lr + ai * li) / den
    gi = (ai * lr - (ar - 1.0) * li) / den
    br, bi = br_ref[...], bi_ref[...]
    ar_ref[...] = ar
    ai_ref[...] = ai
    bbr_ref[...] = gr * br - gi * bi
    bbi_ref[...] = gr * bi + gi * br


def _s5_discretize(a_re, a_im, log_dt, b_re, b_im):
    col = lambda a: a.reshape(S5_NSTATE, 1)
    ldt = jnp.broadcast_to(log_dt[:, None], (S5_GROUPS, S5_STATE))
    cshape = jax.ShapeDtypeStruct((S5_NSTATE, 1), F32)
    bshape = jax.ShapeDtypeStruct((S5_NSTATE, S5_GROUP), F32)
    return pl.pallas_call(
        _s5_discretize_body,
        out_shape=[cshape, cshape, bshape, bshape],
        name="s5_discretize",
    )(col(a_re), col(a_im), col(ldt), b_re.reshape(S5_NSTATE, S5_GROUP), b_im.reshape(S5_NSTATE, S5_GROUP))


def _s5_body(u_ref, ar_ref, ai_ref, bbr_ref, bbi_ref, cr_ref, ci_ref, d_ref, wglu_ref, bglu_ref,
             o_ref, xr_ref, xi_ref, h_ref, *, tile):
    @pl.when(pl.program_id(1) == 0)
    def _():
        h_ref[...] = jnp.zeros_like(h_ref)

    u = u_ref[0]
    ub = u.astype(BF16)
    xr_ref[...] = jnp.dot(ub, bbr_ref[...], preferred_element_type=F32)
    xi_ref[...] = jnp.dot(ub, bbi_ref[...], preferred_element_type=F32)
    ar, ai = ar_ref[...], ai_ref[...]

    def step(t, carry):
        hr, hi = carry
        nhr = ar * hr - ai * hi + xr_ref[pl.ds(t, 1), :]
        nhi = ar * hi + ai * hr + xi_ref[pl.ds(t, 1), :]
        xr_ref[pl.ds(t, 1), :] = nhr
        xi_ref[pl.ds(t, 1), :] = nhi
        return nhr, nhi

    hr, hi = lax.fori_loop(0, tile, step, (h_ref[0:1, :], h_ref[1:2, :]), unroll=8)
    h_ref[0:1, :] = hr
    h_ref[1:2, :] = hi

    y = (jnp.dot(xr_ref[...].astype(BF16), cr_ref[...], preferred_element_type=F32)
         - jnp.dot(xi_ref[...].astype(BF16), ci_ref[...], preferred_element_type=F32))
    y = y + d_ref[...] * u
    z = jax.nn.gelu(y, approximate=True)
    gate = jnp.dot(z.astype(BF16), wglu_ref[...], preferred_element_type=F32) + bglu_ref[...]
    o_ref[0] = (z * jax.nn.sigmoid(gate)).astype(BF16)


def _s5(u, a_re, a_im, log_dt, b_re, b_im, c_re, c_im, d_skip, w_glu, b_glu):
    b, s, w = u.shape
    t = min(S5_TILE, s)
    ar, ai, bbr, bbi = _s5_discretize(a_re, a_im, log_dt, b_re, b_im)
    eye = jnp.eye(S5_GROUPS, dtype=F32)

    def block_in(m):
        m = m.reshape(S5_GROUPS, S5_STATE, S5_GROUP)
        return jnp.einsum('gpc,gh->gchp', m, eye).reshape(w, S5_NSTATE).astype(BF16)

    def block_out(m):
        return jnp.einsum('gcp,gh->gphc', m, eye).reshape(S5_NSTATE, w).astype(BF16)

    seq_blk = lambda bi, ti: (bi, ti, 0)
    return pl.pallas_call(
        functools.partial(_s5_body, tile=t),
        grid=(b, s // t),
        in_specs=[pl.BlockSpec((1, t, w), seq_blk),
                  _resident((1, S5_NSTATE)), _resident((1, S5_NSTATE)),
                  _resident((w, S5_NSTATE)), _resident((w, S5_NSTATE)),
                  _resident((S5_NSTATE, w)), _resident((S5_NSTATE, w)),
                  _resident((1, w)), _resident((w, w)), _resident((1, w))],
        out_specs=pl.BlockSpec((1, t, w), seq_blk),
        out_shape=jax.ShapeDtypeStruct((b, s, w), BF16),
        scratch_shapes=[pltpu.VMEM((t, S5_NSTATE), F32), pltpu.VMEM((t, S5_NSTATE), F32),
                        pltpu.VMEM((2, S5_NSTATE), F32)],
        compiler_params=_params("parallel", "arbitrary"),
        name="s5",
    )(u, ar.reshape(1, S5_NSTATE), ai.reshape(1, S5_NSTATE), block_in(bbr), block_in(bbi),
      block_out(c_re), block_out(c_im), d_skip.reshape(1, w), w_glu.astype(BF16), b_glu.reshape(1, w))


def _proj1_body(x_ref, g_ref, wt_ref, w_ref, qk_ref, vm_ref, om_ref, qft_ref, kf_ref, vft_ref, gt_ref):
    h = _rms(x_ref[...], g_ref[...]).astype(BF16)
    pt = lax.dot_general(wt_ref[...], h, NT_DIMS, preferred_element_type=F32)
    p = jnp.dot(h, w_ref[...], preferred_element_type=F32)
    qft_ref[0] = (pt[:HALF] * FOX_HEAD_DIM ** -0.5).astype(BF16)
    vft_ref[0] = pt[HALF:].astype(BF16)
    qk_ref[...] = p[:, 0:HALF]
    vm_ref[...] = p[:, HALF:2 * HALF].astype(BF16)
    om_ref[...] = p[:, 2 * HALF:3 * HALF]
    kf_ref[...] = p[:, 3 * HALF:4 * HALF].astype(BF16)
    gt_ref[...] = p[:, 4 * HALF:4 * HALF + GATE_COLS]


def _proj1(x, norm, w_in, seq):
    n, d = x.shape
    b = n // seq
    tm = min(TOKEN_TILE, seq)
    tps = seq // tm
    row = lambda i: (i, 0)
    seq_t = lambda i: (i // tps, 0, i % tps)
    o = 2 * HALF
    g = 2 * MLSTM_HEADS
    om = o + g
    w_n = jnp.concatenate(
        [w_in[:, :o], w_in[:, om:om + HALF], w_in[:, om + 2 * HALF:om + 3 * HALF],
         w_in[:, o:om], w_in[:, om + 4 * HALF:],
         jnp.zeros((d, GATE_COLS - g - FOX_HEADS), w_in.dtype)], axis=1).astype(BF16)
    w_t = jnp.concatenate([w_in[:, om + HALF:om + 2 * HALF], w_in[:, om + 3 * HALF:om + 4 * HALF]],
                          axis=1).T.astype(BF16)
    f32_half = jax.ShapeDtypeStruct((n, HALF), F32)
    bf_half = jax.ShapeDtypeStruct((n, HALF), BF16)
    t_shape = jax.ShapeDtypeStruct((b, HALF, seq), BF16)
    half_spec = pl.BlockSpec((tm, HALF), row)
    t_spec = pl.BlockSpec((1, HALF, tm), seq_t)
    return pl.pallas_call(
        _proj1_body,
        grid=(n // tm,),
        in_specs=[pl.BlockSpec((tm, d), row), _resident((1, d)), _resident((2 * HALF, d)),
                  _resident((d, 4 * HALF + GATE_COLS))],
        out_specs=[half_spec, half_spec, half_spec, t_spec, half_spec, t_spec,
                   pl.BlockSpec((tm, GATE_COLS), row)],
        out_shape=[f32_half, bf_half, f32_half, t_shape, bf_half, t_shape,
                   jax.ShapeDtypeStruct((n, GATE_COLS), F32)],
        compiler_params=_params("parallel"),
        name="proj1",
    )(x, norm.reshape(1, d), w_t, w_n)


def _mlstm_body(qk_ref, v_ref, og_ref, gt_ref, cw_ref, cb_ref, gb_ref, on_ref, out_ref,
                xbuf_ref, cn_ref, m_ref, *, chunk):
    pad = SUBLANES
    dk, dv = MLSTM_QK_DIM, MLSTM_V_DIM

    @pl.when(pl.program_id(1) == 0)
    def _():
        xbuf_ref[0:pad, :] = jnp.zeros((pad, HALF), F32)
        cn_ref[...] = jnp.zeros_like(cn_ref)
        m_ref[...] = jnp.zeros_like(m_ref)

    x = qk_ref[0]
    xbuf_ref[pad:pad + chunk, :] = x
    a = jnp.zeros_like(x) + cb_ref[...]
    for tap in range(MLSTM_CONV):
        a = a + cw_ref[tap:tap + 1, :] * xbuf_ref[pl.ds(pad - (MLSTM_CONV - 1) + tap, chunk), :]
    xbuf_ref[0:pad, :] = x[chunk - pad:chunk, :]
    a = a * jax.nn.sigmoid(a)
    q = a[:, :MLSTM_HEADS * dk] * dk ** -0.5
    k = a[:, MLSTM_HEADS * dk:]
    k_t = k.T

    gi = gt_ref[0] + gb_ref[...]
    bcum = _cumsum_rows(_log_sigmoid(gi))
    gi_t = gi.T
    bcum_t = bcum.T
    tri = (lax.broadcasted_iota(jnp.int32, (chunk, chunk), 1)
           <= lax.broadcasted_iota(jnp.int32, (chunk, chunk), 0))
    lane = lax.broadcasted_iota(jnp.int32, (1, LANES), 1)
    v_all = v_ref[0]
    o_gate = og_ref[0]
    ones = jnp.ones((chunk, dv), BF16)

    for hd in range(MLSTM_HEADS):
        pair, odd = hd // 2, hd % 2
        f_lane = MLSTM_HEADS + hd
        b_col = bcum[:, f_lane:f_lane + 1]
        b_row = bcum_t[f_lane:f_lane + 1, :]
        i_row = gi_t[hd:hd + 1, :]
        m_prev = m_ref[hd]
        dm = jnp.where(tri, b_col - b_row + i_row, -jnp.inf)
        g = b_col + m_prev
        mt = jnp.maximum(g, jnp.max(dm, axis=-1, keepdims=True))
        w_intra = jnp.exp(dm - mt)
        w_inter = jnp.exp(g - mt)

        sel = (lane >= dk) if odd else (lane < dk)
        q2 = q[:, pair * LANES:(pair + 1) * LANES]
        q2 = jnp.where(sel, q2, 0.0).astype(BF16)
        k2 = k[:, pair * LANES:(pair + 1) * LANES].astype(BF16)
        s = lax.dot_general(q2, k2, NT_DIMS, preferred_element_type=F32) * w_intra
        v_aug = jnp.concatenate([v_all[:, hd * dv:(hd + 1) * dv], ones], axis=1)
        state = cn_ref[pair]
        q_state = jnp.dot(q2, state.astype(BF16), preferred_element_type=F32)
        s_v = jnp.dot(s.astype(BF16), v_aug, preferred_element_type=F32)
        num = w_inter * q_state[:, :dv] + s_v[:, :dv]
        den = w_inter * q_state[:, dv:] + s_v[:, dv:]
        h_out = num / jnp.maximum(jnp.abs(den), jnp.exp(-mt))

        b_last = bcum[chunk - 1:chunk, f_lane:f_lane + 1]
        dec = b_last - b_row + i_row
        m_new = jnp.maximum(b_last + m_prev, jnp.max(dec, axis=-1, keepdims=True))
        w_s = jnp.exp(dec - m_new)
        carry_w = jnp.exp(b_last + m_prev - m_new)
        k_w = (k_t[hd * dk:(hd + 1) * dk, :] * w_s).astype(BF16)
        rows = slice(odd * dk, (odd + 1) * dk)
        cn_ref[pair, rows, :] = carry_w * state[rows, :] + jnp.dot(k_w, v_aug, preferred_element_type=F32)
        m_ref[hd] = m_new

        cols = slice(hd * dv, (hd + 1) * dv)
        h_norm = _rms(h_out, on_ref[hd:hd + 1, :])
        out_ref[0, :, cols] = (jax.nn.sigmoid(o_gate[:, cols]) * h_norm).astype(BF16)


def _mlstm(qk, v, o_gate, gates, conv_w, conv_b, b_i, b_f, out_norm):
    b, s, _ = qk.shape
    c = min(MLSTM_CHUNK, s)
    gate_bias = jnp.concatenate([b_i, b_f, jnp.zeros((GATE_COLS - 2 * MLSTM_HEADS,), F32)]).reshape(1, GATE_COLS)
    seq_blk = lambda bi, ci: (bi, ci, 0)
    half_spec = pl.BlockSpec((1, c, HALF), seq_blk)
    return pl.pallas_call(
        functools.partial(_mlstm_body, chunk=c),
        grid=(b, s // c),
        in_specs=[half_spec, half_spec, half_spec, pl.BlockSpec((1, c, GATE_COLS), seq_blk),
                  _resident((MLSTM_CONV, HALF)), _resident((1, HALF)), _resident((1, GATE_COLS)),
                  _resident((MLSTM_HEADS, MLSTM_V_DIM))],
        out_specs=half_spec,
        out_shape=jax.ShapeDtypeStruct((b, s, HALF), BF16),
        scratch_shapes=[pltpu.VMEM((c + SUBLANES, HALF), F32),
                        pltpu.VMEM((MLSTM_HEADS // 2, 2 * MLSTM_QK_DIM, 2 * MLSTM_V_DIM), F32),
                        pltpu.VMEM((MLSTM_HEADS, 1, 1), F32)],
        compiler_params=_params("parallel", "arbitrary"),
        name="mlstm",
    )(qk, v, o_gate, gates, conv_w, conv_b.reshape(1, HALF), gate_bias, out_norm)


FOX_GATE_LANE = 2 * MLSTM_HEADS
F_TERMS = 3


def _fox_cumsum_body(gt_ref, gb_ref, sel_ref, frow_ref, fk_ref):
    f = _cumsum_rows(_log_sigmoid(gt_ref[0] + gb_ref[...]))
    hi = f.astype(BF16)
    rest = f - hi.astype(F32)
    mid = rest.astype(BF16)
    lo = (rest - mid.astype(F32)).astype(BF16)
    terms = jnp.concatenate([hi, mid, lo], axis=1)
    fk_ref[0] = jnp.dot(terms, sel_ref[...], preferred_element_type=F32).astype(BF16)
    frow_ref[0] = f.T[FOX_GATE_LANE:FOX_GATE_LANE + FOX_HEADS, :]


def _fox_cumsum(gates, fox_b_f):
    b, s, _ = gates.shape
    bias = jnp.concatenate([jnp.zeros((FOX_GATE_LANE,), F32), fox_b_f,
                            jnp.zeros((GATE_COLS - FOX_GATE_LANE - FOX_HEADS,), F32)]).reshape(1, GATE_COLS)
    src, dst = [], []
    for head in range(FOX_HEADS):
        for term in range(F_TERMS):
            src.append(term * GATE_COLS + FOX_GATE_LANE + head)
            dst.append((head // 2) * LANES + F_TERMS * (head % 2) + term)
    sel = jnp.zeros((F_TERMS * GATE_COLS, HALF), BF16).at[jnp.asarray(src), jnp.asarray(dst)].set(-1.0)
    return pl.pallas_call(
        _fox_cumsum_body,
        grid=(b,),
        in_specs=[pl.BlockSpec((1, s, GATE_COLS), lambda bi: (bi, 0, 0)), _resident((1, GATE_COLS)),
                  _resident((F_TERMS * GATE_COLS, HALF))],
        out_specs=[pl.BlockSpec((1, FOX_HEADS, s), lambda bi: (bi, 0, 0)),
                   pl.BlockSpec((1, s, HALF), lambda bi: (bi, 0, 0))],
        out_shape=[jax.ShapeDtypeStruct((b, FOX_HEADS, s), F32),
                   jax.ShapeDtypeStruct((b, s, HALF), BF16)],
        compiler_params=_params("parallel"),
        name="fox_cumsum",
    )(gates, bias, sel)


def _fox_attn_body(ii_ref, jj_ref, qt_ref, k_ref, fk_ref, vt_ref, frow_ref, o_ref,
                   m_ref, l_ref, fq_ref, acc_ref, *, tile):
    pair = pl.program_id(1)
    p_idx = pl.program_id(2)
    i = ii_ref[p_idx]
    j = jj_ref[p_idx]
    dim = lax.broadcasted_iota(jnp.int32, (LANES, 1), 0)
    low = dim < FOX_HEAD_DIM

    @pl.when(j == 0)
    def _():
        m_ref[...] = jnp.full_like(m_ref, -jnp.inf)
        l_ref[...] = jnp.zeros_like(l_ref)
        acc_ref[...] = jnp.zeros_like(acc_ref)
        frow = frow_ref[0]
        head_row = lax.broadcasted_iota(jnp.int32, (FOX_HEADS, 1), 0)
        for c in range(2):
            fq_ref[c] = jnp.sum(jnp.where(head_row == 2 * pair + c, frow, 0.0), axis=0, keepdims=True)

    def sweep(diagonal):
        q_t = qt_ref[0]
        k_aug = jnp.concatenate([k_ref[0], fk_ref[0]], axis=1)
        v_t = vt_ref[0]
        if diagonal:
            keep = (lax.broadcasted_iota(jnp.int32, (tile, tile), 0)
                    <= lax.broadcasted_iota(jnp.int32, (tile, tile), 1))
        alphas, pvs = [], []
        for c in range(2):
            ones_rows = jnp.logical_and(dim >= F_TERMS * c, dim < F_TERMS * (c + 1))
            q_aug = jnp.concatenate(
                [_keep_half_rows(q_t, c),
                 jnp.broadcast_to(jnp.where(ones_rows, 1.0, 0.0), q_t.shape).astype(BF16)], axis=0)
            s = jnp.dot(k_aug, q_aug, preferred_element_type=F32) + fq_ref[c]
            if diagonal:
                s = jnp.where(keep, s, -jnp.inf)
            m_old = m_ref[c]
            m_new = jnp.maximum(m_old, jnp.max(s, axis=0, keepdims=True))
            alpha = jnp.exp(m_old - m_new)
            p = jnp.exp(s - m_new)
            l_ref[c] = alpha * l_ref[c] + jnp.sum(p, axis=0, keepdims=True)
            m_ref[c] = m_new
            alphas.append(alpha)
            pvs.append(jnp.dot(v_t, p.astype(BF16), preferred_element_type=F32))
        acc_ref[...] = (jnp.where(low, alphas[0], alphas[1]) * acc_ref[...]
                        + jnp.where(low, pvs[0], pvs[1]))

    @pl.when(j < i)
    def _():
        sweep(False)

    @pl.when(j == i)
    def _():
        sweep(True)
        o_t = acc_ref[...] / jnp.where(low, l_ref[0], l_ref[1])
        o_ref[0] = o_t.T.astype(BF16)


def _fox_attention(q_t, k, fk, v_t, frow):
    b, s, _ = k.shape
    t = min(ATTN_TILE, s)
    ii, jj = _causal_pairs(s // t)
    q_map = lambda bi, h, p, ii, jj: (bi, ii[p], h)
    k_map = lambda bi, h, p, ii, jj: (bi, jj[p], h)
    qt_map = lambda bi, h, p, ii, jj: (bi, h, ii[p])
    kt_map = lambda bi, h, p, ii, jj: (bi, h, jj[p])
    grid_spec = pltpu.PrefetchScalarGridSpec(
        num_scalar_prefetch=2,
        grid=(b, FOX_HEADS // 2, int(ii.shape[0])),
        in_specs=[pl.BlockSpec((1, LANES, t), qt_map), pl.BlockSpec((1, t, LANES), k_map),
                  pl.BlockSpec((1, t, LANES), k_map), pl.BlockSpec((1, LANES, t), kt_map),
                  pl.BlockSpec((1, FOX_HEADS, t), lambda bi, h, p, ii, jj: (bi, 0, ii[p]))],
        out_specs=pl.BlockSpec((1, t, LANES), q_map),
        scratch_shapes=[pltpu.VMEM((2, 1, t), F32), pltpu.VMEM((2, 1, t), F32),
                        pltpu.VMEM((2, 1, t), F32), pltpu.VMEM((LANES, t), F32)],
    )
    return pl.pallas_call(
        functools.partial(_fox_attn_body, tile=t),
        grid_spec=grid_spec,
        out_shape=jax.ShapeDtypeStruct((b, s, HALF), BF16),
        compiler_params=_params("parallel", "parallel", "arbitrary"),
        name="fox_attn",
    )(ii, jj, q_t, k, fk, v_t, frow)


def kernel(x, l0_ffn1_norm, l0_ffn1_w_gate, l0_ffn1_w_up, l0_ffn1_w_down, l0_mix_norm, l0_w_in, l0_diff_lambda_q1, l0_diff_lambda_k1, l0_diff_lambda_q2, l0_diff_lambda_k2, l0_diff_subln, l0_s5_a_re, l0_s5_a_im, l0_s5_log_dt, l0_s5_b_re, l0_s5_b_im, l0_s5_c_re, l0_s5_c_im, l0_s5_d, l0_s5_w_glu, l0_s5_b_glu, l0_w_out, l0_ffn2_norm, l0_ffn2_w_gate, l0_ffn2_w_up, l0_ffn2_w_down, l1_ffn1_norm, l1_ffn1_w_gate, l1_ffn1_w_up, l1_ffn1_w_down, l1_mix_norm, l1_w_in, l1_mlstm_conv_w, l1_mlstm_conv_b, l1_mlstm_b_i, l1_mlstm_b_f, l1_mlstm_out_norm, l1_fox_b_f, l1_w_out, l1_ffn2_norm, l1_ffn2_w_gate, l1_ffn2_w_up, l1_ffn2_w_down, final_norm):
    b, s, d = x.shape
    n = b * s
    seq = lambda a: a.reshape(b, s, a.shape[-1])
    flat = lambda a: a.reshape(n, a.shape[-1])

    x = _ffn(x.reshape(n, d), l0_ffn1_norm, l0_ffn1_w_gate, l0_ffn1_w_up, l0_ffn1_w_down)
    q_t, k, v_t, u = _proj0(x, l0_mix_norm, l0_w_in, s)
    lam_vecs = jnp.stack([l0_diff_lambda_q1, l0_diff_lambda_k1, l0_diff_lambda_q2, l0_diff_lambda_k2])
    ya = _diff_attention(q_t, seq(k), v_t, lam_vecs, l0_diff_subln)
    yb = _s5(seq(u), l0_s5_a_re, l0_s5_a_im, l0_s5_log_dt, l0_s5_b_re, l0_s5_b_im,
             l0_s5_c_re, l0_s5_c_im, l0_s5_d, l0_s5_w_glu, l0_s5_b_glu)
    x = _ffn(x, l0_ffn2_norm, l0_ffn2_w_gate, l0_ffn2_w_up, l0_ffn2_w_down,
             mix=(flat(ya), flat(yb), l0_w_out[:HALF], l0_w_out[HALF:]))

    x = _ffn(x, l1_ffn1_norm, l1_ffn1_w_gate, l1_ffn1_w_up, l1_ffn1_w_down)
    qk_m, v_m, o_m, qf_t, k_f, vf_t, gates = _proj1(x, l1_mix_norm, l1_w_in, s)
    hm = _mlstm(seq(qk_m), seq(v_m), seq(o_m), seq(gates), l1_mlstm_conv_w, l1_mlstm_conv_b,
                l1_mlstm_b_i, l1_mlstm_b_f, l1_mlstm_out_norm)
    frow, fk = _fox_cumsum(seq(gates), l1_fox_b_f)
    hf = _fox_attention(qf_t, seq(k_f), fk, vf_t, frow)
    x = _ffn(x, l1_ffn2_norm, l1_ffn2_w_gate, l1_ffn2_w_up, l1_ffn2_w_down,
             mix=(flat(hm), flat(hf), l1_w_out[:HALF], l1_w_out[HALF:]), final_norm=final_norm)
    return x.reshape(b, s, d)
```

```python
import functools
import math

import jax
import jax.numpy as jnp
from jax import lax
from jax.experimental import pallas as pl
from jax.experimental.pallas import tpu as pltpu

F32 = jnp.float32
BF16 = jnp.bfloat16

D_MODEL = 1024
D_FF = 2816
NORM_EPS = 1e-6
ROPE_THETA = 500000.0
DIFF_HEADS = 4
DIFF_HEAD_DIM = 64
ROT_DIM = DIFF_HEAD_DIM // 4
DIFF_LAMBDA_INIT = 0.8 - 0.6 * math.exp(-0.3 * 0)
S5_GROUPS = 32
S5_GROUP = 16
S5_STATE = 64
S5_NSTATE = S5_GROUPS * S5_STATE
MLSTM_HEADS = 4
MLSTM_QK_DIM = 64
MLSTM_V_DIM = 128
MLSTM_CONV = 4
FOX_HEADS = 8
FOX_HEAD_DIM = 64
HALF = 512

LANES = 128
SUBLANES = 8
VMEM_LIMIT_BYTES = 60 * 1024 * 1024

TOKEN_TILE = 512
FF_CHUNK = 256
ATTN_TILE = 512
S5_TILE = 512
S5_CHUNK = 16
MLSTM_CHUNK = 256
GATE_COLS = LANES

NT_DIMS = (((1,), (1,)), ((), ()))
LOG2_E = math.log2(math.e)
ONES_ROWS = 16


def _params(*semantics, flags=None):
    return pltpu.CompilerParams(dimension_semantics=semantics, vmem_limit_bytes=VMEM_LIMIT_BYTES,
                                flags=flags)


ATTN_FLAGS = None


def _resident(shape):
    return pl.BlockSpec(shape, lambda *_: (0,) * len(shape), pipeline_mode=pl.Buffered(1))


def _rms(x, g):
    return x * lax.rsqrt(jnp.mean(x * x, axis=-1, keepdims=True) + NORM_EPS) * g


def _log_sigmoid(x):
    return jnp.minimum(x, 0.0) - jnp.log1p(jnp.exp(-jnp.abs(x)))


def _cumsum_rows(x):
    n = x.shape[0]
    row = lax.broadcasted_iota(jnp.int32, x.shape, 0)
    k = 1
    while k < n:
        x = x + jnp.where(row >= k, pltpu.roll(x, k, 0), 0.0)
        k *= 2
    return x


def _keep_half_rows(x, half):
    rows = x.shape[0] // 2
    zeros = jnp.zeros((rows,) + x.shape[1:], x.dtype)
    return jnp.concatenate([x[:rows], zeros] if half == 0 else [zeros, x[rows:]], axis=0)


def _staged(n, scores, weights, accumulate):
    s, w = {}, {}
    for step in range(n + 2):
        if step < n:
            s[step] = scores(step)
        if step >= 2:
            accumulate(step - 2, *w.pop(step - 2))
        if 1 <= step <= n:
            w[step - 1] = weights(step - 1, s.pop(step - 1))


def _causal_pairs(n_blocks):
    ii, jj = [], []
    for i in range(n_blocks):
        for j in range(i + 1):
            ii.append(i)
            jj.append(j)
    return jnp.asarray(ii, jnp.int32), jnp.asarray(jj, jnp.int32)


def _ffn_body(*refs, has_mix, has_final):
    refs = list(refs)
    x_ref = refs.pop(0)
    if has_mix:
        a_ref, b_ref, wa_ref, wb_ref = refs[:4]
        refs = refs[4:]
    g_ref, wg_ref, wu_ref, wd_ref = refs[:4]
    refs = refs[4:]
    if has_final:
        gf_ref = refs.pop(0)
    o_ref = refs.pop(0)

    x = x_ref[...]
    if has_mix:
        x = x + jnp.dot(a_ref[...], wa_ref[...], preferred_element_type=F32)
        x = x + jnp.dot(b_ref[...], wb_ref[...], preferred_element_type=F32)
    h = _rms(x, g_ref[...]).astype(BF16)
    acc = jnp.zeros_like(x)
    for c in range(D_FF // FF_CHUNK):
        sl = slice(c * FF_CHUNK, (c + 1) * FF_CHUNK)
        gate = jnp.dot(h, wg_ref[:, sl], preferred_element_type=F32)
        up = jnp.dot(h, wu_ref[:, sl], preferred_element_type=F32)
        act = (gate * jax.nn.sigmoid(gate) * up).astype(BF16)
        acc = acc + jnp.dot(act, wd_ref[sl, :], preferred_element_type=F32)
    y = x + 0.5 * acc
    if has_final:
        y = _rms(y, gf_ref[...])
    o_ref[...] = y


def _ffn(x, norm, wg, wu, wd, mix=None, final_norm=None):
    n, d = x.shape
    tm = min(TOKEN_TILE, n)
    row = lambda i: (i, 0)
    args = [x]
    specs = [pl.BlockSpec((tm, d), row)]
    if mix is not None:
        a, b, wa, wb = mix
        args += [a, b, wa.astype(BF16), wb.astype(BF16)]
        specs += [pl.BlockSpec((tm, HALF), row), pl.BlockSpec((tm, HALF), row),
                  _resident((HALF, d)), _resident((HALF, d))]
    args += [norm.reshape(1, d), wg.astype(BF16), wu.astype(BF16), wd.astype(BF16)]
    specs += [_resident((1, d)), _resident((d, D_FF)), _resident((d, D_FF)), _resident((D_FF, d))]
    if final_norm is not None:
        args.append(final_norm.reshape(1, d))
        specs.append(_resident((1, d)))
    body = functools.partial(_ffn_body, has_mix=mix is not None, has_final=final_norm is not None)
    return pl.pallas_call(
        body,
        grid=(n // tm,),
        in_specs=specs,
        out_specs=pl.BlockSpec((tm, d), row),
        out_shape=jax.ShapeDtypeStruct((n, d), F32),
        compiler_params=_params("parallel"),
        name="ffn",
    )(*args)


def _rope_angles(seq):
    pos = jnp.arange(seq, dtype=F32)
    inv = ROPE_THETA ** (-jnp.arange(0, ROT_DIM, 2, dtype=F32) / ROT_DIM)
    ang = pos[:, None] * inv[None, :]
    return jnp.cos(ang), jnp.sin(ang)


def _rope_lane_tables(seq):
    half = ROT_DIM // 2
    cos, sin = _rope_angles(seq)
    one = jnp.ones((seq, DIFF_HEAD_DIM - ROT_DIM), F32)
    zero8 = jnp.zeros((seq, half), F32)
    zero48 = jnp.zeros((seq, DIFF_HEAD_DIM - ROT_DIM), F32)
    cos_t = jnp.concatenate([cos, cos, one] * 2, axis=1)
    sa_t = jnp.concatenate([-sin, zero8, zero48] * 2, axis=1)
    sb_t = jnp.concatenate([zero8, sin, zero48] * 2, axis=1)
    return cos_t, sa_t, sb_t


def _proj0_body(x_ref, g_ref, wt_ref, w_ref, cos_ref, sa_ref, sb_ref, cosr_ref, sinr_ref,
                qt_ref, k_ref, vt_ref, u_ref):
    h = _rms(x_ref[...], g_ref[...]).astype(BF16)
    pt = lax.dot_general(wt_ref[...], h, NT_DIMS, preferred_element_type=F32)
    p = jnp.dot(h, w_ref[...], preferred_element_type=F32)
    half = ROT_DIM // 2
    scale = DIFF_HEAD_DIM ** -0.5 * LOG2_E

    cos_r, sin_r = cosr_ref[...], sinr_ref[...]
    for comp in range(2 * DIFF_HEADS):
        r0 = comp * DIFF_HEAD_DIM
        x1, x2 = pt[r0:r0 + half], pt[r0 + half:r0 + ROT_DIM]
        qt_ref[0, r0:r0 + half, :] = ((x1 * cos_r - x2 * sin_r) * scale).astype(BF16)
        qt_ref[0, r0 + half:r0 + ROT_DIM, :] = ((x2 * cos_r + x1 * sin_r) * scale).astype(BF16)
        qt_ref[0, r0 + ROT_DIM:r0 + DIFF_HEAD_DIM, :] = (pt[r0 + ROT_DIM:r0 + DIFF_HEAD_DIM] * scale).astype(BF16)
    vt_ref[0] = pt[HALF:2 * HALF].astype(BF16)

    cos_t, sa_t, sb_t = cos_ref[...], sa_ref[...], sb_ref[...]
    for hd in range(DIFF_HEADS):
        sl = slice(hd * LANES, (hd + 1) * LANES)
        t = p[:, sl]
        k_ref[:, sl] = (t * cos_t + pltpu.roll(t, LANES - half, 1) * sa_t
                        + pltpu.roll(t, half, 1) * sb_t).astype(BF16)
    u_ref[...] = p[:, HALF:2 * HALF]


def _proj0(x, norm, w_in, seq):
    n, d = x.shape
    b = n // seq
    tm = min(TOKEN_TILE, seq)
    tps = seq // tm
    row = lambda i: (i, 0)
    pos = lambda i: (i % tps, 0)
    pos_t = lambda i: (0, i % tps)
    seq_t = lambda i: (i // tps, 0, i % tps)
    cos, sin = _rope_angles(seq)
    w_t = jnp.concatenate([w_in[:, :HALF], w_in[:, 2 * HALF:3 * HALF]], axis=1).T.astype(BF16)
    w_n = jnp.concatenate([w_in[:, HALF:2 * HALF], w_in[:, 3 * HALF:]], axis=1).astype(BF16)
    t_shape = jax.ShapeDtypeStruct((b, HALF, seq), BF16)
    return pl.pallas_call(
        _proj0_body,
        grid=(n // tm,),
        in_specs=[pl.BlockSpec((tm, d), row), _resident((1, d)), _resident((2 * HALF, d)),
                  _resident((d, 2 * HALF)),
                  pl.BlockSpec((tm, LANES), pos), pl.BlockSpec((tm, LANES), pos),
                  pl.BlockSpec((tm, LANES), pos),
                  pl.BlockSpec((ROT_DIM // 2, tm), pos_t), pl.BlockSpec((ROT_DIM // 2, tm), pos_t)],
        out_specs=[pl.BlockSpec((1, HALF, tm), seq_t), pl.BlockSpec((tm, HALF), row),
                   pl.BlockSpec((1, HALF, tm), seq_t), pl.BlockSpec((tm, HALF), row)],
        out_shape=[t_shape, jax.ShapeDtypeStruct((n, HALF), BF16), t_shape,
                   jax.ShapeDtypeStruct((n, HALF), F32)],
        compiler_params=_params("parallel"),
        name="proj0",
    )(x, norm.reshape(1, d), w_t, w_n, *_rope_lane_tables(seq), cos.T, sin.T)


def _diff_attn_body(ii_ref, jj_ref, lam_ref, subln_ref, qt_ref, k_ref, vt_ref, o_ref,
                    m_ref, acc_ref, *, tile):
    p_idx = pl.program_id(1)
    i = ii_ref[p_idx]
    j = jj_ref[p_idx]

    @pl.when(j == 0)
    def _():
        m_ref[...] = jnp.full_like(m_ref, -jnp.inf)
        acc_ref[...] = jnp.zeros_like(acc_ref)

    def sweep(diagonal):
        if diagonal:
            query = lax.broadcasted_iota(jnp.int32, (tile, 2 * tile), 1)
            query = jnp.where(query >= tile, query - tile, query)
            keep = lax.broadcasted_iota(jnp.int32, (tile, 2 * tile), 0) <= query
        ones = jnp.ones((ONES_ROWS, tile), BF16)

        def scores(hd):
            rows = slice(hd * LANES, (hd + 1) * LANES)
            q_t = qt_ref[0, rows, :]
            q_both = jnp.concatenate([_keep_half_rows(q_t, 0), _keep_half_rows(q_t, 1)], axis=1)
            return jnp.dot(k_ref[0, :, rows], q_both, preferred_element_type=F32)

        def weights(hd, s):
            if diagonal:
                s = jnp.where(keep, s, -jnp.inf)
            m_old = m_ref[hd]
            m_new = jnp.maximum(m_old, jnp.max(s, axis=0, keepdims=True))
            m_ref[hd] = m_new
            return jnp.exp2(m_old - m_new), jnp.exp2(s - m_new).astype(BF16)

        def accumulate(hd, alpha, p):
            v_aug = jnp.concatenate([vt_ref[0, hd * LANES:(hd + 1) * LANES, :], ones], axis=0)
            acc_ref[hd] = alpha * acc_ref[hd] + jnp.dot(v_aug, p, preferred_element_type=F32)

        _staged(DIFF_HEADS, scores, weights, accumulate)

    @pl.when(j < i)
    def _():
        sweep(False)

    @pl.when(j == i)
    def _():
        sweep(True)
        lam_v = lam_ref[...]
        lam = (jnp.exp(jnp.sum(lam_v[0:1] * lam_v[1:2], keepdims=True))
               - jnp.exp(jnp.sum(lam_v[2:3] * lam_v[3:4], keepdims=True)) + DIFF_LAMBDA_INIT)
        for hd in range(DIFF_HEADS):
            a1, a2 = acc_ref[hd, :, :tile], acc_ref[hd, :, tile:]
            o_t = (a1[:LANES] / a1[LANES:LANES + 1] - lam * (a2[:LANES] / a2[LANES:LANES + 1]))
            o_t = o_t * lax.rsqrt(jnp.mean(o_t * o_t, axis=0, keepdims=True) + NORM_EPS)
            o_t = o_t * subln_ref[...] * (1.0 - DIFF_LAMBDA_INIT)
            o_ref[0, :, hd * LANES:(hd + 1) * LANES] = o_t.T.astype(BF16)


def _diff_attention(q_t, k, v_t, lam_vecs, subln):
    b, s, _ = k.shape
    t = min(ATTN_TILE, s)
    ii, jj = _causal_pairs(s // t)
    q_map = lambda bi, p, ii, jj: (bi, ii[p], 0)
    k_map = lambda bi, p, ii, jj: (bi, jj[p], 0)
    qt_map = lambda bi, p, ii, jj: (bi, 0, ii[p])
    kt_map = lambda bi, p, ii, jj: (bi, 0, jj[p])
    const = lambda bi, p, ii, jj: (0, 0)
    grid_spec = pltpu.PrefetchScalarGridSpec(
        num_scalar_prefetch=2,
        grid=(b, int(ii.shape[0])),
        in_specs=[pl.BlockSpec((4, DIFF_HEAD_DIM), const), pl.BlockSpec((LANES, 1), const),
                  pl.BlockSpec((1, HALF, t), qt_map), pl.BlockSpec((1, t, HALF), k_map),
                  pl.BlockSpec((1, HALF, t), kt_map)],
        out_specs=pl.BlockSpec((1, t, HALF), q_map),
        scratch_shapes=[pltpu.VMEM((DIFF_HEADS, 1, 2 * t), F32),
                        pltpu.VMEM((DIFF_HEADS, LANES + ONES_ROWS, 2 * t), F32)],
    )
    return pl.pallas_call(
        functools.partial(_diff_attn_body, tile=t),
        grid_spec=grid_spec,
        out_shape=jax.ShapeDtypeStruct((b, s, HALF), BF16),
        compiler_params=_params("parallel", "arbitrary", flags=ATTN_FLAGS),
        name="diff_attn",
    )(ii, jj, lam_vecs, subln.reshape(LANES, 1), q_t, k, v_t)


def _s5_discretize_body(lr_ref, li_ref, ldt_ref, br_ref, bi_ref, k_ref, pr_ref, pi_ref, bbr_ref, bbi_ref):
    lr, li = lr_ref[...], li_ref[...]
    dt = jnp.exp(ldt_ref[...])
    mag = jnp.exp(lr * dt)
    ar, ai = mag * jnp.cos(li * dt), mag * jnp.sin(li * dt)
    den = lr * lr + li * li
    gr = ((ar - 1.0) * lr + ai * li) / den
    gi = (ai * lr - (ar - 1.0) * li) / den
    br, bi = br_ref[...], bi_ref[...]
    bbr_ref[...] = gr * br - gi * bi
    bbi_ref[...] = gr * bi + gi * br
    steps = k_ref[...]
    mag_k = jnp.exp(lr * dt * steps)
    pr_ref[...] = mag_k * jnp.cos(li * dt * steps)
    pi_ref[...] = mag_k * jnp.sin(li * dt * steps)


def _s5_discretize(a_re, a_im, log_dt, b_re, b_im):
    col = lambda a: a.reshape(S5_NSTATE, 1)
    ldt = jnp.broadcast_to(log_dt[:, None], (S5_GROUPS, S5_STATE))
    steps = jnp.arange(1, S5_CHUNK + 1, dtype=F32).reshape(1, S5_CHUNK)
    pshape = jax.ShapeDtypeStruct((S5_NSTATE, S5_CHUNK), F32)
    bshape = jax.ShapeDtypeStruct((S5_NSTATE, S5_GROUP), F32)
    return pl.pallas_call(
        _s5_discretize_body,
        out_shape=[pshape, pshape, bshape, bshape],
        name="s5_discretize",
    )(col(a_re), col(a_im), col(ldt), b_re.reshape(S5_NSTATE, S5_GROUP), b_im.reshape(S5_NSTATE, S5_GROUP), steps)


def _s5_body(u_ref, pr_ref, pi_ref, bbr_ref, bbi_ref, cr_ref, ci_ref, d_ref, wglu_ref, bglu_ref,
             o_ref, us_ref, ys_ref, xr_ref, xi_ref, hin_ref, carry_ref, *, tile):
    n_chunks = tile // S5_CHUNK
    slabs = HALF // LANES
    gw = S5_NSTATE // slabs

    @pl.when(pl.program_id(1) == 0)
    def _():
        carry_ref[...] = jnp.zeros_like(carry_ref)

    u = u_ref[0]
    for g in range(slabs):
        us_ref[g] = u[:, g * LANES:(g + 1) * LANES]
    u_perm = jnp.concatenate(
        [jnp.concatenate([us_ref[g, pl.ds(st, n_chunks, stride=S5_CHUNK), :] for g in range(slabs)], axis=1)
         for st in range(S5_CHUNK)], axis=0).astype(BF16)

    for g in range(slabs):
        ug = u_perm[:, g * LANES:(g + 1) * LANES]
        xr_ref[:, g * gw:(g + 1) * gw] = jnp.dot(ug, bbr_ref[g], preferred_element_type=F32)
        xi_ref[:, g * gw:(g + 1) * gw] = jnp.dot(ug, bbi_ref[g], preferred_element_type=F32)

    ar, ai = pr_ref[0:1, :], pi_ref[0:1, :]
    hr, hi = xr_ref[0:n_chunks, :], xi_ref[0:n_chunks, :]
    for st in range(1, S5_CHUNK):
        rows = slice(st * n_chunks, (st + 1) * n_chunks)
        hr, hi = ar * hr - ai * hi + xr_ref[rows, :], ar * hi + ai * hr + xi_ref[rows, :]
        xr_ref[rows, :] = hr
        xi_ref[rows, :] = hi

    last = (S5_CHUNK - 1) * n_chunks
    ac_r, ac_i = pr_ref[S5_CHUNK - 1:S5_CHUNK, :], pi_ref[S5_CHUNK - 1:S5_CHUNK, :]

    def carry_step(c, carry):
        cr, ci = carry
        hin_ref[0, pl.ds(c, 1), :] = cr
        hin_ref[1, pl.ds(c, 1), :] = ci
        zr, zi = xr_ref[pl.ds(last + c, 1), :], xi_ref[pl.ds(last + c, 1), :]
        return ac_r * cr - ac_i * ci + zr, ac_r * ci + ac_i * cr + zi

    cr, ci = lax.fori_loop(0, n_chunks, carry_step, (carry_ref[0:1, :], carry_ref[1:2, :]), unroll=4)
    carry_ref[0:1, :] = cr
    carry_ref[1:2, :] = ci

    hin_r, hin_i = hin_ref[0], hin_ref[1]
    for st in range(S5_CHUNK):
        rows = slice(st * n_chunks, (st + 1) * n_chunks)
        p_r, p_i = pr_ref[st:st + 1, :], pi_ref[st:st + 1, :]
        xr_ref[rows, :] = xr_ref[rows, :] + (p_r * hin_r - p_i * hin_i)
        xi_ref[rows, :] = xi_ref[rows, :] + (p_r * hin_i + p_i * hin_r)
    for g in range(slabs):
        cols = slice(g * gw, (g + 1) * gw)
        yg = (jnp.dot(xr_ref[:, cols].astype(BF16), cr_ref[g], preferred_element_type=F32)
              - jnp.dot(xi_ref[:, cols].astype(BF16), ci_ref[g], preferred_element_type=F32))
        for st in range(S5_CHUNK):
            ys_ref[g, pl.ds(st, n_chunks, stride=S5_CHUNK), :] = yg[st * n_chunks:(st + 1) * n_chunks]

    y = jnp.concatenate([ys_ref[g] for g in range(slabs)], axis=1) + d_ref[...] * u
    z = jax.nn.gelu(y, approximate=True)
    gate = jnp.dot(z.astype(BF16), wglu_ref[...], preferred_element_type=F32) + bglu_ref[...]
    o_ref[0] = (z * jax.nn.sigmoid(gate)).astype(BF16)


def _s5(u, a_re, a_im, log_dt, b_re, b_im, c_re, c_im, d_skip, w_glu, b_glu):
    b, s, w = u.shape
    t = min(S5_TILE, s)
    slabs = w // LANES
    per = S5_GROUPS // slabs
    gw = S5_NSTATE // slabs
    pw_r, pw_i, bbr, bbi = _s5_discretize(a_re, a_im, log_dt, b_re, b_im)
    eye = jnp.eye(per, dtype=F32)

    def block_in(m):
        m = m.reshape(slabs, per, S5_STATE, S5_GROUP)
        return jnp.einsum('sgpc,gh->sgchp', m, eye).reshape(slabs, LANES, gw).astype(BF16)

    def block_out(m):
        m = m.reshape(slabs, per, S5_GROUP, S5_STATE)
        return jnp.einsum('sgcp,gh->sgphc', m, eye).reshape(slabs, gw, LANES).astype(BF16)

    seq_blk = lambda bi, ti: (bi, ti, 0)
    n_chunks = t // S5_CHUNK
    return pl.pallas_call(
        functools.partial(_s5_body, tile=t),
        grid=(b, s // t),
        in_specs=[pl.BlockSpec((1, t, w), seq_blk),
                  _resident((S5_CHUNK, S5_NSTATE)), _resident((S5_CHUNK, S5_NSTATE)),
                  _resident((slabs, LANES, gw)), _resident((slabs, LANES, gw)),
                  _resident((slabs, gw, LANES)), _resident((slabs, gw, LANES)),
                  _resident((1, w)), _resident((w, w)), _resident((1, w))],
        out_specs=pl.BlockSpec((1, t, w), seq_blk),
        out_shape=jax.ShapeDtypeStruct((b, s, w), BF16),
        scratch_shapes=[pltpu.VMEM((slabs, t, LANES), F32), pltpu.VMEM((slabs, t, LANES), F32),
                        pltpu.VMEM((t, S5_NSTATE), F32), pltpu.VMEM((t, S5_NSTATE), F32),
                        pltpu.VMEM((2, n_chunks, S5_NSTATE), F32), pltpu.VMEM((2, S5_NSTATE), F32)],
        compiler_params=_params("parallel", "arbitrary"),
        name="s5",
    )(u, pw_r.T, pw_i.T, block_in(bbr), block_in(bbi),
      block_out(c_re), block_out(c_im), d_skip.reshape(1, w), w_glu.astype(BF16), b_glu.reshape(1, w))


def _proj1_body(x_ref, g_ref, wt_ref, w_ref, qk_ref, vm_ref, om_ref, qft_ref, kf_ref, vft_ref, gt_ref):
    h = _rms(x_ref[...], g_ref[...]).astype(BF16)
    pt = lax.dot_general(wt_ref[...], h, NT_DIMS, preferred_element_type=F32)
    p = jnp.dot(h, w_ref[...], preferred_element_type=F32)
    qft_ref[0] = (pt[:HALF] * (FOX_HEAD_DIM ** -0.5 * LOG2_E)).astype(BF16)
    vft_ref[0] = pt[HALF:].astype(BF16)
    qk_ref[...] = p[:, 0:HALF]
    vm_ref[...] = p[:, HALF:2 * HALF].astype(BF16)
    om_ref[...] = p[:, 2 * HALF:3 * HALF]
    kf_ref[...] = p[:, 3 * HALF:4 * HALF].astype(BF16)
    gt_ref[...] = p[:, 4 * HALF:4 * HALF + GATE_COLS]


def _proj1(x, norm, w_in, seq):
    n, d = x.shape
    b = n // seq
    tm = min(TOKEN_TILE, seq)
    tps = seq // tm
    row = lambda i: (i, 0)
    seq_t = lambda i: (i // tps, 0, i % tps)
    o = 2 * HALF
    g = 2 * MLSTM_HEADS
    om = o + g
    w_n = jnp.concatenate(
        [w_in[:, :o], w_in[:, om:om + HALF], w_in[:, om + 2 * HALF:om + 3 * HALF],
         w_in[:, o:om], w_in[:, om + 4 * HALF:],
         jnp.zeros((d, GATE_COLS - g - FOX_HEADS), w_in.dtype)], axis=1).astype(BF16)
    w_t = jnp.concatenate([w_in[:, om + HALF:om + 2 * HALF], w_in[:, om + 3 * HALF:om + 4 * HALF]],
                          axis=1).T.astype(BF16)
    f32_half = jax.ShapeDtypeStruct((n, HALF), F32)
    bf_half = jax.ShapeDtypeStruct((n, HALF), BF16)
    t_shape = jax.ShapeDtypeStruct((b, HALF, seq), BF16)
    half_spec = pl.BlockSpec((tm, HALF), row)
    t_spec = pl.BlockSpec((1, HALF, tm), seq_t)
    return pl.pallas_call(
        _proj1_body,
        grid=(n // tm,),
        in_specs=[pl.BlockSpec((tm, d), row), _resident((1, d)), _resident((2 * HALF, d)),
                  _resident((d, 4 * HALF + GATE_COLS))],
        out_specs=[half_spec, half_spec, half_spec, t_spec, half_spec, t_spec,
                   pl.BlockSpec((tm, GATE_COLS), row)],
        out_shape=[f32_half, bf_half, f32_half, t_shape, bf_half, t_shape,
                   jax.ShapeDtypeStruct((n, GATE_COLS), F32)],
        compiler_params=_params("parallel"),
        name="proj1",
    )(x, norm.reshape(1, d), w_t, w_n)


def _mlstm_body(qk_ref, v_ref, og_ref, gt_ref, cw_ref, cb_ref, gb_ref, on_ref, out_ref,
                xbuf_ref, cn_ref, m_ref, *, chunk):
    pad = SUBLANES
    dk, dv = MLSTM_QK_DIM, MLSTM_V_DIM

    @pl.when(pl.program_id(1) == 0)
    def _():
        xbuf_ref[0:pad, :] = jnp.zeros((pad, HALF), F32)
        cn_ref[...] = jnp.zeros_like(cn_ref)
        m_ref[...] = jnp.zeros_like(m_ref)

    x = qk_ref[0]
    xbuf_ref[pad:pad + chunk, :] = x
    a = jnp.zeros_like(x) + cb_ref[...]
    for tap in range(MLSTM_CONV):
        a = a + cw_ref[tap:tap + 1, :] * xbuf_ref[pl.ds(pad - (MLSTM_CONV - 1) + tap, chunk), :]
    xbuf_ref[0:pad, :] = x[chunk - pad:chunk, :]
    a = a * jax.nn.sigmoid(a)
    q = a[:, :MLSTM_HEADS * dk] * dk ** -0.5
    k = a[:, MLSTM_HEADS * dk:]
    k_t = k.T

    gi = gt_ref[0] + gb_ref[...]
    bcum = _cumsum_rows(_log_sigmoid(gi))
    gi_t = gi.T
    bcum_t = bcum.T
    tri = (lax.broadcasted_iota(jnp.int32, (chunk, chunk), 1)
           <= lax.broadcasted_iota(jnp.int32, (chunk, chunk), 0))
    lane = lax.broadcasted_iota(jnp.int32, (1, LANES), 1)
    v_all = v_ref[0]
    o_gate = og_ref[0]
    ones = jnp.ones((chunk, dv), BF16)

    for hd in range(MLSTM_HEADS):
        pair, odd = hd // 2, hd % 2
        f_lane = MLSTM_HEADS + hd
        b_col = bcum[:, f_lane:f_lane + 1]
        b_row = bcum_t[f_lane:f_lane + 1, :]
        i_row = gi_t[hd:hd + 1, :]
        m_prev = m_ref[hd]
        dm = jnp.where(tri, b_col - b_row + i_row, -jnp.inf)
        g = b_col + m_prev
        mt = jnp.maximum(g, jnp.max(dm, axis=-1, keepdims=True))
        w_intra = jnp.exp(dm - mt)
        w_inter = jnp.exp(g - mt)

        sel = (lane >= dk) if odd else (lane < dk)
        q2 = q[:, pair * LANES:(pair + 1) * LANES]
        q2 = jnp.where(sel, q2, 0.0).astype(BF16)
        k2 = k[:, pair * LANES:(pair + 1) * LANES].astype(BF16)
        s = lax.dot_general(q2, k2, NT_DIMS, preferred_element_type=F32) * w_intra
        v_aug = jnp.concatenate([v_all[:, hd * dv:(hd + 1) * dv], ones], axis=1)
        state = cn_ref[pair]
        q_state = jnp.dot(q2, state.astype(BF16), preferred_element_type=F32)
        s_v = jnp.dot(s.astype(BF16), v_aug, preferred_element_type=F32)
        num = w_inter * q_state[:, :dv] + s_v[:, :dv]
        den = w_inter * q_state[:, dv:] + s_v[:, dv:]
        h_out = num / jnp.maximum(jnp.abs(den), jnp.exp(-mt))

        b_last = bcum[chunk - 1:chunk, f_lane:f_lane + 1]
        dec = b_last - b_row + i_row
        m_new = jnp.maximum(b_last + m_prev, jnp.max(dec, axis=-1, keepdims=True))
        w_s = jnp.exp(dec - m_new)
        carry_w = jnp.exp(b_last + m_prev - m_new)
        k_w = (k_t[hd * dk:(hd + 1) * dk, :] * w_s).astype(BF16)
        rows = slice(odd * dk, (odd + 1) * dk)
        cn_ref[pair, rows, :] = carry_w * state[rows, :] + jnp.dot(k_w, v_aug, preferred_element_type=F32)
        m_ref[hd] = m_new

        cols = slice(hd * dv, (hd + 1) * dv)
        h_norm = _rms(h_out, on_ref[hd:hd + 1, :])
        out_ref[0, :, cols] = (jax.nn.sigmoid(o_gate[:, cols]) * h_norm).astype(BF16)


def _mlstm(qk, v, o_gate, gates, conv_w, conv_b, b_i, b_f, out_norm):
    b, s, _ = qk.shape
    c = min(MLSTM_CHUNK, s)
    gate_bias = jnp.concatenate([b_i, b_f, jnp.zeros((GATE_COLS - 2 * MLSTM_HEADS,), F32)]).reshape(1, GATE_COLS)
    seq_blk = lambda bi, ci: (bi, ci, 0)
    half_spec = pl.BlockSpec((1, c, HALF), seq_blk)
    return pl.pallas_call(
        functools.partial(_mlstm_body, chunk=c),
        grid=(b, s // c),
        in_specs=[half_spec, half_spec, half_spec, pl.BlockSpec((1, c, GATE_COLS), seq_blk),
                  _resident((MLSTM_CONV, HALF)), _resident((1, HALF)), _resident((1, GATE_COLS)),
                  _resident((MLSTM_HEADS, MLSTM_V_DIM))],
        out_specs=half_spec,
        out_shape=jax.ShapeDtypeStruct((b, s, HALF), BF16),
        scratch_shapes=[pltpu.VMEM((c + SUBLANES, HALF), F32),
                        pltpu.VMEM((MLSTM_HEADS // 2, 2 * MLSTM_QK_DIM, 2 * MLSTM_V_DIM), F32),
                        pltpu.VMEM((MLSTM_HEADS, 1, 1), F32)],
        compiler_params=_params("parallel", "arbitrary"),
        name="mlstm",
    )(qk, v, o_gate, gates, conv_w, conv_b.reshape(1, HALF), gate_bias, out_norm)


FOX_GATE_LANE = 2 * MLSTM_HEADS
F_TERMS = 3


def _fox_cumsum_body(gt_ref, gb_ref, sel_ref, frow_ref, fk_ref):
    f = _cumsum_rows(_log_sigmoid(gt_ref[0] + gb_ref[...])) * LOG2_E
    hi = f.astype(BF16)
    rest = f - hi.astype(F32)
    mid = rest.astype(BF16)
    lo = (rest - mid.astype(F32)).astype(BF16)
    terms = jnp.concatenate([hi, mid, lo], axis=1)
    fk_ref[0] = jnp.dot(terms, sel_ref[...], preferred_element_type=F32).astype(BF16)
    frow_ref[0] = f.T[FOX_GATE_LANE:FOX_GATE_LANE + FOX_HEADS, :]


def _fox_cumsum(gates, fox_b_f):
    b, s, _ = gates.shape
    bias = jnp.concatenate([jnp.zeros((FOX_GATE_LANE,), F32), fox_b_f,
                            jnp.zeros((GATE_COLS - FOX_GATE_LANE - FOX_HEADS,), F32)]).reshape(1, GATE_COLS)
    src, dst = [], []
    for head in range(FOX_HEADS):
        for term in range(F_TERMS):
            src.append(term * GATE_COLS + FOX_GATE_LANE + head)
            dst.append((head // 2) * LANES + F_TERMS * (head % 2) + term)
    sel = jnp.zeros((F_TERMS * GATE_COLS, HALF), BF16).at[jnp.asarray(src), jnp.asarray(dst)].set(-1.0)
    return pl.pallas_call(
        _fox_cumsum_body,
        grid=(b,),
        in_specs=[pl.BlockSpec((1, s, GATE_COLS), lambda bi: (bi, 0, 0)), _resident((1, GATE_COLS)),
                  _resident((F_TERMS * GATE_COLS, HALF))],
        out_specs=[pl.BlockSpec((1, FOX_HEADS, s), lambda bi: (bi, 0, 0)),
                   pl.BlockSpec((1, s, HALF), lambda bi: (bi, 0, 0))],
        out_shape=[jax.ShapeDtypeStruct((b, FOX_HEADS, s), F32),
                   jax.ShapeDtypeStruct((b, s, HALF), BF16)],
        compiler_params=_params("parallel"),
        name="fox_cumsum",
    )(gates, bias, sel)


def _fox_attn_body(ii_ref, jj_ref, qt_ref, k_ref, fk_ref, vt_ref, frow_ref, o_ref,
                   m_ref, acc_ref, *, tile):
    p_idx = pl.program_id(1)
    i = ii_ref[p_idx]
    j = jj_ref[p_idx]
    dim = lax.broadcasted_iota(jnp.int32, (LANES, 1), 0)
    dh = FOX_HEAD_DIM

    @pl.when(j == 0)
    def _():
        m_ref[...] = jnp.full_like(m_ref, -jnp.inf)
        acc_ref[...] = jnp.zeros_like(acc_ref)

    def sweep(diagonal):
        if diagonal:
            query = lax.broadcasted_iota(jnp.int32, (tile, 2 * tile), 1)
            query = jnp.where(query >= tile, query - tile, query)
            keep = lax.broadcasted_iota(jnp.int32, (tile, 2 * tile), 0) <= query
        ones = jnp.ones((ONES_ROWS, tile), BF16)

        def scores(pair):
            rows = slice(pair * LANES, (pair + 1) * LANES)
            q_t = qt_ref[0, rows, :]
            k_aug = jnp.concatenate([k_ref[0, :, rows], fk_ref[0, :, rows]], axis=1)
            q_aug = []
            for c in range(2):
                ones_rows = jnp.logical_and(dim >= F_TERMS * c, dim < F_TERMS * (c + 1))
                q_aug.append(jnp.concatenate(
                    [_keep_half_rows(q_t, c),
                     jnp.broadcast_to(jnp.where(ones_rows, 1.0, 0.0), q_t.shape).astype(BF16)], axis=0))
            return jnp.dot(k_aug, jnp.concatenate(q_aug, axis=1), preferred_element_type=F32)

        def weights(pair, s):
            s = s + jnp.concatenate([frow_ref[0, 2 * pair:2 * pair + 1, :],
                                     frow_ref[0, 2 * pair + 1:2 * pair + 2, :]], axis=1)
            if diagonal:
                s = jnp.where(keep, s, -jnp.inf)
            m_old = m_ref[pair]
            m_new = jnp.maximum(m_old, jnp.max(s, axis=0, keepdims=True))
            m_ref[pair] = m_new
            return jnp.exp2(m_old - m_new), jnp.exp2(s - m_new).astype(BF16)

        def accumulate(pair, alpha, p):
            for c in range(2):
                hd = 2 * pair + c
                cols = slice(c * tile, (c + 1) * tile)
                v_aug = jnp.concatenate([vt_ref[0, hd * dh:(hd + 1) * dh, :], ones], axis=0)
                acc_ref[hd] = alpha[:, cols] * acc_ref[hd] + jnp.dot(v_aug, p[:, cols], preferred_element_type=F32)

        _staged(FOX_HEADS // 2, scores, weights, accumulate)

    @pl.when(j < i)
    def _():
        sweep(False)

    @pl.when(j == i)
    def _():
        sweep(True)
        for pair in range(FOX_HEADS // 2):
            a0, a1 = acc_ref[2 * pair], acc_ref[2 * pair + 1]
            o_t = jnp.concatenate([a0[:dh] / a0[dh:dh + 1], a1[:dh] / a1[dh:dh + 1]], axis=0)
            o_ref[0, :, pair * LANES:(pair + 1) * LANES] = o_t.T.astype(BF16)


def _fox_attention(q_t, k, fk, v_t, frow):
    b, s, _ = k.shape
    t = min(ATTN_TILE, s)
    ii, jj = _causal_pairs(s // t)
    q_map = lambda bi, p, ii, jj: (bi, ii[p], 0)
    k_map = lambda bi, p, ii, jj: (bi, jj[p], 0)
    qt_map = lambda bi, p, ii, jj: (bi, 0, ii[p])
    kt_map = lambda bi, p, ii, jj: (bi, 0, jj[p])
    grid_spec = pltpu.PrefetchScalarGridSpec(
        num_scalar_prefetch=2,
        grid=(b, int(ii.shape[0])),
        in_specs=[pl.BlockSpec((1, HALF, t), qt_map), pl.BlockSpec((1, t, HALF), k_map),
                  pl.BlockSpec((1, t, HALF), k_map), pl.BlockSpec((1, HALF, t), kt_map),
                  pl.BlockSpec((1, FOX_HEADS, t), qt_map)],
        out_specs=pl.BlockSpec((1, t, HALF), q_map),
        scratch_shapes=[pltpu.VMEM((FOX_HEADS // 2, 1, 2 * t), F32),
                        pltpu.VMEM((FOX_HEADS, FOX_HEAD_DIM + ONES_ROWS, t), F32)],
    )
    return pl.pallas_call(
        functools.partial(_fox_attn_body, tile=t),
        grid_spec=grid_spec,
        out_shape=jax.ShapeDtypeStruct((b, s, HALF), BF16),
        compiler_params=_params("parallel", "arbitrary", flags=ATTN_FLAGS),
        name="fox_attn",
    )(ii, jj, q_t, k, fk, v_t, frow)


def kernel(x, l0_ffn1_norm, l0_ffn1_w_gate, l0_ffn1_w_up, l0_ffn1_w_down, l0_mix_norm, l0_w_in, l0_diff_lambda_q1, l0_diff_lambda_k1, l0_diff_lambda_q2, l0_diff_lambda_k2, l0_diff_subln, l0_s5_a_re, l0_s5_a_im, l0_s5_log_dt, l0_s5_b_re, l0_s5_b_im, l0_s5_c_re, l0_s5_c_im, l0_s5_d, l0_s5_w_glu, l0_s5_b_glu, l0_w_out, l0_ffn2_norm, l0_ffn2_w_gate, l0_ffn2_w_up, l0_ffn2_w_down, l1_ffn1_norm, l1_ffn1_w_gate, l1_ffn1_w_up, l1_ffn1_w_down, l1_mix_norm, l1_w_in, l1_mlstm_conv_w, l1_mlstm_conv_b, l1_mlstm_b_i, l1_mlstm_b_f, l1_mlstm_out_norm, l1_fox_b_f, l1_w_out, l1_ffn2_norm, l1_ffn2_w_gate, l1_ffn2_w_up, l1_ffn2_w_down, final_norm):
    b, s, d = x.shape
    n = b * s
    seq = lambda a: a.reshape(b, s, a.shape[-1])
    flat = lambda a: a.reshape(n, a.shape[-1])

    x = _ffn(x.reshape(n, d), l0_ffn1_norm, l0_ffn1_w_gate, l0_ffn1_w_up, l0_ffn1_w_down)
    q_t, k, v_t, u = _proj0(x, l0_mix_norm, l0_w_in, s)
    lam_vecs = jnp.stack([l0_diff_lambda_q1, l0_diff_lambda_k1, l0_diff_lambda_q2, l0_diff_lambda_k2])
    ya = _diff_attention(q_t, seq(k), v_t, lam_vecs, l0_diff_subln)
    yb = _s5(seq(u), l0_s5_a_re, l0_s5_a_im, l0_s5_log_dt, l0_s5_b_re, l0_s5_b_im,
             l0_s5_c_re, l0_s5_c_im, l0_s5_d, l0_s5_w_glu, l0_s5_b_glu)
    x = _ffn(x, l0_ffn2_norm, l0_ffn2_w_gate, l0_ffn2_w_up, l0_ffn2_w_down,
             mix=(flat(ya), flat(yb), l0_w_out[:HALF], l0_w_out[HALF:]))

    x = _ffn(x, l1_ffn1_norm, l1_ffn1_w_gate, l1_ffn1_w_up, l1_ffn1_w_down)
    qk_m, v_m, o_m, qf_t, k_f, vf_t, gates = _proj1(x, l1_mix_norm, l1_w_in, s)
    hm = _mlstm(seq(qk_m), seq(v_m), seq(o_m), seq(gates), l1_mlstm_conv_w, l1_mlstm_conv_b,
                l1_mlstm_b_i, l1_mlstm_b_f, l1_mlstm_out_norm)
    frow, fk = _fox_cumsum(seq(gates), l1_fox_b_f)
    hf = _fox_attention(qf_t, seq(k_f), fk, vf_t, frow)
    x = _ffn(x, l1_ffn2_norm, l1_ffn2_w_gate, l1_ffn2_w_up, l1_ffn2_w_down,
             mix=(flat(hm), flat(hf), l1_w_out[:HALF], l1_w_out[HALF:]), final_norm=final_norm)
    return x.reshape(b, s, d)
```

```python
import functools
import math

import jax
import jax.numpy as jnp
from jax import lax
from jax.experimental import pallas as pl
from jax.experimental.pallas import tpu as pltpu

F32 = jnp.float32
BF16 = jnp.bfloat16

D_MODEL = 1024
D_FF = 2816
NORM_EPS = 1e-6
ROPE_THETA = 500000.0
DIFF_HEADS = 4
DIFF_HEAD_DIM = 64
ROT_DIM = DIFF_HEAD_DIM // 4
DIFF_LAMBDA_INIT = 0.8 - 0.6 * math.exp(-0.3 * 0)
S5_GROUPS = 32
S5_GROUP = 16
S5_STATE = 64
S5_NSTATE = S5_GROUPS * S5_STATE
MLSTM_HEADS = 4
MLSTM_QK_DIM = 64
MLSTM_V_DIM = 128
MLSTM_CONV = 4
FOX_HEADS = 8
FOX_HEAD_DIM = 64
HALF = 512

LANES = 128
SUBLANES = 8
VMEM_LIMIT_BYTES = 60 * 1024 * 1024

TOKEN_TILE = 1024
FFN_TILE = 512
FF_CHUNK = 256
ATTN_TILE = 512
S5_TILE = 512
S5_CHUNK = 16
MLSTM_CHUNK = 256
GATE_COLS = LANES

NT_DIMS = (((1,), (1,)), ((), ()))
LOG2_E = math.log2(math.e)
ONES_ROWS = 16


def _params(*semantics, flags=None):
    return pltpu.CompilerParams(dimension_semantics=semantics, vmem_limit_bytes=VMEM_LIMIT_BYTES,
                                flags=flags)


ATTN_FLAGS = None


def _resident(shape):
    return pl.BlockSpec(shape, lambda *_: (0,) * len(shape), pipeline_mode=pl.Buffered(1))


def _rms(x, g):
    return x * lax.rsqrt(jnp.mean(x * x, axis=-1, keepdims=True) + NORM_EPS) * g


def _log_sigmoid(x):
    return jnp.minimum(x, 0.0) - jnp.log1p(jnp.exp(-jnp.abs(x)))


def _cumsum_rows(x):
    n = x.shape[0]
    row = lax.broadcasted_iota(jnp.int32, x.shape, 0)
    k = 1
    while k < n:
        x = x + jnp.where(row >= k, pltpu.roll(x, k, 0), 0.0)
        k *= 2
    return x


def _keep_half_rows(x, half):
    rows = x.shape[0] // 2
    zeros = jnp.zeros((rows,) + x.shape[1:], x.dtype)
    return jnp.concatenate([x[:rows], zeros] if half == 0 else [zeros, x[rows:]], axis=0)


def _staged(n, scores, weights, accumulate):
    s, w = {}, {}
    for step in range(n + 2):
        if step < n:
            s[step] = scores(step)
        if step >= 2:
            accumulate(step - 2, *w.pop(step - 2))
        if 1 <= step <= n:
            w[step - 1] = weights(step - 1, s.pop(step - 1))


def _causal_pairs(n_blocks):
    ii, jj = [], []
    for i in range(n_blocks):
        for j in range(i + 1):
            ii.append(i)
            jj.append(j)
    return jnp.asarray(ii, jnp.int32), jnp.asarray(jj, jnp.int32)


def _ffn_body(*refs, has_mix, has_final):
    refs = list(refs)
    x_ref = refs.pop(0)
    if has_mix:
        a_ref, b_ref, wa_ref, wb_ref = refs[:4]
        refs = refs[4:]
    g_ref, wg_ref, wu_ref, wd_ref = refs[:4]
    refs = refs[4:]
    if has_final:
        gf_ref = refs.pop(0)
    o_ref = refs.pop(0)

    x = x_ref[...]
    if has_mix:
        x = x + jnp.dot(a_ref[...], wa_ref[...], preferred_element_type=F32)
        x = x + jnp.dot(b_ref[...], wb_ref[...], preferred_element_type=F32)
    h = _rms(x, g_ref[...]).astype(BF16)
    acc = jnp.zeros_like(x)
    for c in range(D_FF // FF_CHUNK):
        sl = slice(c * FF_CHUNK, (c + 1) * FF_CHUNK)
        gate = jnp.dot(h, wg_ref[:, sl].astype(BF16), preferred_element_type=F32)
        up = jnp.dot(h, wu_ref[:, sl].astype(BF16), preferred_element_type=F32)
        act = (gate * jax.nn.sigmoid(gate) * up).astype(BF16)
        acc = acc + jnp.dot(act, wd_ref[sl, :].astype(BF16), preferred_element_type=F32)
    y = x + 0.5 * acc
    if has_final:
        y = _rms(y, gf_ref[...])
    o_ref[...] = y


def _ffn(x, norm, wg, wu, wd, mix=None, final_norm=None):
    n, d = x.shape
    tm = min(FFN_TILE, n)
    row = lambda i: (i, 0)
    args = [x]
    specs = [pl.BlockSpec((tm, d), row)]
    if mix is not None:
        a, b, wa, wb = mix
        args += [a, b, wa.astype(BF16), wb.astype(BF16)]
        specs += [pl.BlockSpec((tm, HALF), row), pl.BlockSpec((tm, HALF), row),
                  _resident((HALF, d)), _resident((HALF, d))]
    args += [norm.reshape(1, d), wg, wu, wd]
    specs += [_resident((1, d)), _resident((d, D_FF)), _resident((d, D_FF)), _resident((D_FF, d))]
    if final_norm is not None:
        args.append(final_norm.reshape(1, d))
        specs.append(_resident((1, d)))
    body = functools.partial(_ffn_body, has_mix=mix is not None, has_final=final_norm is not None)
    return pl.pallas_call(
        body,
        grid=(n // tm,),
        in_specs=specs,
        out_specs=pl.BlockSpec((tm, d), row),
        out_shape=jax.ShapeDtypeStruct((n, d), F32),
        compiler_params=_params("parallel"),
        name="ffn",
    )(*args)


def _rope_angles(seq):
    pos = jnp.arange(seq, dtype=F32)
    inv = ROPE_THETA ** (-jnp.arange(0, ROT_DIM, 2, dtype=F32) / ROT_DIM)
    ang = pos[:, None] * inv[None, :]
    return jnp.cos(ang), jnp.sin(ang)


def _rope_lane_tables(seq):
    half = ROT_DIM // 2
    cos, sin = _rope_angles(seq)
    one = jnp.ones((seq, DIFF_HEAD_DIM - ROT_DIM), F32)
    zero8 = jnp.zeros((seq, half), F32)
    zero48 = jnp.zeros((seq, DIFF_HEAD_DIM - ROT_DIM), F32)
    cos_t = jnp.concatenate([cos, cos, one] * 2, axis=1)
    sa_t = jnp.concatenate([-sin, zero8, zero48] * 2, axis=1)
    sb_t = jnp.concatenate([zero8, sin, zero48] * 2, axis=1)
    return cos_t, sa_t, sb_t


def _proj0_body(x_ref, g_ref, wt_ref, w_ref, cos_ref, sa_ref, sb_ref, cosr_ref, sinr_ref,
                qt_ref, k_ref, vt_ref, u_ref):
    h = _rms(x_ref[...], g_ref[...]).astype(BF16)
    pt = lax.dot_general(wt_ref[...], h, NT_DIMS, preferred_element_type=F32)
    p = jnp.dot(h, w_ref[...], preferred_element_type=F32)
    half = ROT_DIM // 2
    scale = DIFF_HEAD_DIM ** -0.5 * LOG2_E

    cos_r, sin_r = cosr_ref[...], sinr_ref[...]
    for comp in range(2 * DIFF_HEADS):
        r0 = comp * DIFF_HEAD_DIM
        x1, x2 = pt[r0:r0 + half], pt[r0 + half:r0 + ROT_DIM]
        qt_ref[0, r0:r0 + half, :] = ((x1 * cos_r - x2 * sin_r) * scale).astype(BF16)
        qt_ref[0, r0 + half:r0 + ROT_DIM, :] = ((x2 * cos_r + x1 * sin_r) * scale).astype(BF16)
        qt_ref[0, r0 + ROT_DIM:r0 + DIFF_HEAD_DIM, :] = (pt[r0 + ROT_DIM:r0 + DIFF_HEAD_DIM] * scale).astype(BF16)
    vt_ref[0] = pt[HALF:2 * HALF].astype(BF16)

    cos_t, sa_t, sb_t = cos_ref[...], sa_ref[...], sb_ref[...]
    for hd in range(DIFF_HEADS):
        sl = slice(hd * LANES, (hd + 1) * LANES)
        t = p[:, sl]
        k_ref[:, sl] = (t * cos_t + pltpu.roll(t, LANES - half, 1) * sa_t
                        + pltpu.roll(t, half, 1) * sb_t).astype(BF16)
    u_ref[...] = p[:, HALF:2 * HALF]


def _proj0(x, norm, w_in, seq):
    n, d = x.shape
    b = n // seq
    tm = min(TOKEN_TILE, seq)
    tps = seq // tm
    row = lambda i: (i, 0)
    pos = lambda i: (i % tps, 0)
    pos_t = lambda i: (0, i % tps)
    seq_t = lambda i: (i // tps, 0, i % tps)
    cos, sin = _rope_angles(seq)
    w_t = jnp.concatenate([w_in[:, :HALF], w_in[:, 2 * HALF:3 * HALF]], axis=1).T.astype(BF16)
    w_n = jnp.concatenate([w_in[:, HALF:2 * HALF], w_in[:, 3 * HALF:]], axis=1).astype(BF16)
    t_shape = jax.ShapeDtypeStruct((b, HALF, seq), BF16)
    return pl.pallas_call(
        _proj0_body,
        grid=(n // tm,),
        in_specs=[pl.BlockSpec((tm, d), row), _resident((1, d)), _resident((2 * HALF, d)),
                  _resident((d, 2 * HALF)),
                  pl.BlockSpec((tm, LANES), pos), pl.BlockSpec((tm, LANES), pos),
                  pl.BlockSpec((tm, LANES), pos),
                  pl.BlockSpec((ROT_DIM // 2, tm), pos_t), pl.BlockSpec((ROT_DIM // 2, tm), pos_t)],
        out_specs=[pl.BlockSpec((1, HALF, tm), seq_t), pl.BlockSpec((tm, HALF), row),
                   pl.BlockSpec((1, HALF, tm), seq_t), pl.BlockSpec((tm, HALF), row)],
        out_shape=[t_shape, jax.ShapeDtypeStruct((n, HALF), BF16), t_shape,
                   jax.ShapeDtypeStruct((n, HALF), F32)],
        compiler_params=_params("parallel"),
        name="proj0",
    )(x, norm.reshape(1, d), w_t, w_n, *_rope_lane_tables(seq), cos.T, sin.T)


def _diff_attn_body(ii_ref, jj_ref, lam_ref, subln_ref, qt_ref, k_ref, vt_ref, o_ref,
                    m_ref, acc_ref, *, tile):
    p_idx = pl.program_id(1)
    i = ii_ref[p_idx]
    j = jj_ref[p_idx]

    @pl.when(j == 0)
    def _():
        m_ref[...] = jnp.full_like(m_ref, -jnp.inf)
        acc_ref[...] = jnp.zeros_like(acc_ref)

    def sweep(diagonal):
        if diagonal:
            query = lax.broadcasted_iota(jnp.int32, (tile, 2 * tile), 1)
            query = jnp.where(query >= tile, query - tile, query)
            keep = lax.broadcasted_iota(jnp.int32, (tile, 2 * tile), 0) <= query
        ones = jnp.ones((ONES_ROWS, tile), BF16)

        def scores(hd):
            rows = slice(hd * LANES, (hd + 1) * LANES)
            q_t = qt_ref[0, rows, :]
            q_both = jnp.concatenate([_keep_half_rows(q_t, 0), _keep_half_rows(q_t, 1)], axis=1)
            return jnp.dot(k_ref[0, :, rows], q_both, preferred_element_type=F32)

        def weights(hd, s):
            if diagonal:
                s = jnp.where(keep, s, -jnp.inf)
            m_old = m_ref[hd]
            m_new = jnp.maximum(m_old, jnp.max(s, axis=0, keepdims=True))
            m_ref[hd] = m_new
            return jnp.exp2(m_old - m_new), jnp.exp2(s - m_new).astype(BF16)

        def accumulate(hd, alpha, p):
            v_aug = jnp.concatenate([vt_ref[0, hd * LANES:(hd + 1) * LANES, :], ones], axis=0)
            acc_ref[hd] = alpha * acc_ref[hd] + jnp.dot(v_aug, p, preferred_element_type=F32)

        _staged(DIFF_HEADS, scores, weights, accumulate)

    @pl.when(j < i)
    def _():
        sweep(False)

    @pl.when(j == i)
    def _():
        sweep(True)
        lam_v = lam_ref[...]
        lam = (jnp.exp(jnp.sum(lam_v[0:1] * lam_v[1:2], keepdims=True))
               - jnp.exp(jnp.sum(lam_v[2:3] * lam_v[3:4], keepdims=True)) + DIFF_LAMBDA_INIT)
        for hd in range(DIFF_HEADS):
            a1, a2 = acc_ref[hd, :, :tile], acc_ref[hd, :, tile:]
            o_t = (a1[:LANES] / a1[LANES:LANES + 1] - lam * (a2[:LANES] / a2[LANES:LANES + 1]))
            o_t = o_t * lax.rsqrt(jnp.mean(o_t * o_t, axis=0, keepdims=True) + NORM_EPS)
            o_t = o_t * subln_ref[...] * (1.0 - DIFF_LAMBDA_INIT)
            o_ref[0, :, hd * LANES:(hd + 1) * LANES] = o_t.T.astype(BF16)


def _diff_attention(q_t, k, v_t, lam_vecs, subln):
    b, s, _ = k.shape
    t = min(ATTN_TILE, s)
    ii, jj = _causal_pairs(s // t)
    q_map = lambda bi, p, ii, jj: (bi, ii[p], 0)
    k_map = lambda bi, p, ii, jj: (bi, jj[p], 0)
    qt_map = lambda bi, p, ii, jj: (bi, 0, ii[p])
    kt_map = lambda bi, p, ii, jj: (bi, 0, jj[p])
    const = lambda bi, p, ii, jj: (0, 0)
    grid_spec = pltpu.PrefetchScalarGridSpec(
        num_scalar_prefetch=2,
        grid=(b, int(ii.shape[0])),
        in_specs=[pl.BlockSpec((4, DIFF_HEAD_DIM), const), pl.BlockSpec((LANES, 1), const),
                  pl.BlockSpec((1, HALF, t), qt_map), pl.BlockSpec((1, t, HALF), k_map),
                  pl.BlockSpec((1, HALF, t), kt_map)],
        out_specs=pl.BlockSpec((1, t, HALF), q_map),
        scratch_shapes=[pltpu.VMEM((DIFF_HEADS, 1, 2 * t), F32),
                        pltpu.VMEM((DIFF_HEADS, LANES + ONES_ROWS, 2 * t), F32)],
    )
    return pl.pallas_call(
        functools.partial(_diff_attn_body, tile=t),
        grid_spec=grid_spec,
        out_shape=jax.ShapeDtypeStruct((b, s, HALF), BF16),
        compiler_params=_params("parallel", "arbitrary", flags=ATTN_FLAGS),
        name="diff_attn",
    )(ii, jj, lam_vecs, subln.reshape(LANES, 1), q_t, k, v_t)


def _s5_discretize_body(lr_ref, li_ref, ldt_ref, br_ref, bi_ref, k_ref, pr_ref, pi_ref, bbr_ref, bbi_ref):
    lr, li = lr_ref[...], li_ref[...]
    dt = jnp.exp(ldt_ref[...])
    mag = jnp.exp(lr * dt)
    ar, ai = mag * jnp.cos(li * dt), mag * jnp.sin(li * dt)
    den = lr * lr + li * li
    gr = ((ar - 1.0) * lr + ai * li) / den
    gi = (ai * lr - (ar - 1.0) * li) / den
    br, bi = br_ref[...], bi_ref[...]
    bbr_ref[...] = gr * br - gi * bi
    bbi_ref[...] = gr * bi + gi * br
    steps = k_ref[...]
    mag_k = jnp.exp(lr * dt * steps)
    pr_ref[...] = mag_k * jnp.cos(li * dt * steps)
    pi_ref[...] = mag_k * jnp.sin(li * dt * steps)


def _s5_discretize(a_re, a_im, log_dt, b_re, b_im):
    col = lambda a: a.reshape(S5_NSTATE, 1)
    ldt = jnp.broadcast_to(log_dt[:, None], (S5_GROUPS, S5_STATE))
    steps = jnp.arange(1, S5_CHUNK + 1, dtype=F32).reshape(1, S5_CHUNK)
    pshape = jax.ShapeDtypeStruct((S5_NSTATE, S5_CHUNK), F32)
    bshape = jax.ShapeDtypeStruct((S5_NSTATE, S5_GROUP), F32)
    return pl.pallas_call(
        _s5_discretize_body,
        out_shape=[pshape, pshape, bshape, bshape],
        name="s5_discretize",
    )(col(a_re), col(a_im), col(ldt), b_re.reshape(S5_NSTATE, S5_GROUP), b_im.reshape(S5_NSTATE, S5_GROUP), steps)


def _s5_body(u_ref, pr_ref, pi_ref, bbr_ref, bbi_ref, cr_ref, ci_ref, d_ref, wglu_ref, bglu_ref,
             o_ref, us_ref, ys_ref, xr_ref, xi_ref, hin_ref, carry_ref, *, tile):
    n_chunks = tile // S5_CHUNK
    slabs = HALF // LANES
    gw = S5_NSTATE // slabs

    @pl.when(pl.program_id(1) == 0)
    def _():
        carry_ref[...] = jnp.zeros_like(carry_ref)

    u = u_ref[0]
    for g in range(slabs):
        us_ref[g] = u[:, g * LANES:(g + 1) * LANES]
    u_perm = jnp.concatenate(
        [jnp.concatenate([us_ref[g, pl.ds(st, n_chunks, stride=S5_CHUNK), :] for g in range(slabs)], axis=1)
         for st in range(S5_CHUNK)], axis=0).astype(BF16)

    for g in range(slabs):
        ug = u_perm[:, g * LANES:(g + 1) * LANES]
        xr_ref[:, g * gw:(g + 1) * gw] = jnp.dot(ug, bbr_ref[g], preferred_element_type=F32)
        xi_ref[:, g * gw:(g + 1) * gw] = jnp.dot(ug, bbi_ref[g], preferred_element_type=F32)

    ar, ai = pr_ref[0:1, :], pi_ref[0:1, :]
    hr, hi = xr_ref[0:n_chunks, :], xi_ref[0:n_chunks, :]
    for st in range(1, S5_CHUNK):
        rows = slice(st * n_chunks, (st + 1) * n_chunks)
        hr, hi = ar * hr - ai * hi + xr_ref[rows, :], ar * hi + ai * hr + xi_ref[rows, :]
        xr_ref[rows, :] = hr
        xi_ref[rows, :] = hi

    last = (S5_CHUNK - 1) * n_chunks
    ac_r, ac_i = pr_ref[S5_CHUNK - 1:S5_CHUNK, :], pi_ref[S5_CHUNK - 1:S5_CHUNK, :]

    def carry_step(c, carry):
        cr, ci = carry
        hin_ref[0, pl.ds(c, 1), :] = cr
        hin_ref[1, pl.ds(c, 1), :] = ci
        zr, zi = xr_ref[pl.ds(last + c, 1), :], xi_ref[pl.ds(last + c, 1), :]
        return ac_r * cr - ac_i * ci + zr, ac_r * ci + ac_i * cr + zi

    cr, ci = lax.fori_loop(0, n_chunks, carry_step, (carry_ref[0:1, :], carry_ref[1:2, :]), unroll=4)
    carry_ref[0:1, :] = cr
    carry_ref[1:2, :] = ci

    hin_r, hin_i = hin_ref[0], hin_ref[1]
    for st in range(S5_CHUNK):
        rows = slice(st * n_chunks, (st + 1) * n_chunks)
        p_r, p_i = pr_ref[st:st + 1, :], pi_ref[st:st + 1, :]
        xr_ref[rows, :] = xr_ref[rows, :] + (p_r * hin_r - p_i * hin_i)
        xi_ref[rows, :] = xi_ref[rows, :] + (p_r * hin_i + p_i * hin_r)
    for g in range(slabs):
        cols = slice(g * gw, (g + 1) * gw)
        yg = (jnp.dot(xr_ref[:, cols].astype(BF16), cr_ref[g], preferred_element_type=F32)
              - jnp.dot(xi_ref[:, cols].astype(BF16), ci_ref[g], preferred_element_type=F32))
        for st in range(S5_CHUNK):
            ys_ref[g, pl.ds(st, n_chunks, stride=S5_CHUNK), :] = yg[st * n_chunks:(st + 1) * n_chunks]

    y = jnp.concatenate([ys_ref[g] for g in range(slabs)], axis=1) + d_ref[...] * u
    z = jax.nn.gelu(y, approximate=True)
    gate = jnp.dot(z.astype(BF16), wglu_ref[...], preferred_element_type=F32) + bglu_ref[...]
    o_ref[0] = (z * jax.nn.sigmoid(gate)).astype(BF16)


def _s5(u, a_re, a_im, log_dt, b_re, b_im, c_re, c_im, d_skip, w_glu, b_glu):
    b, s, w = u.shape
    t = min(S5_TILE, s)
    slabs = w // LANES
    per = S5_GROUPS // slabs
    gw = S5_NSTATE // slabs
    pw_r, pw_i, bbr, bbi = _s5_discretize(a_re, a_im, log_dt, b_re, b_im)
    eye = jnp.eye(per, dtype=F32)

    def block_in(m):
        m = m.reshape(slabs, per, S5_STATE, S5_GROUP)
        return jnp.einsum('sgpc,gh->sgchp', m, eye).reshape(slabs, LANES, gw).astype(BF16)

    def block_out(m):
        m = m.reshape(slabs, per, S5_GROUP, S5_STATE)
        return jnp.einsum('sgcp,gh->sgphc', m, eye).reshape(slabs, gw, LANES).astype(BF16)

    seq_blk = lambda bi, ti: (bi, ti, 0)
    n_chunks = t // S5_CHUNK
    return pl.pallas_call(
        functools.partial(_s5_body, tile=t),
        grid=(b, s // t),
        in_specs=[pl.BlockSpec((1, t, w), seq_blk),
                  _resident((S5_CHUNK, S5_NSTATE)), _resident((S5_CHUNK, S5_NSTATE)),
                  _resident((slabs, LANES, gw)), _resident((slabs, LANES, gw)),
                  _resident((slabs, gw, LANES)), _resident((slabs, gw, LANES)),
                  _resident((1, w)), _resident((w, w)), _resident((1, w))],
        out_specs=pl.BlockSpec((1, t, w), seq_blk),
        out_shape=jax.ShapeDtypeStruct((b, s, w), BF16),
        scratch_shapes=[pltpu.VMEM((slabs, t, LANES), F32), pltpu.VMEM((slabs, t, LANES), F32),
                        pltpu.VMEM((t, S5_NSTATE), F32), pltpu.VMEM((t, S5_NSTATE), F32),
                        pltpu.VMEM((2, n_chunks, S5_NSTATE), F32), pltpu.VMEM((2, S5_NSTATE), F32)],
        compiler_params=_params("parallel", "arbitrary"),
        name="s5",
    )(u, pw_r.T, pw_i.T, block_in(bbr), block_in(bbi),
      block_out(c_re), block_out(c_im), d_skip.reshape(1, w), w_glu.astype(BF16), b_glu.reshape(1, w))


def _proj1_body(x_ref, g_ref, wt_ref, w_ref, qk_ref, vm_ref, om_ref, qft_ref, kf_ref, vft_ref, gt_ref):
    h = _rms(x_ref[...], g_ref[...]).astype(BF16)
    pt = lax.dot_general(wt_ref[...], h, NT_DIMS, preferred_element_type=F32)
    p = jnp.dot(h, w_ref[...], preferred_element_type=F32)
    qft_ref[0] = (pt[:HALF] * (FOX_HEAD_DIM ** -0.5 * LOG2_E)).astype(BF16)
    vft_ref[0] = pt[HALF:].astype(BF16)
    qk_ref[...] = p[:, 0:HALF]
    vm_ref[...] = p[:, HALF:2 * HALF].astype(BF16)
    om_ref[...] = p[:, 2 * HALF:3 * HALF]
    kf_ref[...] = p[:, 3 * HALF:4 * HALF].astype(BF16)
    gt_ref[...] = p[:, 4 * HALF:4 * HALF + GATE_COLS]


def _proj1(x, norm, w_in, seq):
    n, d = x.shape
    b = n // seq
    tm = min(TOKEN_TILE, seq)
    tps = seq // tm
    row = lambda i: (i, 0)
    seq_t = lambda i: (i // tps, 0, i % tps)
    o = 2 * HALF
    g = 2 * MLSTM_HEADS
    om = o + g
    w_n = jnp.concatenate(
        [w_in[:, :o], w_in[:, om:om + HALF], w_in[:, om + 2 * HALF:om + 3 * HALF],
         w_in[:, o:om], w_in[:, om + 4 * HALF:],
         jnp.zeros((d, GATE_COLS - g - FOX_HEADS), w_in.dtype)], axis=1).astype(BF16)
    w_t = jnp.concatenate([w_in[:, om + HALF:om + 2 * HALF], w_in[:, om + 3 * HALF:om + 4 * HALF]],
                          axis=1).T.astype(BF16)
    f32_half = jax.ShapeDtypeStruct((n, HALF), F32)
    bf_half = jax.ShapeDtypeStruct((n, HALF), BF16)
    t_shape = jax.ShapeDtypeStruct((b, HALF, seq), BF16)
    half_spec = pl.BlockSpec((tm, HALF), row)
    t_spec = pl.BlockSpec((1, HALF, tm), seq_t)
    return pl.pallas_call(
        _proj1_body,
        grid=(n // tm,),
        in_specs=[pl.BlockSpec((tm, d), row), _resident((1, d)), _resident((2 * HALF, d)),
                  _resident((d, 4 * HALF + GATE_COLS))],
        out_specs=[half_spec, half_spec, half_spec, t_spec, half_spec, t_spec,
                   pl.BlockSpec((tm, GATE_COLS), row)],
        out_shape=[f32_half, bf_half, f32_half, t_shape, bf_half, t_shape,
                   jax.ShapeDtypeStruct((n, GATE_COLS), F32)],
        compiler_params=_params("parallel"),
        name="proj1",
    )(x, norm.reshape(1, d), w_t, w_n)


def _mlstm_body(qk_ref, v_ref, og_ref, gt_ref, cw_ref, cb_ref, gb_ref, on_ref, out_ref,
                xbuf_ref, cn_ref, m_ref, *, chunk):
    pad = SUBLANES
    dk, dv = MLSTM_QK_DIM, MLSTM_V_DIM

    @pl.when(pl.program_id(1) == 0)
    def _():
        xbuf_ref[0:pad, :] = jnp.zeros((pad, HALF), F32)
        cn_ref[...] = jnp.zeros_like(cn_ref)
        m_ref[...] = jnp.zeros_like(m_ref)

    x = qk_ref[0]
    xbuf_ref[pad:pad + chunk, :] = x
    a = jnp.zeros_like(x) + cb_ref[...]
    for tap in range(MLSTM_CONV):
        a = a + cw_ref[tap:tap + 1, :] * xbuf_ref[pl.ds(pad - (MLSTM_CONV - 1) + tap, chunk), :]
    xbuf_ref[0:pad, :] = x[chunk - pad:chunk, :]
    a = a * jax.nn.sigmoid(a)
    q = a[:, :MLSTM_HEADS * dk] * dk ** -0.5
    k = a[:, MLSTM_HEADS * dk:]
    k_t = k.T

    gi = gt_ref[0] + gb_ref[...]
    bcum = _cumsum_rows(_log_sigmoid(gi))
    gi_t = gi.T
    bcum_t = bcum.T
    tri = (lax.broadcasted_iota(jnp.int32, (chunk, chunk), 1)
           <= lax.broadcasted_iota(jnp.int32, (chunk, chunk), 0))
    lane = lax.broadcasted_iota(jnp.int32, (1, LANES), 1)
    v_all = v_ref[0]
    o_gate = og_ref[0]
    ones = jnp.ones((chunk, dv), BF16)

    states = [cn_ref[pair] for pair in range(MLSTM_HEADS // 2)]

    def products(hd):
        pair, odd = hd // 2, hd % 2
        sel = (lane >= dk) if odd else (lane < dk)
        q2 = jnp.where(sel, q[:, pair * LANES:(pair + 1) * LANES], 0.0).astype(BF16)
        k2 = k[:, pair * LANES:(pair + 1) * LANES].astype(BF16)
        qk = lax.dot_general(q2, k2, NT_DIMS, preferred_element_type=F32)
        return qk, jnp.dot(q2, states[pair].astype(BF16), preferred_element_type=F32)

    def gate_weights(hd, prods):
        qk, q_state = prods
        f_lane = MLSTM_HEADS + hd
        b_col = bcum[:, f_lane:f_lane + 1]
        b_row = bcum_t[f_lane:f_lane + 1, :]
        i_row = gi_t[hd:hd + 1, :]
        m_prev = m_ref[hd]
        dm = jnp.where(tri, b_col - b_row + i_row, -jnp.inf)
        g = b_col + m_prev
        mt = jnp.maximum(g, jnp.max(dm, axis=-1, keepdims=True))
        s = (qk * jnp.exp(dm - mt)).astype(BF16)
        w_inter = jnp.exp(g - mt)

        b_last = bcum[chunk - 1:chunk, f_lane:f_lane + 1]
        dec = b_last - b_row + i_row
        m_new = jnp.maximum(b_last + m_prev, jnp.max(dec, axis=-1, keepdims=True))
        k_w = (k_t[hd * dk:(hd + 1) * dk, :] * jnp.exp(dec - m_new)).astype(BF16)
        carry_w = jnp.exp(b_last + m_prev - m_new)
        m_ref[hd] = m_new
        return s, k_w, q_state, w_inter, mt, carry_w

    def outputs(hd, s, k_w, q_state, w_inter, mt, carry_w):
        pair, odd = hd // 2, hd % 2
        v_aug = jnp.concatenate([v_all[:, hd * dv:(hd + 1) * dv], ones], axis=1)
        s_v = jnp.dot(s, v_aug, preferred_element_type=F32)
        num = w_inter * q_state[:, :dv] + s_v[:, :dv]
        den = w_inter * q_state[:, dv:] + s_v[:, dv:]
        h_out = num / jnp.maximum(jnp.abs(den), jnp.exp(-mt))
        rows = slice(odd * dk, (odd + 1) * dk)
        cn_ref[pair, rows, :] = (carry_w * states[pair][rows, :]
                                 + jnp.dot(k_w, v_aug, preferred_element_type=F32))
        cols = slice(hd * dv, (hd + 1) * dv)
        h_norm = _rms(h_out, on_ref[hd:hd + 1, :])
        out_ref[0, :, cols] = (jax.nn.sigmoid(o_gate[:, cols]) * h_norm).astype(BF16)

    _staged(MLSTM_HEADS, products, gate_weights, outputs)


def _mlstm(qk, v, o_gate, gates, conv_w, conv_b, b_i, b_f, out_norm):
    b, s, _ = qk.shape
    c = min(MLSTM_CHUNK, s)
    gate_bias = jnp.concatenate([b_i, b_f, jnp.zeros((GATE_COLS - 2 * MLSTM_HEADS,), F32)]).reshape(1, GATE_COLS)
    seq_blk = lambda bi, ci: (bi, ci, 0)
    half_spec = pl.BlockSpec((1, c, HALF), seq_blk)
    return pl.pallas_call(
        functools.partial(_mlstm_body, chunk=c),
        grid=(b, s // c),
        in_specs=[half_spec, half_spec, half_spec, pl.BlockSpec((1, c, GATE_COLS), seq_blk),
                  _resident((MLSTM_CONV, HALF)), _resident((1, HALF)), _resident((1, GATE_COLS)),
                  _resident((MLSTM_HEADS, MLSTM_V_DIM))],
        out_specs=half_spec,
        out_shape=jax.ShapeDtypeStruct((b, s, HALF), BF16),
        scratch_shapes=[pltpu.VMEM((c + SUBLANES, HALF), F32),
                        pltpu.VMEM((MLSTM_HEADS // 2, 2 * MLSTM_QK_DIM, 2 * MLSTM_V_DIM), F32),
                        pltpu.VMEM((MLSTM_HEADS, 1, 1), F32)],
        compiler_params=_params("parallel", "arbitrary"),
        name="mlstm",
    )(qk, v, o_gate, gates, conv_w, conv_b.reshape(1, HALF), gate_bias, out_norm)


FOX_GATE_LANE = 2 * MLSTM_HEADS
F_TERMS = 3


def _fox_cumsum_body(gt_ref, gb_ref, sel_ref, frow_ref, fk_ref):
    f = _cumsum_rows(_log_sigmoid(gt_ref[0] + gb_ref[...])) * LOG2_E
    hi = f.astype(BF16)
    rest = f - hi.astype(F32)
    mid = rest.astype(BF16)
    lo = (rest - mid.astype(F32)).astype(BF16)
    terms = jnp.concatenate([hi, mid, lo], axis=1)
    fk_ref[0] = jnp.dot(terms, sel_ref[...], preferred_element_type=F32).astype(BF16)
    frow_ref[0] = f.T[FOX_GATE_LANE:FOX_GATE_LANE + FOX_HEADS, :]


def _fox_cumsum(gates, fox_b_f):
    b, s, _ = gates.shape
    bias = jnp.concatenate([jnp.zeros((FOX_GATE_LANE,), F32), fox_b_f,
                            jnp.zeros((GATE_COLS - FOX_GATE_LANE - FOX_HEADS,), F32)]).reshape(1, GATE_COLS)
    src, dst = [], []
    for head in range(FOX_HEADS):
        for term in range(F_TERMS):
            src.append(term * GATE_COLS + FOX_GATE_LANE + head)
            dst.append((head // 2) * LANES + F_TERMS * (head % 2) + term)
    sel = jnp.zeros((F_TERMS * GATE_COLS, HALF), BF16).at[jnp.asarray(src), jnp.asarray(dst)].set(-1.0)
    return pl.pallas_call(
        _fox_cumsum_body,
        grid=(b,),
        in_specs=[pl.BlockSpec((1, s, GATE_COLS), lambda bi: (bi, 0, 0)), _resident((1, GATE_COLS)),
                  _resident((F_TERMS * GATE_COLS, HALF))],
        out_specs=[pl.BlockSpec((1, FOX_HEADS, s), lambda bi: (bi, 0, 0)),
                   pl.BlockSpec((1, s, HALF), lambda bi: (bi, 0, 0))],
        out_shape=[jax.ShapeDtypeStruct((b, FOX_HEADS, s), F32),
                   jax.ShapeDtypeStruct((b, s, HALF), BF16)],
        compiler_params=_params("parallel"),
        name="fox_cumsum",
    )(gates, bias, sel)


def _fox_attn_body(ii_ref, jj_ref, qt_ref, k_ref, fk_ref, vt_ref, frow_ref, o_ref,
                   m_ref, acc_ref, *, tile):
    p_idx = pl.program_id(1)
    i = ii_ref[p_idx]
    j = jj_ref[p_idx]
    dim = lax.broadcasted_iota(jnp.int32, (LANES, 1), 0)
    dh = FOX_HEAD_DIM

    @pl.when(j == 0)
    def _():
        m_ref[...] = jnp.full_like(m_ref, -jnp.inf)
        acc_ref[...] = jnp.zeros_like(acc_ref)

    def sweep(diagonal):
        if diagonal:
            query = lax.broadcasted_iota(jnp.int32, (tile, 2 * tile), 1)
            query = jnp.where(query >= tile, query - tile, query)
            keep = lax.broadcasted_iota(jnp.int32, (tile, 2 * tile), 0) <= query
        ones = jnp.ones((ONES_ROWS, tile), BF16)

        def scores(pair):
            rows = slice(pair * LANES, (pair + 1) * LANES)
            q_t = qt_ref[0, rows, :]
            k_aug = jnp.concatenate([k_ref[0, :, rows], fk_ref[0, :, rows]], axis=1)
            q_aug = []
            for c in range(2):
                ones_rows = jnp.logical_and(dim >= F_TERMS * c, dim < F_TERMS * (c + 1))
                q_aug.append(jnp.concatenate(
                    [_keep_half_rows(q_t, c),
                     jnp.broadcast_to(jnp.where(ones_rows, 1.0, 0.0), q_t.shape).astype(BF16)], axis=0))
            return jnp.dot(k_aug, jnp.concatenate(q_aug, axis=1), preferred_element_type=F32)

        def weights(pair, s):
            s = s + jnp.concatenate([frow_ref[0, 2 * pair:2 * pair + 1, :],
                                     frow_ref[0, 2 * pair + 1:2 * pair + 2, :]], axis=1)
            if diagonal:
                s = jnp.where(keep, s, -jnp.inf)
            m_old = m_ref[pair]
            m_new = jnp.maximum(m_old, jnp.max(s, axis=0, keepdims=True))
            m_ref[pair] = m_new
            return jnp.exp2(m_old - m_new), jnp.exp2(s - m_new).astype(BF16)

        def accumulate(pair, alpha, p):
            for c in range(2):
                hd = 2 * pair + c
                cols = slice(c * tile, (c + 1) * tile)
                v_aug = jnp.concatenate([vt_ref[0, hd * dh:(hd + 1) * dh, :], ones], axis=0)
                acc_ref[hd] = alpha[:, cols] * acc_ref[hd] + jnp.dot(v_aug, p[:, cols], preferred_element_type=F32)

        _staged(FOX_HEADS // 2, scores, weights, accumulate)

    @pl.when(j < i)
    def _():
        sweep(False)

    @pl.when(j == i)
    def _():
        sweep(True)
        for pair in range(FOX_HEADS // 2):
            a0, a1 = acc_ref[2 * pair], acc_ref[2 * pair + 1]
            o_t = jnp.concatenate([a0[:dh] / a0[dh:dh + 1], a1[:dh] / a1[dh:dh + 1]], axis=0)
            o_ref[0, :, pair * LANES:(pair + 1) * LANES] = o_t.T.astype(BF16)


def _fox_attention(q_t, k, fk, v_t, frow):
    b, s, _ = k.shape
    t = min(ATTN_TILE, s)
    ii, jj = _causal_pairs(s // t)
    q_map = lambda bi, p, ii, jj: (bi, ii[p], 0)
    k_map = lambda bi, p, ii, jj: (bi, jj[p], 0)
    qt_map = lambda bi, p, ii, jj: (bi, 0, ii[p])
    kt_map = lambda bi, p, ii, jj: (bi, 0, jj[p])
    grid_spec = pltpu.PrefetchScalarGridSpec(
        num_scalar_prefetch=2,
        grid=(b, int(ii.shape[0])),
        in_specs=[pl.BlockSpec((1, HALF, t), qt_map), pl.BlockSpec((1, t, HALF), k_map),
                  pl.BlockSpec((1, t, HALF), k_map), pl.BlockSpec((1, HALF, t), kt_map),
                  pl.BlockSpec((1, FOX_HEADS, t), qt_map)],
        out_specs=pl.BlockSpec((1, t, HALF), q_map),
        scratch_shapes=[pltpu.VMEM((FOX_HEADS // 2, 1, 2 * t), F32),
                        pltpu.VMEM((FOX_HEADS, FOX_HEAD_DIM + ONES_ROWS, t), F32)],
    )
    return pl.pallas_call(
        functools.partial(_fox_attn_body, tile=t),
        grid_spec=grid_spec,
        out_shape=jax.ShapeDtypeStruct((b, s, HALF), BF16),
        compiler_params=_params("parallel", "arbitrary", flags=ATTN_FLAGS),
        name="fox_attn",
    )(ii, jj, q_t, k, fk, v_t, frow)


def kernel(x, l0_ffn1_norm, l0_ffn1_w_gate, l0_ffn1_w_up, l0_ffn1_w_down, l0_mix_norm, l0_w_in, l0_diff_lambda_q1, l0_diff_lambda_k1, l0_diff_lambda_q2, l0_diff_lambda_k2, l0_diff_subln, l0_s5_a_re, l0_s5_a_im, l0_s5_log_dt, l0_s5_b_re, l0_s5_b_im, l0_s5_c_re, l0_s5_c_im, l0_s5_d, l0_s5_w_glu, l0_s5_b_glu, l0_w_out, l0_ffn2_norm, l0_ffn2_w_gate, l0_ffn2_w_up, l0_ffn2_w_down, l1_ffn1_norm, l1_ffn1_w_gate, l1_ffn1_w_up, l1_ffn1_w_down, l1_mix_norm, l1_w_in, l1_mlstm_conv_w, l1_mlstm_conv_b, l1_mlstm_b_i, l1_mlstm_b_f, l1_mlstm_out_norm, l1_fox_b_f, l1_w_out, l1_ffn2_norm, l1_ffn2_w_gate, l1_ffn2_w_up, l1_ffn2_w_down, final_norm):
    b, s, d = x.shape
    n = b * s
    seq = lambda a: a.reshape(b, s, a.shape[-1])
    flat = lambda a: a.reshape(n, a.shape[-1])

    x = _ffn(x.reshape(n, d), l0_ffn1_norm, l0_ffn1_w_gate, l0_ffn1_w_up, l0_ffn1_w_down)
    q_t, k, v_t, u = _proj0(x, l0_mix_norm, l0_w_in, s)
    lam_vecs = jnp.stack([l0_diff_lambda_q1, l0_diff_lambda_k1, l0_diff_lambda_q2, l0_diff_lambda_k2])
    ya = _diff_attention(q_t, seq(k), v_t, lam_vecs, l0_diff_subln)
    yb = _s5(seq(u), l0_s5_a_re, l0_s5_a_im, l0_s5_log_dt, l0_s5_b_re, l0_s5_b_im,
             l0_s5_c_re, l0_s5_c_im, l0_s5_d, l0_s5_w_glu, l0_s5_b_glu)
    x = _ffn(x, l0_ffn2_norm, l0_ffn2_w_gate, l0_ffn2_w_up, l0_ffn2_w_down,
             mix=(flat(ya), flat(yb), l0_w_out[:HALF], l0_w_out[HALF:]))

    x = _ffn(x, l1_ffn1_norm, l1_ffn1_w_gate, l1_ffn1_w_up, l1_ffn1_w_down)
    qk_m, v_m, o_m, qf_t, k_f, vf_t, gates = _proj1(x, l1_mix_norm, l1_w_in, s)
    hm = _mlstm(seq(qk_m), seq(v_m), seq(o_m), seq(gates), l1_mlstm_conv_w, l1_mlstm_conv_b,
                l1_mlstm_b_i, l1_mlstm_b_f, l1_mlstm_out_norm)
    frow, fk = _fox_cumsum(seq(gates), l1_fox_b_f)
    hf = _fox_attention(qf_t, seq(k_f), fk, vf_t, frow)
    x = _ffn(x, l1_ffn2_norm, l1_ffn2_w_gate, l1_ffn2_w_up, l1_ffn2_w_down,
             mix=(flat(hm), flat(hf), l1_w_out[:HALF], l1_w_out[HALF:]), final_norm=final_norm)
    return x.reshape(b, s, d)
```

```python
import functools
import math

import jax
import jax.numpy as jnp
import numpy as np
from jax import lax
from jax.experimental import pallas as pl
from jax.experimental.pallas import tpu as pltpu

F32 = jnp.float32
BF16 = jnp.bfloat16

D_MODEL = 1024
D_FF = 2816
NORM_EPS = 1e-6
ROPE_THETA = 500000.0
DIFF_HEADS = 4
DIFF_HEAD_DIM = 64
ROT_DIM = DIFF_HEAD_DIM // 4
DIFF_LAMBDA_INIT = 0.8 - 0.6 * math.exp(-0.3 * 0)
S5_GROUPS = 32
S5_GROUP = 16
S5_STATE = 64
S5_NSTATE = S5_GROUPS * S5_STATE
MLSTM_HEADS = 4
MLSTM_QK_DIM = 64
MLSTM_V_DIM = 128
MLSTM_CONV = 4
FOX_HEADS = 8
FOX_HEAD_DIM = 64
HALF = 512

LANES = 128
SUBLANES = 8
VMEM_LIMIT_BYTES = 60 * 1024 * 1024

TOKEN_TILE = 1024
FFN_TILE = 512
FF_CHUNK = 256
ATTN_TILE = 512
S5_TILE = 512
S5_CHUNK = 16
MLSTM_CHUNK = 256
GATE_COLS = LANES

NT_DIMS = (((1,), (1,)), ((), ()))
LOG2_E = math.log2(math.e)
ONES_ROWS = 16


def _params(*semantics):
    return pltpu.CompilerParams(dimension_semantics=semantics, vmem_limit_bytes=VMEM_LIMIT_BYTES)


def _resident(shape):
    return pl.BlockSpec(shape, lambda *_: (0,) * len(shape), pipeline_mode=pl.Buffered(1))


def _rms(x, g):
    return x * lax.rsqrt(jnp.mean(x * x, axis=-1, keepdims=True) + NORM_EPS) * g


def _log_sigmoid(x):
    return jnp.minimum(x, 0.0) - jnp.log1p(jnp.exp(-jnp.abs(x)))


def _cumsum_rows(x):
    n = x.shape[0]
    row = lax.broadcasted_iota(jnp.int32, x.shape, 0)
    k = 1
    while k < n:
        x = x + jnp.where(row >= k, pltpu.roll(x, k, 0), 0.0)
        k *= 2
    return x


def _keep_half_rows(x, half):
    rows = x.shape[0] // 2
    zeros = jnp.zeros((rows,) + x.shape[1:], x.dtype)
    return jnp.concatenate([x[:rows], zeros] if half == 0 else [zeros, x[rows:]], axis=0)


def _staged(n, scores, weights, accumulate):
    s, w = {}, {}
    for step in range(n + 2):
        if step < n:
            s[step] = scores(step)
        if step >= 2:
            accumulate(step - 2, *w.pop(step - 2))
        if 1 <= step <= n:
            w[step - 1] = weights(step - 1, s.pop(step - 1))


def _store_key_blocks(ref, value):
    width = ref.shape[3]
    for blk in range(ref.shape[1]):
        ref[0, blk] = value[:, blk * width:(blk + 1) * width]


def _ffn_body(*refs, has_mix, has_final):
    refs = list(refs)
    x_ref = refs.pop(0)
    if has_mix:
        a_ref, b_ref, wa_ref, wb_ref = refs[:4]
        refs = refs[4:]
    g_ref, wg_ref, wu_ref, wd_ref = refs[:4]
    refs = refs[4:]
    if has_final:
        gf_ref = refs.pop(0)
    o_ref = refs.pop(0)

    x = x_ref[...]
    if has_mix:
        x = x + jnp.dot(a_ref[...], wa_ref[...], preferred_element_type=F32)
        x = x + jnp.dot(b_ref[...], wb_ref[...], preferred_element_type=F32)
    h = _rms(x, g_ref[...]).astype(BF16)
    acc = jnp.zeros_like(x)
    for c in range(D_FF // FF_CHUNK):
        sl = slice(c * FF_CHUNK, (c + 1) * FF_CHUNK)
        gate = jnp.dot(h, wg_ref[:, sl].astype(BF16), preferred_element_type=F32)
        up = jnp.dot(h, wu_ref[:, sl].astype(BF16), preferred_element_type=F32)
        act = (gate * jax.nn.sigmoid(gate) * up).astype(BF16)
        acc = acc + jnp.dot(act, wd_ref[sl, :].astype(BF16), preferred_element_type=F32)
    y = x + 0.5 * acc
    if has_final:
        y = _rms(y, gf_ref[...])
    o_ref[...] = y


def _ffn(x, norm, wg, wu, wd, mix=None, final_norm=None):
    n, d = x.shape
    tm = min(FFN_TILE, n)
    row = lambda i: (i, 0)
    args = [x]
    specs = [pl.BlockSpec((tm, d), row)]
    if mix is not None:
        a, b, wa, wb = mix
        args += [a, b, wa.astype(BF16), wb.astype(BF16)]
        specs += [pl.BlockSpec((tm, HALF), row), pl.BlockSpec((tm, HALF), row),
                  _resident((HALF, d)), _resident((HALF, d))]
    args += [norm.reshape(1, d), wg, wu, wd]
    specs += [_resident((1, d)), _resident((d, D_FF)), _resident((d, D_FF)), _resident((D_FF, d))]
    if final_norm is not None:
        args.append(final_norm.reshape(1, d))
        specs.append(_resident((1, d)))
    body = functools.partial(_ffn_body, has_mix=mix is not None, has_final=final_norm is not None)
    return pl.pallas_call(
        body,
        grid=(n // tm,),
        in_specs=specs,
        out_specs=pl.BlockSpec((tm, d), row),
        out_shape=jax.ShapeDtypeStruct((n, d), F32),
        compiler_params=_params("parallel"),
        name="ffn",
    )(*args)


def _rope_angles(seq):
    pos = np.arange(seq, dtype=np.float32)
    inv = np.float32(ROPE_THETA) ** (-np.arange(0, ROT_DIM, 2, dtype=np.float32) / np.float32(ROT_DIM))
    ang = pos[:, None] * inv[None, :]
    return np.cos(ang), np.sin(ang)


def _rope_lane_tables(seq):
    half = ROT_DIM // 2
    cos, sin = _rope_angles(seq)
    one = np.ones((seq, DIFF_HEAD_DIM - ROT_DIM), np.float32)
    zero8 = np.zeros((seq, half), np.float32)
    zero48 = np.zeros((seq, DIFF_HEAD_DIM - ROT_DIM), np.float32)
    cos_t = np.concatenate([cos, cos, one] * 2, axis=1)
    sa_t = np.concatenate([-sin, zero8, zero48] * 2, axis=1)
    sb_t = np.concatenate([zero8, sin, zero48] * 2, axis=1)
    return cos_t, sa_t, sb_t


def _proj0_body(x_ref, g_ref, wt_ref, w_ref, cos_ref, sa_ref, sb_ref, cosr_ref, sinr_ref,
                qt_ref, k_ref, vt_ref, u_ref):
    h = _rms(x_ref[...], g_ref[...]).astype(BF16)
    pt = lax.dot_general(wt_ref[...], h, NT_DIMS, preferred_element_type=F32)
    p = jnp.dot(h, w_ref[...], preferred_element_type=F32)
    half = ROT_DIM // 2
    scale = DIFF_HEAD_DIM ** -0.5 * LOG2_E

    cos_r, sin_r = cosr_ref[...], sinr_ref[...]
    for comp in range(2 * DIFF_HEADS):
        r0 = comp * DIFF_HEAD_DIM
        x1, x2 = pt[r0:r0 + half], pt[r0 + half:r0 + ROT_DIM]
        qt_ref[0, r0:r0 + half, :] = ((x1 * cos_r - x2 * sin_r) * scale).astype(BF16)
        qt_ref[0, r0 + half:r0 + ROT_DIM, :] = ((x2 * cos_r + x1 * sin_r) * scale).astype(BF16)
        qt_ref[0, r0 + ROT_DIM:r0 + DIFF_HEAD_DIM, :] = (pt[r0 + ROT_DIM:r0 + DIFF_HEAD_DIM] * scale).astype(BF16)
    _store_key_blocks(vt_ref, pt[HALF:2 * HALF].astype(BF16))

    cos_t, sa_t, sb_t = cos_ref[...], sa_ref[...], sb_ref[...]
    for hd in range(DIFF_HEADS):
        sl = slice(hd * LANES, (hd + 1) * LANES)
        t = p[:, sl]
        k_ref[:, sl] = (t * cos_t + pltpu.roll(t, LANES - half, 1) * sa_t
                        + pltpu.roll(t, half, 1) * sb_t).astype(BF16)
    u_ref[...] = p[:, HALF:2 * HALF]


def _proj0(x, norm, w_in, seq):
    n, d = x.shape
    b = n // seq
    tm = min(TOKEN_TILE, seq)
    tps = seq // tm
    row = lambda i: (i, 0)
    pos = lambda i: (i % tps, 0)
    pos_t = lambda i: (0, i % tps)
    seq_t = lambda i: (i // tps, 0, i % tps)
    ta = min(ATTN_TILE, seq)
    key_blk = lambda i: (i // tps, i % tps, 0, 0)
    cos, sin = _rope_angles(seq)
    w_t = jnp.concatenate([w_in[:, :HALF], w_in[:, 2 * HALF:3 * HALF]], axis=1).T.astype(BF16)
    w_n = jnp.concatenate([w_in[:, HALF:2 * HALF], w_in[:, 3 * HALF:]], axis=1).astype(BF16)
    t_shape = jax.ShapeDtypeStruct((b, HALF, seq), BF16)
    return pl.pallas_call(
        _proj0_body,
        grid=(n // tm,),
        in_specs=[pl.BlockSpec((tm, d), row), _resident((1, d)), _resident((2 * HALF, d)),
                  _resident((d, 2 * HALF)),
                  pl.BlockSpec((tm, LANES), pos), pl.BlockSpec((tm, LANES), pos),
                  pl.BlockSpec((tm, LANES), pos),
                  pl.BlockSpec((ROT_DIM // 2, tm), pos_t), pl.BlockSpec((ROT_DIM // 2, tm), pos_t)],
        out_specs=[pl.BlockSpec((1, HALF, tm), seq_t), pl.BlockSpec((tm, HALF), row),
                   pl.BlockSpec((1, tm // ta, HALF, ta), key_blk), pl.BlockSpec((tm, HALF), row)],
        out_shape=[t_shape, jax.ShapeDtypeStruct((n, HALF), BF16),
                   jax.ShapeDtypeStruct((b, seq // ta, HALF, ta), BF16),
                   jax.ShapeDtypeStruct((n, HALF), F32)],
        compiler_params=_params("parallel"),
        name="proj0",
    )(x, norm.reshape(1, d), w_t, w_n, *_rope_lane_tables(seq), cos.T, sin.T)


def _diff_attn_body(lam_ref, subln_ref, qt_ref, k_ref, vt_ref, o_ref, m_ref, acc_ref, *, tile):
    i = pl.program_id(1)
    m_ref[...] = jnp.full_like(m_ref, -jnp.inf)
    acc_ref[...] = jnp.zeros_like(acc_ref)
    ones = jnp.ones((ONES_ROWS, tile), BF16)

    def sweep(j, diagonal):
        key0 = pl.multiple_of(j * tile, tile)
        if diagonal:
            query = lax.broadcasted_iota(jnp.int32, (tile, 2 * tile), 1)
            query = jnp.where(query >= tile, query - tile, query)
            keep = lax.broadcasted_iota(jnp.int32, (tile, 2 * tile), 0) <= query

        def scores(hd):
            rows = slice(hd * LANES, (hd + 1) * LANES)
            q_t = qt_ref[0, rows, :]
            q_both = jnp.concatenate([_keep_half_rows(q_t, 0), _keep_half_rows(q_t, 1)], axis=1)
            return jnp.dot(k_ref[0, pl.ds(key0, tile), rows], q_both, preferred_element_type=F32)

        def weights(hd, s):
            if diagonal:
                s = jnp.where(keep, s, -jnp.inf)
            m_old = m_ref[hd]
            m_new = jnp.maximum(m_old, jnp.max(s, axis=0, keepdims=True))
            m_ref[hd] = m_new
            return jnp.exp2(m_old - m_new), jnp.exp2(s - m_new).astype(BF16)

        def accumulate(hd, alpha, p):
            v_aug = jnp.concatenate([vt_ref[0, j, hd * LANES:(hd + 1) * LANES, :], ones], axis=0)
            acc_ref[hd] = alpha * acc_ref[hd] + jnp.dot(v_aug, p, preferred_element_type=F32)

        _staged(DIFF_HEADS, scores, weights, accumulate)

    def off_diagonal(j, carry):
        sweep(j, False)
        return carry

    lax.fori_loop(0, i, off_diagonal, 0)
    sweep(i, True)

    lam_v = lam_ref[...]
    lam = (jnp.exp(jnp.sum(lam_v[0:1] * lam_v[1:2], keepdims=True))
           - jnp.exp(jnp.sum(lam_v[2:3] * lam_v[3:4], keepdims=True)) + DIFF_LAMBDA_INIT)
    for hd in range(DIFF_HEADS):
        a1, a2 = acc_ref[hd, :, :tile], acc_ref[hd, :, tile:]
        o_t = (a1[:LANES] / a1[LANES:LANES + 1] - lam * (a2[:LANES] / a2[LANES:LANES + 1]))
        o_t = o_t * lax.rsqrt(jnp.mean(o_t * o_t, axis=0, keepdims=True) + NORM_EPS)
        o_t = o_t * subln_ref[...] * (1.0 - DIFF_LAMBDA_INIT)
        o_ref[0, :, hd * LANES:(hd + 1) * LANES] = o_t.T.astype(BF16)


def _diff_attention(q_t, k, v_t, lam_vecs, subln):
    b, s, _ = k.shape
    n_blk, _, t = v_t.shape[1:]
    const = lambda bi, i: (0, 0)
    return pl.pallas_call(
        functools.partial(_diff_attn_body, tile=t),
        grid=(b, n_blk),
        in_specs=[pl.BlockSpec((4, DIFF_HEAD_DIM), const), pl.BlockSpec((LANES, 1), const),
                  pl.BlockSpec((1, HALF, t), lambda bi, i: (bi, 0, i)),
                  pl.BlockSpec((1, s, HALF), lambda bi, i: (bi, 0, 0)),
                  pl.BlockSpec((1, n_blk, HALF, t), lambda bi, i: (bi, 0, 0, 0))],
        out_specs=pl.BlockSpec((1, t, HALF), lambda bi, i: (bi, i, 0)),
        out_shape=jax.ShapeDtypeStruct((b, s, HALF), BF16),
        scratch_shapes=[pltpu.VMEM((DIFF_HEADS, 1, 2 * t), F32),
                        pltpu.VMEM((DIFF_HEADS, LANES + ONES_ROWS, 2 * t), F32)],
        compiler_params=_params("parallel", "arbitrary"),
        name="diff_attn",
    )(lam_vecs, subln.reshape(LANES, 1), q_t, k, v_t)


def _s5_discretize_body(lr_ref, li_ref, ldt_ref, br_ref, bi_ref, k_ref, pr_ref, pi_ref, bbr_ref, bbi_ref):
    lr, li = lr_ref[...], li_ref[...]
    dt = jnp.exp(ldt_ref[...])
    mag = jnp.exp(lr * dt)
    ar, ai = mag * jnp.cos(li * dt), mag * jnp.sin(li * dt)
    den = lr * lr + li * li
    gr = ((ar - 1.0) * lr + ai * li) / den
    gi = (ai * lr - (ar - 1.0) * li) / den
    br, bi = br_ref[...], bi_ref[...]
    bbr_ref[...] = gr * br - gi * bi
    bbi_ref[...] = gr * bi + gi * br
    steps = k_ref[...]
    mag_k = jnp.exp(lr * dt * steps)
    pr_ref[...] = mag_k * jnp.cos(li * dt * steps)
    pi_ref[...] = mag_k * jnp.sin(li * dt * steps)


def _s5_discretize(a_re, a_im, log_dt, b_re, b_im):
    col = lambda a: a.reshape(S5_NSTATE, 1)
    ldt = jnp.broadcast_to(log_dt[:, None], (S5_GROUPS, S5_STATE))
    steps = jnp.arange(1, S5_CHUNK + 1, dtype=F32).reshape(1, S5_CHUNK)
    pshape = jax.ShapeDtypeStruct((S5_NSTATE, S5_CHUNK), F32)
    bshape = jax.ShapeDtypeStruct((S5_NSTATE, S5_GROUP), F32)
    return pl.pallas_call(
        _s5_discretize_body,
        out_shape=[pshape, pshape, bshape, bshape],
        name="s5_discretize",
    )(col(a_re), col(a_im), col(ldt), b_re.reshape(S5_NSTATE, S5_GROUP), b_im.reshape(S5_NSTATE, S5_GROUP), steps)


def _s5_body(u_ref, pr_ref, pi_ref, bbr_ref, bbi_ref, cr_ref, ci_ref, d_ref, wglu_ref, bglu_ref,
             o_ref, us_ref, ys_ref, xr_ref, xi_ref, hin_ref, carry_ref, *, tile):
    n_chunks = tile // S5_CHUNK
    slabs = HALF // LANES
    gw = S5_NSTATE // slabs

    @pl.when(pl.program_id(1) == 0)
    def _():
        carry_ref[...] = jnp.zeros_like(carry_ref)

    u = u_ref[0]
    for g in range(slabs):
        us_ref[g] = u[:, g * LANES:(g + 1) * LANES]
    u_perm = jnp.concatenate(
        [jnp.concatenate([us_ref[g, pl.ds(st, n_chunks, stride=S5_CHUNK), :] for g in range(slabs)], axis=1)
         for st in range(S5_CHUNK)], axis=0).astype(BF16)

    for g in range(slabs):
        ug = u_perm[:, g * LANES:(g + 1) * LANES]
        xr_ref[:, g * gw:(g + 1) * gw] = jnp.dot(ug, bbr_ref[g], preferred_element_type=F32)
        xi_ref[:, g * gw:(g + 1) * gw] = jnp.dot(ug, bbi_ref[g], preferred_element_type=F32)

    ar, ai = pr_ref[0:1, :], pi_ref[0:1, :]
    hr, hi = xr_ref[0:n_chunks, :], xi_ref[0:n_chunks, :]
    for st in range(1, S5_CHUNK):
        rows = slice(st * n_chunks, (st + 1) * n_chunks)
        hr, hi = ar * hr - ai * hi + xr_ref[rows, :], ar * hi + ai * hr + xi_ref[rows, :]
        xr_ref[rows, :] = hr
        xi_ref[rows, :] = hi

    last = (S5_CHUNK - 1) * n_chunks
    ac_r, ac_i = pr_ref[S5_CHUNK - 1:S5_CHUNK, :], pi_ref[S5_CHUNK - 1:S5_CHUNK, :]

    def carry_step(c, carry):
        cr, ci = carry
        hin_ref[0, pl.ds(c, 1), :] = cr
        hin_ref[1, pl.ds(c, 1), :] = ci
        zr, zi = xr_ref[pl.ds(last + c, 1), :], xi_ref[pl.ds(last + c, 1), :]
        return ac_r * cr - ac_i * ci + zr, ac_r * ci + ac_i * cr + zi

    cr, ci = lax.fori_loop(0, n_chunks, carry_step, (carry_ref[0:1, :], carry_ref[1:2, :]), unroll=4)
    carry_ref[0:1, :] = cr
    carry_ref[1:2, :] = ci

    hin_r, hin_i = hin_ref[0], hin_ref[1]
    for st in range(S5_CHUNK):
        rows = slice(st * n_chunks, (st + 1) * n_chunks)
        p_r, p_i = pr_ref[st:st + 1, :], pi_ref[st:st + 1, :]
        xr_ref[rows, :] = xr_ref[rows, :] + (p_r * hin_r - p_i * hin_i)
        xi_ref[rows, :] = xi_ref[rows, :] + (p_r * hin_i + p_i * hin_r)
    for g in range(slabs):
        cols = slice(g * gw, (g + 1) * gw)
        yg = (jnp.dot(xr_ref[:, cols].astype(BF16), cr_ref[g], preferred_element_type=F32)
              - jnp.dot(xi_ref[:, cols].astype(BF16), ci_ref[g], preferred_element_type=F32))
        for st in range(S5_CHUNK):
            ys_ref[g, pl.ds(st, n_chunks, stride=S5_CHUNK), :] = yg[st * n_chunks:(st + 1) * n_chunks]

    y = jnp.concatenate([ys_ref[g] for g in range(slabs)], axis=1) + d_ref[...] * u
    z = jax.nn.gelu(y, approximate=True)
    gate = jnp.dot(z.astype(BF16), wglu_ref[...], preferred_element_type=F32) + bglu_ref[...]
    o_ref[0] = (z * jax.nn.sigmoid(gate)).astype(BF16)


def _s5(u, a_re, a_im, log_dt, b_re, b_im, c_re, c_im, d_skip, w_glu, b_glu):
    b, s, w = u.shape
    t = min(S5_TILE, s)
    slabs = w // LANES
    per = S5_GROUPS // slabs
    gw = S5_NSTATE // slabs
    pw_r, pw_i, bbr, bbi = _s5_discretize(a_re, a_im, log_dt, b_re, b_im)
    eye = jnp.eye(per, dtype=F32)

    def block_in(m):
        m = m.reshape(slabs, per, S5_STATE, S5_GROUP)
        return jnp.einsum('sgpc,gh->sgchp', m, eye).reshape(slabs, LANES, gw).astype(BF16)

    def block_out(m):
        m = m.reshape(slabs, per, S5_GROUP, S5_STATE)
        return jnp.einsum('sgcp,gh->sgphc', m, eye).reshape(slabs, gw, LANES).astype(BF16)

    seq_blk = lambda bi, ti: (bi, ti, 0)
    n_chunks = t // S5_CHUNK
    return pl.pallas_call(
        functools.partial(_s5_body, tile=t),
        grid=(b, s // t),
        in_specs=[pl.BlockSpec((1, t, w), seq_blk),
                  _resident((S5_CHUNK, S5_NSTATE)), _resident((S5_CHUNK, S5_NSTATE)),
                  _resident((slabs, LANES, gw)), _resident((slabs, LANES, gw)),
                  _resident((slabs, gw, LANES)), _resident((slabs, gw, LANES)),
                  _resident((1, w)), _resident((w, w)), _resident((1, w))],
        out_specs=pl.BlockSpec((1, t, w), seq_blk),
        out_shape=jax.ShapeDtypeStruct((b, s, w), BF16),
        scratch_shapes=[pltpu.VMEM((slabs, t, LANES), F32), pltpu.VMEM((slabs, t, LANES), F32),
                        pltpu.VMEM((t, S5_NSTATE), F32), pltpu.VMEM((t, S5_NSTATE), F32),
                        pltpu.VMEM((2, n_chunks, S5_NSTATE), F32), pltpu.VMEM((2, S5_NSTATE), F32)],
        compiler_params=_params("parallel", "arbitrary"),
        name="s5",
    )(u, pw_r.T, pw_i.T, block_in(bbr), block_in(bbi),
      block_out(c_re), block_out(c_im), d_skip.reshape(1, w), w_glu.astype(BF16), b_glu.reshape(1, w))


def _proj1_body(x_ref, g_ref, wt_ref, w_ref, qk_ref, vm_ref, om_ref, qft_ref, kf_ref, vft_ref, gt_ref):
    h = _rms(x_ref[...], g_ref[...]).astype(BF16)
    pt = lax.dot_general(wt_ref[...], h, NT_DIMS, preferred_element_type=F32)
    p = jnp.dot(h, w_ref[...], preferred_element_type=F32)
    qft_ref[0] = (pt[:HALF] * (FOX_HEAD_DIM ** -0.5 * LOG2_E)).astype(BF16)
    _store_key_blocks(vft_ref, pt[HALF:].astype(BF16))
    qk_ref[...] = p[:, 0:HALF]
    vm_ref[...] = p[:, HALF:2 * HALF].astype(BF16)
    om_ref[...] = p[:, 2 * HALF:3 * HALF]
    kf_ref[...] = p[:, 3 * HALF:4 * HALF].astype(BF16)
    gt_ref[...] = p[:, 4 * HALF:4 * HALF + GATE_COLS]


def _proj1(x, norm, w_in, seq):
    n, d = x.shape
    b = n // seq
    tm = min(TOKEN_TILE, seq)
    tps = seq // tm
    row = lambda i: (i, 0)
    seq_t = lambda i: (i // tps, 0, i % tps)
    ta = min(ATTN_TILE, seq)
    key_blk = lambda i: (i // tps, i % tps, 0, 0)
    o = 2 * HALF
    g = 2 * MLSTM_HEADS
    om = o + g
    w_n = jnp.concatenate(
        [w_in[:, :o], w_in[:, om:om + HALF], w_in[:, om + 2 * HALF:om + 3 * HALF],
         w_in[:, o:om], w_in[:, om + 4 * HALF:],
         jnp.zeros((d, GATE_COLS - g - FOX_HEADS), w_in.dtype)], axis=1).astype(BF16)
    w_t = jnp.concatenate([w_in[:, om + HALF:om + 2 * HALF], w_in[:, om + 3 * HALF:om + 4 * HALF]],
                          axis=1).T.astype(BF16)
    f32_half = jax.ShapeDtypeStruct((n, HALF), F32)
    bf_half = jax.ShapeDtypeStruct((n, HALF), BF16)
    t_shape = jax.ShapeDtypeStruct((b, HALF, seq), BF16)
    half_spec = pl.BlockSpec((tm, HALF), row)
    t_spec = pl.BlockSpec((1, HALF, tm), seq_t)
    return pl.pallas_call(
        _proj1_body,
        grid=(n // tm,),
        in_specs=[pl.BlockSpec((tm, d), row), _resident((1, d)), _resident((2 * HALF, d)),
                  _resident((d, 4 * HALF + GATE_COLS))],
        out_specs=[half_spec, half_spec, half_spec, t_spec, half_spec,
                   pl.BlockSpec((1, tm // ta, HALF, ta), key_blk), pl.BlockSpec((tm, GATE_COLS), row)],
        out_shape=[f32_half, bf_half, f32_half, t_shape, bf_half,
                   jax.ShapeDtypeStruct((b, seq // ta, HALF, ta), BF16),
                   jax.ShapeDtypeStruct((n, GATE_COLS), F32)],
        compiler_params=_params("parallel"),
        name="proj1",
    )(x, norm.reshape(1, d), w_t, w_n)


def _mlstm_body(qk_ref, v_ref, og_ref, gt_ref, cw_ref, cb_ref, gb_ref, on_ref, out_ref,
                xbuf_ref, cn_ref, m_ref, *, chunk):
    pad = SUBLANES
    dk, dv = MLSTM_QK_DIM, MLSTM_V_DIM

    @pl.when(pl.program_id(1) == 0)
    def _():
        xbuf_ref[0:pad, :] = jnp.zeros((pad, HALF), F32)
        cn_ref[...] = jnp.zeros_like(cn_ref)
        m_ref[...] = jnp.zeros_like(m_ref)

    x = qk_ref[0]
    xbuf_ref[pad:pad + chunk, :] = x
    a = jnp.zeros_like(x) + cb_ref[...]
    for tap in range(MLSTM_CONV):
        a = a + cw_ref[tap:tap + 1, :] * xbuf_ref[pl.ds(pad - (MLSTM_CONV - 1) + tap, chunk), :]
    xbuf_ref[0:pad, :] = x[chunk - pad:chunk, :]
    a = a * jax.nn.sigmoid(a)
    q = a[:, :MLSTM_HEADS * dk] * dk ** -0.5
    k = a[:, MLSTM_HEADS * dk:]
    k_t = k.T

    gi = gt_ref[0] + gb_ref[...]
    bcum = _cumsum_rows(_log_sigmoid(gi))
    gi_t = gi.T
    bcum_t = bcum.T
    tri = (lax.broadcasted_iota(jnp.int32, (chunk, chunk), 1)
           <= lax.broadcasted_iota(jnp.int32, (chunk, chunk), 0))
    lane = lax.broadcasted_iota(jnp.int32, (1, LANES), 1)
    v_all = v_ref[0]
    o_gate = og_ref[0]
    ones = jnp.ones((chunk, dv), BF16)

    states = [cn_ref[pair] for pair in range(MLSTM_HEADS // 2)]

    def products(hd):
        pair, odd = hd // 2, hd % 2
        sel = (lane >= dk) if odd else (lane < dk)
        q2 = jnp.where(sel, q[:, pair * LANES:(pair + 1) * LANES], 0.0).astype(BF16)
        k2 = k[:, pair * LANES:(pair + 1) * LANES].astype(BF16)
        qk = lax.dot_general(q2, k2, NT_DIMS, preferred_element_type=F32)
        return qk, jnp.dot(q2, states[pair].astype(BF16), preferred_element_type=F32)

    def gate_weights(hd, prods):
        qk, q_state = prods
        f_lane = MLSTM_HEADS + hd
        b_col = bcum[:, f_lane:f_lane + 1]
        b_row = bcum_t[f_lane:f_lane + 1, :]
        i_row = gi_t[hd:hd + 1, :]
        m_prev = m_ref[hd]
        dm = jnp.where(tri, b_col - b_row + i_row, -jnp.inf)
        g = b_col + m_prev
        mt = jnp.maximum(g, jnp.max(dm, axis=-1, keepdims=True))
        s = (qk * jnp.exp(dm - mt)).astype(BF16)
        w_inter = jnp.exp(g - mt)

        b_last = bcum[chunk - 1:chunk, f_lane:f_lane + 1]
        dec = b_last - b_row + i_row
        m_new = jnp.maximum(b_last + m_prev, jnp.max(dec, axis=-1, keepdims=True))
        k_w = (k_t[hd * dk:(hd + 1) * dk, :] * jnp.exp(dec - m_new)).astype(BF16)
        carry_w = jnp.exp(b_last + m_prev - m_new)
        m_ref[hd] = m_new
        return s, k_w, q_state, w_inter, mt, carry_w

    def outputs(hd, s, k_w, q_state, w_inter, mt, carry_w):
        pair, odd = hd // 2, hd % 2
        v_aug = jnp.concatenate([v_all[:, hd * dv:(hd + 1) * dv], ones], axis=1)
        s_v = jnp.dot(s, v_aug, preferred_element_type=F32)
        num = w_inter * q_state[:, :dv] + s_v[:, :dv]
        den = w_inter * q_state[:, dv:] + s_v[:, dv:]
        h_out = num / jnp.maximum(jnp.abs(den), jnp.exp(-mt))
        rows = slice(odd * dk, (odd + 1) * dk)
        cn_ref[pair, rows, :] = (carry_w * states[pair][rows, :]
                                 + jnp.dot(k_w, v_aug, preferred_element_type=F32))
        cols = slice(hd * dv, (hd + 1) * dv)
        h_norm = _rms(h_out, on_ref[hd:hd + 1, :])
        out_ref[0, :, cols] = (jax.nn.sigmoid(o_gate[:, cols]) * h_norm).astype(BF16)

    _staged(MLSTM_HEADS, products, gate_weights, outputs)


def _mlstm(qk, v, o_gate, gates, conv_w, conv_b, b_i, b_f, out_norm):
    b, s, _ = qk.shape
    c = min(MLSTM_CHUNK, s)
    gate_bias = jnp.concatenate([b_i, b_f, jnp.zeros((GATE_COLS - 2 * MLSTM_HEADS,), F32)]).reshape(1, GATE_COLS)
    seq_blk = lambda bi, ci: (bi, ci, 0)
    half_spec = pl.BlockSpec((1, c, HALF), seq_blk)
    return pl.pallas_call(
        functools.partial(_mlstm_body, chunk=c),
        grid=(b, s // c),
        in_specs=[half_spec, half_spec, half_spec, pl.BlockSpec((1, c, GATE_COLS), seq_blk),
                  _resident((MLSTM_CONV, HALF)), _resident((1, HALF)), _resident((1, GATE_COLS)),
                  _resident((MLSTM_HEADS, MLSTM_V_DIM))],
        out_specs=half_spec,
        out_shape=jax.ShapeDtypeStruct((b, s, HALF), BF16),
        scratch_shapes=[pltpu.VMEM((c + SUBLANES, HALF), F32),
                        pltpu.VMEM((MLSTM_HEADS // 2, 2 * MLSTM_QK_DIM, 2 * MLSTM_V_DIM), F32),
                        pltpu.VMEM((MLSTM_HEADS, 1, 1), F32)],
        compiler_params=_params("parallel", "arbitrary"),
        name="mlstm",
    )(qk, v, o_gate, gates, conv_w, conv_b.reshape(1, HALF), gate_bias, out_norm)


FOX_GATE_LANE = 2 * MLSTM_HEADS
F_TERMS = 3


def _fox_cumsum_body(gt_ref, gb_ref, sel_ref, frow_ref, fk_ref):
    f = _cumsum_rows(_log_sigmoid(gt_ref[0] + gb_ref[...])) * LOG2_E
    hi = f.astype(BF16)
    rest = f - hi.astype(F32)
    mid = rest.astype(BF16)
    lo = (rest - mid.astype(F32)).astype(BF16)
    terms = jnp.concatenate([hi, mid, lo], axis=1)
    fk_ref[0] = jnp.dot(terms, sel_ref[...], preferred_element_type=F32).astype(BF16)
    frow_ref[0] = f.T[FOX_GATE_LANE:FOX_GATE_LANE + FOX_HEADS, :]


def _fox_cumsum(gates, fox_b_f):
    b, s, _ = gates.shape
    bias = jnp.concatenate([jnp.zeros((FOX_GATE_LANE,), F32), fox_b_f,
                            jnp.zeros((GATE_COLS - FOX_GATE_LANE - FOX_HEADS,), F32)]).reshape(1, GATE_COLS)
    src, dst = [], []
    for head in range(FOX_HEADS):
        for term in range(F_TERMS):
            src.append(term * GATE_COLS + FOX_GATE_LANE + head)
            dst.append((head // 2) * LANES + F_TERMS * (head % 2) + term)
    sel = jnp.zeros((F_TERMS * GATE_COLS, HALF), BF16).at[jnp.asarray(src), jnp.asarray(dst)].set(-1.0)
    return pl.pallas_call(
        _fox_cumsum_body,
        grid=(b,),
        in_specs=[pl.BlockSpec((1, s, GATE_COLS), lambda bi: (bi, 0, 0)), _resident((1, GATE_COLS)),
                  _resident((F_TERMS * GATE_COLS, HALF))],
        out_specs=[pl.BlockSpec((1, FOX_HEADS, s), lambda bi: (bi, 0, 0)),
                   pl.BlockSpec((1, s, HALF), lambda bi: (bi, 0, 0))],
        out_shape=[jax.ShapeDtypeStruct((b, FOX_HEADS, s), F32),
                   jax.ShapeDtypeStruct((b, s, HALF), BF16)],
        compiler_params=_params("parallel"),
        name="fox_cumsum",
    )(gates, bias, sel)


def _fox_attn_body(qt_ref, k_ref, fk_ref, vt_ref, frow_ref, o_ref, m_ref, acc_ref, *, tile):
    i = pl.program_id(1)
    dim = lax.broadcasted_iota(jnp.int32, (LANES, 1), 0)
    dh = FOX_HEAD_DIM
    m_ref[...] = jnp.full_like(m_ref, -jnp.inf)
    acc_ref[...] = jnp.zeros_like(acc_ref)
    ones = jnp.ones((ONES_ROWS, tile), BF16)

    def sweep(j, diagonal):
        key0 = pl.multiple_of(j * tile, tile)
        if diagonal:
            query = lax.broadcasted_iota(jnp.int32, (tile, 2 * tile), 1)
            query = jnp.where(query >= tile, query - tile, query)
            keep = lax.broadcasted_iota(jnp.int32, (tile, 2 * tile), 0) <= query

        def scores(pair):
            rows = slice(pair * LANES, (pair + 1) * LANES)
            q_t = qt_ref[0, rows, :]
            k_aug = jnp.concatenate([k_ref[0, pl.ds(key0, tile), rows],
                                     fk_ref[0, pl.ds(key0, tile), rows]], axis=1)
            q_aug = []
            for c in range(2):
                ones_rows = jnp.logical_and(dim >= F_TERMS * c, dim < F_TERMS * (c + 1))
                q_aug.append(jnp.concatenate(
                    [_keep_half_rows(q_t, c),
                     jnp.broadcast_to(jnp.where(ones_rows, 1.0, 0.0), q_t.shape).astype(BF16)], axis=0))
            return jnp.dot(k_aug, jnp.concatenate(q_aug, axis=1), preferred_element_type=F32)

        def weights(pair, s):
            s = s + jnp.concatenate([frow_ref[0, 2 * pair:2 * pair + 1, :],
                                     frow_ref[0, 2 * pair + 1:2 * pair + 2, :]], axis=1)
            if diagonal:
                s = jnp.where(keep, s, -jnp.inf)
            m_old = m_ref[pair]
            m_new = jnp.maximum(m_old, jnp.max(s, axis=0, keepdims=True))
            m_ref[pair] = m_new
            return jnp.exp2(m_old - m_new), jnp.exp2(s - m_new).astype(BF16)

        def accumulate(pair, alpha, p):
            for c in range(2):
                hd = 2 * pair + c
                cols = slice(c * tile, (c + 1) * tile)
                v_aug = jnp.concatenate([vt_ref[0, j, hd * dh:(hd + 1) * dh, :], ones], axis=0)
                acc_ref[hd] = alpha[:, cols] * acc_ref[hd] + jnp.dot(v_aug, p[:, cols], preferred_element_type=F32)

        _staged(FOX_HEADS // 2, scores, weights, accumulate)

    def off_diagonal(j, carry):
        sweep(j, False)
        return carry

    lax.fori_loop(0, i, off_diagonal, 0)
    sweep(i, True)

    for pair in range(FOX_HEADS // 2):
        a0, a1 = acc_ref[2 * pair], acc_ref[2 * pair + 1]
        o_t = jnp.concatenate([a0[:dh] / a0[dh:dh + 1], a1[:dh] / a1[dh:dh + 1]], axis=0)
        o_ref[0, :, pair * LANES:(pair + 1) * LANES] = o_t.T.astype(BF16)


def _fox_attention(q_t, k, fk, v_t, frow):
    b, s, _ = k.shape
    n_blk, _, t = v_t.shape[1:]
    whole = lambda bi, i: (bi, 0, 0)
    return pl.pallas_call(
        functools.partial(_fox_attn_body, tile=t),
        grid=(b, n_blk),
        in_specs=[pl.BlockSpec((1, HALF, t), lambda bi, i: (bi, 0, i)),
                  pl.BlockSpec((1, s, HALF), whole), pl.BlockSpec((1, s, HALF), whole),
                  pl.BlockSpec((1, n_blk, HALF, t), lambda bi, i: (bi, 0, 0, 0)),
                  pl.BlockSpec((1, FOX_HEADS, t), lambda bi, i: (bi, 0, i))],
        out_specs=pl.BlockSpec((1, t, HALF), lambda bi, i: (bi, i, 0)),
        out_shape=jax.ShapeDtypeStruct((b, s, HALF), BF16),
        scratch_shapes=[pltpu.VMEM((FOX_HEADS // 2, 1, 2 * t), F32),
                        pltpu.VMEM((FOX_HEADS, FOX_HEAD_DIM + ONES_ROWS, t), F32)],
        compiler_params=_params("parallel", "arbitrary"),
        name="fox_attn",
    )(q_t, k, fk, v_t, frow)


def kernel(x, l0_ffn1_norm, l0_ffn1_w_gate, l0_ffn1_w_up, l0_ffn1_w_down, l0_mix_norm, l0_w_in, l0_diff_lambda_q1, l0_diff_lambda_k1, l0_diff_lambda_q2, l0_diff_lambda_k2, l0_diff_subln, l0_s5_a_re, l0_s5_a_im, l0_s5_log_dt, l0_s5_b_re, l0_s5_b_im, l0_s5_c_re, l0_s5_c_im, l0_s5_d, l0_s5_w_glu, l0_s5_b_glu, l0_w_out, l0_ffn2_norm, l0_ffn2_w_gate, l0_ffn2_w_up, l0_ffn2_w_down, l1_ffn1_norm, l1_ffn1_w_gate, l1_ffn1_w_up, l1_ffn1_w_down, l1_mix_norm, l1_w_in, l1_mlstm_conv_w, l1_mlstm_conv_b, l1_mlstm_b_i, l1_mlstm_b_f, l1_mlstm_out_norm, l1_fox_b_f, l1_w_out, l1_ffn2_norm, l1_ffn2_w_gate, l1_ffn2_w_up, l1_ffn2_w_down, final_norm):
    b, s, d = x.shape
    n = b * s
    seq = lambda a: a.reshape(b, s, a.shape[-1])
    flat = lambda a: a.reshape(n, a.shape[-1])

    x = _ffn(x.reshape(n, d), l0_ffn1_norm, l0_ffn1_w_gate, l0_ffn1_w_up, l0_ffn1_w_down)
    q_t, k, v_t, u = _proj0(x, l0_mix_norm, l0_w_in, s)
    lam_vecs = jnp.stack([l0_diff_lambda_q1, l0_diff_lambda_k1, l0_diff_lambda_q2, l0_diff_lambda_k2])
    ya = _diff_attention(q_t, seq(k), v_t, lam_vecs, l0_diff_subln)
    yb = _s5(seq(u), l0_s5_a_re, l0_s5_a_im, l0_s5_log_dt, l0_s5_b_re, l0_s5_b_im,
             l0_s5_c_re, l0_s5_c_im, l0_s5_d, l0_s5_w_glu, l0_s5_b_glu)
    x = _ffn(x, l0_ffn2_norm, l0_ffn2_w_gate, l0_ffn2_w_up, l0_ffn2_w_down,
             mix=(flat(ya), flat(yb), l0_w_out[:HALF], l0_w_out[HALF:]))

    x = _ffn(x, l1_ffn1_norm, l1_ffn1_w_gate, l1_ffn1_w_up, l1_ffn1_w_down)
    qk_m, v_m, o_m, qf_t, k_f, vf_t, gates = _proj1(x, l1_mix_norm, l1_w_in, s)
    hm = _mlstm(seq(qk_m), seq(v_m), seq(o_m), seq(gates), l1_mlstm_conv_w, l1_mlstm_conv_b,
                l1_mlstm_b_i, l1_mlstm_b_f, l1_mlstm_out_norm)
    frow, fk = _fox_cumsum(seq(gates), l1_fox_b_f)
    hf = _fox_attention(qf_t, seq(k_f), fk, vf_t, frow)
    x = _ffn(x, l1_ffn2_norm, l1_ffn2_w_gate, l1_ffn2_w_up, l1_ffn2_w_down,
             mix=(flat(hm), flat(hf), l1_w_out[:HALF], l1_w_out[HALF:]), final_norm=final_norm)
    return x.reshape(b, s, d)
```

```python
import functools
import math

import jax
import jax.numpy as jnp
import numpy as np
from jax import lax
from jax.experimental import pallas as pl
from jax.experimental.pallas import tpu as pltpu

F32 = jnp.float32
BF16 = jnp.bfloat16

D_MODEL = 1024
D_FF = 2816
NORM_EPS = 1e-6
ROPE_THETA = 500000.0
DIFF_HEADS = 4
DIFF_HEAD_DIM = 64
ROT_DIM = DIFF_HEAD_DIM // 4
DIFF_LAMBDA_INIT = 0.8 - 0.6 * math.exp(-0.3 * 0)
S5_GROUPS = 32
S5_GROUP = 16
S5_STATE = 64
S5_NSTATE = S5_GROUPS * S5_STATE
MLSTM_HEADS = 4
MLSTM_QK_DIM = 64
MLSTM_V_DIM = 128
MLSTM_CONV = 4
FOX_HEADS = 8
FOX_HEAD_DIM = 64
HALF = 512

LANES = 128
SUBLANES = 8
VMEM_LIMIT_BYTES = 60 * 1024 * 1024

TOKEN_TILE = 1024
FFN_TILE = 512
FF_CHUNK = 256
ATTN_TILE = 512
S5_TILE = 1024
S5_CHUNK = 16
MLSTM_CHUNK = 512
GATE_COLS = LANES

NT_DIMS = (((1,), (1,)), ((), ()))
LOG2_E = math.log2(math.e)
ONES_ROWS = 16


def _params(*semantics):
    return pltpu.CompilerParams(dimension_semantics=semantics, vmem_limit_bytes=VMEM_LIMIT_BYTES)


def _resident(shape):
    return pl.BlockSpec(shape, lambda *_: (0,) * len(shape), pipeline_mode=pl.Buffered(1))


def _rms(x, g):
    return x * lax.rsqrt(jnp.mean(x * x, axis=-1, keepdims=True) + NORM_EPS) * g


def _log_sigmoid(x):
    return jnp.minimum(x, 0.0) - jnp.log1p(jnp.exp(-jnp.abs(x)))


def _cumsum_rows(x):
    n = x.shape[0]
    row = lax.broadcasted_iota(jnp.int32, x.shape, 0)
    k = 1
    while k < n:
        x = x + jnp.where(row >= k, pltpu.roll(x, k, 0), 0.0)
        k *= 2
    return x


def _keep_half_rows(x, half):
    rows = x.shape[0] // 2
    zeros = jnp.zeros((rows,) + x.shape[1:], x.dtype)
    return jnp.concatenate([x[:rows], zeros] if half == 0 else [zeros, x[rows:]], axis=0)


def _staged(n, scores, weights, accumulate):
    s, w = {}, {}
    for step in range(n + 2):
        if step < n:
            s[step] = scores(step)
        if step >= 2:
            accumulate(step - 2, *w.pop(step - 2))
        if 1 <= step <= n:
            w[step - 1] = weights(step - 1, s.pop(step - 1))


def _store_key_blocks(ref, value):
    width = ref.shape[3]
    for blk in range(ref.shape[1]):
        ref[0, blk] = value[:, blk * width:(blk + 1) * width]


def _ffn_body(*refs, has_mix, has_final):
    refs = list(refs)
    x_ref = refs.pop(0)
    if has_mix:
        a_ref, b_ref, wa_ref, wb_ref = refs[:4]
        refs = refs[4:]
    g_ref, wg_ref, wu_ref, wd_ref = refs[:4]
    refs = refs[4:]
    if has_final:
        gf_ref = refs.pop(0)
    o_ref = refs.pop(0)

    x = x_ref[...]
    if has_mix:
        x = x + jnp.dot(a_ref[...], wa_ref[...], preferred_element_type=F32)
        x = x + jnp.dot(b_ref[...], wb_ref[...], preferred_element_type=F32)
    h = _rms(x, g_ref[...]).astype(BF16)
    acc = jnp.zeros_like(x)
    for c in range(D_FF // FF_CHUNK):
        sl = slice(c * FF_CHUNK, (c + 1) * FF_CHUNK)
        gate = jnp.dot(h, wg_ref[:, sl].astype(BF16), preferred_element_type=F32)
        up = jnp.dot(h, wu_ref[:, sl].astype(BF16), preferred_element_type=F32)
        act = (gate * jax.nn.sigmoid(gate) * up).astype(BF16)
        acc = acc + jnp.dot(act, wd_ref[sl, :].astype(BF16), preferred_element_type=F32)
    y = x + 0.5 * acc
    if has_final:
        y = _rms(y, gf_ref[...])
    o_ref[...] = y


def _ffn(x, norm, wg, wu, wd, mix=None, final_norm=None):
    n, d = x.shape
    tm = min(FFN_TILE, n)
    row = lambda i: (i, 0)
    args = [x]
    specs = [pl.BlockSpec((tm, d), row)]
    if mix is not None:
        a, b, wa, wb = mix
        args += [a, b, wa.astype(BF16), wb.astype(BF16)]
        specs += [pl.BlockSpec((tm, HALF), row), pl.BlockSpec((tm, HALF), row),
                  _resident((HALF, d)), _resident((HALF, d))]
    args += [norm.reshape(1, d), wg, wu, wd]
    specs += [_resident((1, d)), _resident((d, D_FF)), _resident((d, D_FF)), _resident((D_FF, d))]
    if final_norm is not None:
        args.append(final_norm.reshape(1, d))
        specs.append(_resident((1, d)))
    body = functools.partial(_ffn_body, has_mix=mix is not None, has_final=final_norm is not None)
    return pl.pallas_call(
        body,
        grid=(n // tm,),
        in_specs=specs,
        out_specs=pl.BlockSpec((tm, d), row),
        out_shape=jax.ShapeDtypeStruct((n, d), F32),
        compiler_params=_params("parallel"),
        name="ffn",
    )(*args)


def _rope_angles(seq):
    pos = np.arange(seq, dtype=np.float32)
    inv = np.float32(ROPE_THETA) ** (-np.arange(0, ROT_DIM, 2, dtype=np.float32) / np.float32(ROT_DIM))
    ang = pos[:, None] * inv[None, :]
    return np.cos(ang), np.sin(ang)


def _rope_lane_tables(seq):
    half = ROT_DIM // 2
    cos, sin = _rope_angles(seq)
    one = np.ones((seq, DIFF_HEAD_DIM - ROT_DIM), np.float32)
    zero8 = np.zeros((seq, half), np.float32)
    zero48 = np.zeros((seq, DIFF_HEAD_DIM - ROT_DIM), np.float32)
    cos_t = np.concatenate([cos, cos, one] * 2, axis=1)
    sa_t = np.concatenate([-sin, zero8, zero48] * 2, axis=1)
    sb_t = np.concatenate([zero8, sin, zero48] * 2, axis=1)
    return cos_t, sa_t, sb_t


def _proj0_body(x_ref, g_ref, wt_ref, w_ref, cos_ref, sa_ref, sb_ref, cosr_ref, sinr_ref,
                qt_ref, k_ref, vt_ref, u_ref):
    h = _rms(x_ref[...], g_ref[...]).astype(BF16)
    pt = lax.dot_general(wt_ref[...], h, NT_DIMS, preferred_element_type=F32)
    p = jnp.dot(h, w_ref[...], preferred_element_type=F32)
    half = ROT_DIM // 2
    scale = DIFF_HEAD_DIM ** -0.5 * LOG2_E

    cos_r, sin_r = cosr_ref[...], sinr_ref[...]
    for comp in range(2 * DIFF_HEADS):
        r0 = comp * DIFF_HEAD_DIM
        x1, x2 = pt[r0:r0 + half], pt[r0 + half:r0 + ROT_DIM]
        qt_ref[0, r0:r0 + half, :] = ((x1 * cos_r - x2 * sin_r) * scale).astype(BF16)
        qt_ref[0, r0 + half:r0 + ROT_DIM, :] = ((x2 * cos_r + x1 * sin_r) * scale).astype(BF16)
        qt_ref[0, r0 + ROT_DIM:r0 + DIFF_HEAD_DIM, :] = (pt[r0 + ROT_DIM:r0 + DIFF_HEAD_DIM] * scale).astype(BF16)
    _store_key_blocks(vt_ref, pt[HALF:2 * HALF].astype(BF16))

    cos_t, sa_t, sb_t = cos_ref[...], sa_ref[...], sb_ref[...]
    for hd in range(DIFF_HEADS):
        sl = slice(hd * LANES, (hd + 1) * LANES)
        t = p[:, sl]
        k_ref[:, sl] = (t * cos_t + pltpu.roll(t, LANES - half, 1) * sa_t
                        + pltpu.roll(t, half, 1) * sb_t).astype(BF16)
    u_ref[...] = p[:, HALF:2 * HALF]


def _proj0(x, norm, w_in, seq):
    n, d = x.shape
    b = n // seq
    tm = min(TOKEN_TILE, seq)
    tps = seq // tm
    row = lambda i: (i, 0)
    pos = lambda i: (i % tps, 0)
    pos_t = lambda i: (0, i % tps)
    seq_t = lambda i: (i // tps, 0, i % tps)
    ta = min(ATTN_TILE, seq)
    key_blk = lambda i: (i // tps, i % tps, 0, 0)
    cos, sin = _rope_angles(seq)
    w_t = jnp.concatenate([w_in[:, :HALF], w_in[:, 2 * HALF:3 * HALF]], axis=1).T.astype(BF16)
    w_n = jnp.concatenate([w_in[:, HALF:2 * HALF], w_in[:, 3 * HALF:]], axis=1).astype(BF16)
    t_shape = jax.ShapeDtypeStruct((b, HALF, seq), BF16)
    return pl.pallas_call(
        _proj0_body,
        grid=(n // tm,),
        in_specs=[pl.BlockSpec((tm, d), row), _resident((1, d)), _resident((2 * HALF, d)),
                  _resident((d, 2 * HALF)),
                  pl.BlockSpec((tm, LANES), pos), pl.BlockSpec((tm, LANES), pos),
                  pl.BlockSpec((tm, LANES), pos),
                  pl.BlockSpec((ROT_DIM // 2, tm), pos_t), pl.BlockSpec((ROT_DIM // 2, tm), pos_t)],
        out_specs=[pl.BlockSpec((1, HALF, tm), seq_t), pl.BlockSpec((tm, HALF), row),
                   pl.BlockSpec((1, tm // ta, HALF, ta), key_blk), pl.BlockSpec((tm, HALF), row)],
        out_shape=[t_shape, jax.ShapeDtypeStruct((n, HALF), BF16),
                   jax.ShapeDtypeStruct((b, seq // ta, HALF, ta), BF16),
                   jax.ShapeDtypeStruct((n, HALF), F32)],
        compiler_params=_params("parallel"),
        name="proj0",
    )(x, norm.reshape(1, d), w_t, w_n, *_rope_lane_tables(seq), cos.T, sin.T)


def _diff_attn_body(lam_ref, subln_ref, qt_ref, k_ref, vt_ref, o_ref, m_ref, acc_ref, *, tile):
    i = pl.program_id(1)
    m_ref[...] = jnp.full_like(m_ref, -jnp.inf)
    acc_ref[...] = jnp.zeros_like(acc_ref)
    ones = jnp.ones((ONES_ROWS, tile), BF16)

    def sweep(j, diagonal):
        key0 = pl.multiple_of(j * tile, tile)
        if diagonal:
            query = lax.broadcasted_iota(jnp.int32, (tile, 2 * tile), 1)
            query = jnp.where(query >= tile, query - tile, query)
            keep = lax.broadcasted_iota(jnp.int32, (tile, 2 * tile), 0) <= query

        def scores(hd):
            rows = slice(hd * LANES, (hd + 1) * LANES)
            q_t = qt_ref[0, rows, :]
            q_both = jnp.concatenate([_keep_half_rows(q_t, 0), _keep_half_rows(q_t, 1)], axis=1)
            return jnp.dot(k_ref[0, pl.ds(key0, tile), rows], q_both, preferred_element_type=F32)

        def weights(hd, s):
            if diagonal:
                s = jnp.where(keep, s, -jnp.inf)
            m_old = m_ref[hd]
            m_new = jnp.maximum(m_old, jnp.max(s, axis=0, keepdims=True))
            m_ref[hd] = m_new
            return jnp.exp2(m_old - m_new), jnp.exp2(s - m_new).astype(BF16)

        def accumulate(hd, alpha, p):
            v_aug = jnp.concatenate([vt_ref[0, j, hd * LANES:(hd + 1) * LANES, :], ones], axis=0)
            acc_ref[hd] = alpha * acc_ref[hd] + jnp.dot(v_aug, p, preferred_element_type=F32)

        _staged(DIFF_HEADS, scores, weights, accumulate)

    def off_diagonal(j, carry):
        sweep(j, False)
        return carry

    lax.fori_loop(0, i, off_diagonal, 0)
    sweep(i, True)

    lam_v = lam_ref[...]
    lam = (jnp.exp(jnp.sum(lam_v[0:1] * lam_v[1:2], keepdims=True))
           - jnp.exp(jnp.sum(lam_v[2:3] * lam_v[3:4], keepdims=True)) + DIFF_LAMBDA_INIT)
    for hd in range(DIFF_HEADS):
        a1, a2 = acc_ref[hd, :, :tile], acc_ref[hd, :, tile:]
        o_t = (a1[:LANES] / a1[LANES:LANES + 1] - lam * (a2[:LANES] / a2[LANES:LANES + 1]))
        o_t = o_t * lax.rsqrt(jnp.mean(o_t * o_t, axis=0, keepdims=True) + NORM_EPS)
        o_t = o_t * subln_ref[...] * (1.0 - DIFF_LAMBDA_INIT)
        o_ref[0, :, hd * LANES:(hd + 1) * LANES] = o_t.T.astype(BF16)


def _diff_attention(q_t, k, v_t, lam_vecs, subln):
    b, s, _ = k.shape
    n_blk, _, t = v_t.shape[1:]
    const = lambda bi, i: (0, 0)
    return pl.pallas_call(
        functools.partial(_diff_attn_body, tile=t),
        grid=(b, n_blk),
        in_specs=[pl.BlockSpec((4, DIFF_HEAD_DIM), const), pl.BlockSpec((LANES, 1), const),
                  pl.BlockSpec((1, HALF, t), lambda bi, i: (bi, 0, i)),
                  pl.BlockSpec((1, s, HALF), lambda bi, i: (bi, 0, 0)),
                  pl.BlockSpec((1, n_blk, HALF, t), lambda bi, i: (bi, 0, 0, 0))],
        out_specs=pl.BlockSpec((1, t, HALF), lambda bi, i: (bi, i, 0)),
        out_shape=jax.ShapeDtypeStruct((b, s, HALF), BF16),
        scratch_shapes=[pltpu.VMEM((DIFF_HEADS, 1, 2 * t), F32),
                        pltpu.VMEM((DIFF_HEADS, LANES + ONES_ROWS, 2 * t), F32)],
        compiler_params=_params("parallel", "arbitrary"),
        name="diff_attn",
    )(lam_vecs, subln.reshape(LANES, 1), q_t, k, v_t)


def _s5_discretize_body(lr_ref, li_ref, ldt_ref, br_ref, bi_ref, k_ref, pr_ref, pi_ref, bbr_ref, bbi_ref):
    lr, li = lr_ref[...], li_ref[...]
    dt = jnp.exp(ldt_ref[...])
    mag = jnp.exp(lr * dt)
    ar, ai = mag * jnp.cos(li * dt), mag * jnp.sin(li * dt)
    den = lr * lr + li * li
    gr = ((ar - 1.0) * lr + ai * li) / den
    gi = (ai * lr - (ar - 1.0) * li) / den
    br, bi = br_ref[...], bi_ref[...]
    bbr_ref[...] = gr * br - gi * bi
    bbi_ref[...] = gr * bi + gi * br
    steps = k_ref[...]
    mag_k = jnp.exp(lr * dt * steps)
    pr_ref[...] = mag_k * jnp.cos(li * dt * steps)
    pi_ref[...] = mag_k * jnp.sin(li * dt * steps)


def _s5_discretize(a_re, a_im, log_dt, b_re, b_im):
    col = lambda a: a.reshape(S5_NSTATE, 1)
    ldt = jnp.broadcast_to(log_dt[:, None], (S5_GROUPS, S5_STATE))
    steps = jnp.arange(1, S5_CHUNK + 1, dtype=F32).reshape(1, S5_CHUNK)
    pshape = jax.ShapeDtypeStruct((S5_NSTATE, S5_CHUNK), F32)
    bshape = jax.ShapeDtypeStruct((S5_NSTATE, S5_GROUP), F32)
    return pl.pallas_call(
        _s5_discretize_body,
        out_shape=[pshape, pshape, bshape, bshape],
        name="s5_discretize",
    )(col(a_re), col(a_im), col(ldt), b_re.reshape(S5_NSTATE, S5_GROUP), b_im.reshape(S5_NSTATE, S5_GROUP), steps)


def _s5_body(u_ref, pr_ref, pi_ref, bbr_ref, bbi_ref, cr_ref, ci_ref, d_ref, wglu_ref, bglu_ref,
             o_ref, us_ref, ys_ref, xr_ref, xi_ref, hin_ref, carry_ref, *, tile):
    n_chunks = tile // S5_CHUNK
    slabs = HALF // LANES
    gw = S5_NSTATE // slabs

    @pl.when(pl.program_id(1) == 0)
    def _():
        carry_ref[...] = jnp.zeros_like(carry_ref)

    u = u_ref[0]
    for g in range(slabs):
        us_ref[g] = u[:, g * LANES:(g + 1) * LANES]
    u_perm = jnp.concatenate(
        [jnp.concatenate([us_ref[g, pl.ds(st, n_chunks, stride=S5_CHUNK), :] for g in range(slabs)], axis=1)
         for st in range(S5_CHUNK)], axis=0).astype(BF16)

    for g in range(slabs):
        ug = u_perm[:, g * LANES:(g + 1) * LANES]
        xr_ref[:, g * gw:(g + 1) * gw] = jnp.dot(ug, bbr_ref[g], preferred_element_type=F32)
        xi_ref[:, g * gw:(g + 1) * gw] = jnp.dot(ug, bbi_ref[g], preferred_element_type=F32)

    ar, ai = pr_ref[0:1, :], pi_ref[0:1, :]
    hr, hi = xr_ref[0:n_chunks, :], xi_ref[0:n_chunks, :]
    for st in range(1, S5_CHUNK):
        rows = slice(st * n_chunks, (st + 1) * n_chunks)
        hr, hi = ar * hr - ai * hi + xr_ref[rows, :], ar * hi + ai * hr + xi_ref[rows, :]
        xr_ref[rows, :] = hr
        xi_ref[rows, :] = hi

    last = (S5_CHUNK - 1) * n_chunks
    ac_r, ac_i = pr_ref[S5_CHUNK - 1:S5_CHUNK, :], pi_ref[S5_CHUNK - 1:S5_CHUNK, :]

    def carry_step(c, carry):
        cr, ci = carry
        hin_ref[0, pl.ds(c, 1), :] = cr
        hin_ref[1, pl.ds(c, 1), :] = ci
        zr, zi = xr_ref[pl.ds(last + c, 1), :], xi_ref[pl.ds(last + c, 1), :]
        return ac_r * cr - ac_i * ci + zr, ac_r * ci + ac_i * cr + zi

    cr, ci = lax.fori_loop(0, n_chunks, carry_step, (carry_ref[0:1, :], carry_ref[1:2, :]), unroll=4)
    carry_ref[0:1, :] = cr
    carry_ref[1:2, :] = ci

    hin_r, hin_i = hin_ref[0], hin_ref[1]
    for st in range(S5_CHUNK):
        rows = slice(st * n_chunks, (st + 1) * n_chunks)
        p_r, p_i = pr_ref[st:st + 1, :], pi_ref[st:st + 1, :]
        xr_ref[rows, :] = xr_ref[rows, :] + (p_r * hin_r - p_i * hin_i)
        xi_ref[rows, :] = xi_ref[rows, :] + (p_r * hin_i + p_i * hin_r)
    for g in range(slabs):
        cols = slice(g * gw, (g + 1) * gw)
        yg = (jnp.dot(xr_ref[:, cols].astype(BF16), cr_ref[g], preferred_element_type=F32)
              - jnp.dot(xi_ref[:, cols].astype(BF16), ci_ref[g], preferred_element_type=F32))
        for st in range(S5_CHUNK):
            ys_ref[g, pl.ds(st, n_chunks, stride=S5_CHUNK), :] = yg[st * n_chunks:(st + 1) * n_chunks]

    y = jnp.concatenate([ys_ref[g] for g in range(slabs)], axis=1) + d_ref[...] * u
    z = jax.nn.gelu(y, approximate=True)
    gate = jnp.dot(z.astype(BF16), wglu_ref[...], preferred_element_type=F32) + bglu_ref[...]
    o_ref[0] = (z * jax.nn.sigmoid(gate)).astype(BF16)


def _s5(u, a_re, a_im, log_dt, b_re, b_im, c_re, c_im, d_skip, w_glu, b_glu):
    b, s, w = u.shape
    t = min(S5_TILE, s)
    slabs = w // LANES
    per = S5_GROUPS // slabs
    gw = S5_NSTATE // slabs
    pw_r, pw_i, bbr, bbi = _s5_discretize(a_re, a_im, log_dt, b_re, b_im)
    eye = jnp.eye(per, dtype=F32)

    def block_in(m):
        m = m.reshape(slabs, per, S5_STATE, S5_GROUP)
        return jnp.einsum('sgpc,gh->sgchp', m, eye).reshape(slabs, LANES, gw).astype(BF16)

    def block_out(m):
        m = m.reshape(slabs, per, S5_GROUP, S5_STATE)
        return jnp.einsum('sgcp,gh->sgphc', m, eye).reshape(slabs, gw, LANES).astype(BF16)

    seq_blk = lambda bi, ti: (bi, ti, 0)
    n_chunks = t // S5_CHUNK
    return pl.pallas_call(
        functools.partial(_s5_body, tile=t),
        grid=(b, s // t),
        in_specs=[pl.BlockSpec((1, t, w), seq_blk),
                  _resident((S5_CHUNK, S5_NSTATE)), _resident((S5_CHUNK, S5_NSTATE)),
                  _resident((slabs, LANES, gw)), _resident((slabs, LANES, gw)),
                  _resident((slabs, gw, LANES)), _resident((slabs, gw, LANES)),
                  _resident((1, w)), _resident((w, w)), _resident((1, w))],
        out_specs=pl.BlockSpec((1, t, w), seq_blk),
        out_shape=jax.ShapeDtypeStruct((b, s, w), BF16),
        scratch_shapes=[pltpu.VMEM((slabs, t, LANES), F32), pltpu.VMEM((slabs, t, LANES), F32),
                        pltpu.VMEM((t, S5_NSTATE), F32), pltpu.VMEM((t, S5_NSTATE), F32),
                        pltpu.VMEM((2, n_chunks, S5_NSTATE), F32), pltpu.VMEM((2, S5_NSTATE), F32)],
        compiler_params=_params("parallel", "arbitrary"),
        name="s5",
    )(u, pw_r.T, pw_i.T, block_in(bbr), block_in(bbi),
      block_out(c_re), block_out(c_im), d_skip.reshape(1, w), w_glu.astype(BF16), b_glu.reshape(1, w))


def _proj1_body(x_ref, g_ref, wt_ref, w_ref, qk_ref, vm_ref, om_ref, qft_ref, kf_ref, vft_ref, gt_ref):
    h = _rms(x_ref[...], g_ref[...]).astype(BF16)
    pt = lax.dot_general(wt_ref[...], h, NT_DIMS, preferred_element_type=F32)
    p = jnp.dot(h, w_ref[...], preferred_element_type=F32)
    qft_ref[0] = (pt[:HALF] * (FOX_HEAD_DIM ** -0.5 * LOG2_E)).astype(BF16)
    _store_key_blocks(vft_ref, pt[HALF:].astype(BF16))
    qk_ref[...] = p[:, 0:HALF]
    vm_ref[...] = p[:, HALF:2 * HALF].astype(BF16)
    om_ref[...] = p[:, 2 * HALF:3 * HALF]
    kf_ref[...] = p[:, 3 * HALF:4 * HALF].astype(BF16)
    gt_ref[...] = p[:, 4 * HALF:4 * HALF + GATE_COLS]


def _proj1(x, norm, w_in, seq):
    n, d = x.shape
    b = n // seq
    tm = min(TOKEN_TILE, seq)
    tps = seq // tm
    row = lambda i: (i, 0)
    seq_t = lambda i: (i // tps, 0, i % tps)
    ta = min(ATTN_TILE, seq)
    key_blk = lambda i: (i // tps, i % tps, 0, 0)
    o = 2 * HALF
    g = 2 * MLSTM_HEADS
    om = o + g
    w_n = jnp.concatenate(
        [w_in[:, :o], w_in[:, om:om + HALF], w_in[:, om + 2 * HALF:om + 3 * HALF],
         w_in[:, o:om], w_in[:, om + 4 * HALF:],
         jnp.zeros((d, GATE_COLS - g - FOX_HEADS), w_in.dtype)], axis=1).astype(BF16)
    w_t = jnp.concatenate([w_in[:, om + HALF:om + 2 * HALF], w_in[:, om + 3 * HALF:om + 4 * HALF]],
                          axis=1).T.astype(BF16)
    f32_half = jax.ShapeDtypeStruct((n, HALF), F32)
    bf_half = jax.ShapeDtypeStruct((n, HALF), BF16)
    t_shape = jax.ShapeDtypeStruct((b, HALF, seq), BF16)
    half_spec = pl.BlockSpec((tm, HALF), row)
    t_spec = pl.BlockSpec((1, HALF, tm), seq_t)
    return pl.pallas_call(
        _proj1_body,
        grid=(n // tm,),
        in_specs=[pl.BlockSpec((tm, d), row), _resident((1, d)), _resident((2 * HALF, d)),
                  _resident((d, 4 * HALF + GATE_COLS))],
        out_specs=[half_spec, half_spec, half_spec, t_spec, half_spec,
                   pl.BlockSpec((1, tm // ta, HALF, ta), key_blk), pl.BlockSpec((tm, GATE_COLS), row)],
        out_shape=[f32_half, bf_half, f32_half, t_shape, bf_half,
                   jax.ShapeDtypeStruct((b, seq // ta, HALF, ta), BF16),
                   jax.ShapeDtypeStruct((n, GATE_COLS), F32)],
        compiler_params=_params("parallel"),
        name="proj1",
    )(x, norm.reshape(1, d), w_t, w_n)


def _mlstm_body(qk_ref, v_ref, og_ref, gt_ref, cw_ref, cb_ref, gb_ref, on_ref, out_ref,
                xbuf_ref, cn_ref, m_ref, *, chunk):
    pad = SUBLANES
    dk, dv = MLSTM_QK_DIM, MLSTM_V_DIM

    @pl.when(pl.program_id(1) == 0)
    def _():
        xbuf_ref[0:pad, :] = jnp.zeros((pad, HALF), F32)
        cn_ref[...] = jnp.zeros_like(cn_ref)
        m_ref[...] = jnp.zeros_like(m_ref)

    x = qk_ref[0]
    xbuf_ref[pad:pad + chunk, :] = x
    a = jnp.zeros_like(x) + cb_ref[...]
    for tap in range(MLSTM_CONV):
        a = a + cw_ref[tap:tap + 1, :] * xbuf_ref[pl.ds(pad - (MLSTM_CONV - 1) + tap, chunk), :]
    xbuf_ref[0:pad, :] = x[chunk - pad:chunk, :]
    a = a * jax.nn.sigmoid(a)
    q = a[:, :MLSTM_HEADS * dk] * dk ** -0.5
    k = a[:, MLSTM_HEADS * dk:]
    k_t = k.T

    gi = gt_ref[0] + gb_ref[...]
    bcum = _cumsum_rows(_log_sigmoid(gi))
    gi_t = gi.T
    bcum_t = bcum.T
    tri = (lax.broadcasted_iota(jnp.int32, (chunk, chunk), 1)
           <= lax.broadcasted_iota(jnp.int32, (chunk, chunk), 0))
    lane = lax.broadcasted_iota(jnp.int32, (1, LANES), 1)
    v_all = v_ref[0]
    o_gate = og_ref[0]
    ones = jnp.ones((chunk, dv), BF16)

    states = [cn_ref[pair] for pair in range(MLSTM_HEADS // 2)]

    def products(hd):
        pair, odd = hd // 2, hd % 2
        sel = (lane >= dk) if odd else (lane < dk)
        q2 = jnp.where(sel, q[:, pair * LANES:(pair + 1) * LANES], 0.0).astype(BF16)
        k2 = k[:, pair * LANES:(pair + 1) * LANES].astype(BF16)
        qk = lax.dot_general(q2, k2, NT_DIMS, preferred_element_type=F32)
        return qk, jnp.dot(q2, states[pair].astype(BF16), preferred_element_type=F32)

    def gate_weights(hd, prods):
        qk, q_state = prods
        f_lane = MLSTM_HEADS + hd
        b_col = bcum[:, f_lane:f_lane + 1]
        b_row = bcum_t[f_lane:f_lane + 1, :]
        i_row = gi_t[hd:hd + 1, :]
        m_prev = m_ref[hd]
        dm = jnp.where(tri, b_col - b_row + i_row, -jnp.inf)
        g = b_col + m_prev
        mt = jnp.maximum(g, jnp.max(dm, axis=-1, keepdims=True))
        s = (qk * jnp.exp(dm - mt)).astype(BF16)
        w_inter = jnp.exp(g - mt)

        b_last = bcum[chunk - 1:chunk, f_lane:f_lane + 1]
        dec = b_last - b_row + i_row
        m_new = jnp.maximum(b_last + m_prev, jnp.max(dec, axis=-1, keepdims=True))
        k_w = (k_t[hd * dk:(hd + 1) * dk, :] * jnp.exp(dec - m_new)).astype(BF16)
        carry_w = jnp.exp(b_last + m_prev - m_new)
        m_ref[hd] = m_new
        return s, k_w, q_state, w_inter, mt, carry_w

    def outputs(hd, s, k_w, q_state, w_inter, mt, carry_w):
        pair, odd = hd // 2, hd % 2
        v_aug = jnp.concatenate([v_all[:, hd * dv:(hd + 1) * dv], ones], axis=1)
        s_v = jnp.dot(s, v_aug, preferred_element_type=F32)
        num = w_inter * q_state[:, :dv] + s_v[:, :dv]
        den = w_inter * q_state[:, dv:] + s_v[:, dv:]
        h_out = num / jnp.maximum(jnp.abs(den), jnp.exp(-mt))
        rows = slice(odd * dk, (odd + 1) * dk)
        cn_ref[pair, rows, :] = (carry_w * states[pair][rows, :]
                                 + jnp.dot(k_w, v_aug, preferred_element_type=F32))
        cols = slice(hd * dv, (hd + 1) * dv)
        h_norm = _rms(h_out, on_ref[hd:hd + 1, :])
        out_ref[0, :, cols] = (jax.nn.sigmoid(o_gate[:, cols]) * h_norm).astype(BF16)

    _staged(MLSTM_HEADS, products, gate_weights, outputs)


def _mlstm(qk, v, o_gate, gates, conv_w, conv_b, b_i, b_f, out_norm):
    b, s, _ = qk.shape
    c = min(MLSTM_CHUNK, s)
    gate_bias = jnp.concatenate([b_i, b_f, jnp.zeros((GATE_COLS - 2 * MLSTM_HEADS,), F32)]).reshape(1, GATE_COLS)
    seq_blk = lambda bi, ci: (bi, ci, 0)
    half_spec = pl.BlockSpec((1, c, HALF), seq_blk)
    return pl.pallas_call(
        functools.partial(_mlstm_body, chunk=c),
        grid=(b, s // c),
        in_specs=[half_spec, half_spec, half_spec, pl.BlockSpec((1, c, GATE_COLS), seq_blk),
                  _resident((MLSTM_CONV, HALF)), _resident((1, HALF)), _resident((1, GATE_COLS)),
                  _resident((MLSTM_HEADS, MLSTM_V_DIM))],
        out_specs=half_spec,
        out_shape=jax.ShapeDtypeStruct((b, s, HALF), BF16),
        scratch_shapes=[pltpu.VMEM((c + SUBLANES, HALF), F32),
                        pltpu.VMEM((MLSTM_HEADS // 2, 2 * MLSTM_QK_DIM, 2 * MLSTM_V_DIM), F32),
                        pltpu.VMEM((MLSTM_HEADS, 1, 1), F32)],
        compiler_params=_params("parallel", "arbitrary"),
        name="mlstm",
    )(qk, v, o_gate, gates, conv_w, conv_b.reshape(1, HALF), gate_bias, out_norm)


FOX_GATE_LANE = 2 * MLSTM_HEADS
F_TERMS = 3


def _fox_cumsum_body(gt_ref, gb_ref, sel_ref, frow_ref, fk_ref):
    f = _cumsum_rows(_log_sigmoid(gt_ref[0] + gb_ref[...])) * LOG2_E
    hi = f.astype(BF16)
    rest = f - hi.astype(F32)
    mid = rest.astype(BF16)
    lo = (rest - mid.astype(F32)).astype(BF16)
    terms = jnp.concatenate([hi, mid, lo], axis=1)
    fk_ref[0] = jnp.dot(terms, sel_ref[...], preferred_element_type=F32).astype(BF16)
    frow_ref[0] = f.T[FOX_GATE_LANE:FOX_GATE_LANE + FOX_HEADS, :]


def _fox_cumsum(gates, fox_b_f):
    b, s, _ = gates.shape
    bias = jnp.concatenate([jnp.zeros((FOX_GATE_LANE,), F32), fox_b_f,
                            jnp.zeros((GATE_COLS - FOX_GATE_LANE - FOX_HEADS,), F32)]).reshape(1, GATE_COLS)
    src, dst = [], []
    for head in range(FOX_HEADS):
        for term in range(F_TERMS):
            src.append(term * GATE_COLS + FOX_GATE_LANE + head)
            dst.append((head // 2) * LANES + F_TERMS * (head % 2) + term)
    sel = jnp.zeros((F_TERMS * GATE_COLS, HALF), BF16).at[jnp.asarray(src), jnp.asarray(dst)].set(-1.0)
    return pl.pallas_call(
        _fox_cumsum_body,
        grid=(b,),
        in_specs=[pl.BlockSpec((1, s, GATE_COLS), lambda bi: (bi, 0, 0)), _resident((1, GATE_COLS)),
                  _resident((F_TERMS * GATE_COLS, HALF))],
        out_specs=[pl.BlockSpec((1, FOX_HEADS, s), lambda bi: (bi, 0, 0)),
                   pl.BlockSpec((1, s, HALF), lambda bi: (bi, 0, 0))],
        out_shape=[jax.ShapeDtypeStruct((b, FOX_HEADS, s), F32),
                   jax.ShapeDtypeStruct((b, s, HALF), BF16)],
        compiler_params=_params("parallel"),
        name="fox_cumsum",
    )(gates, bias, sel)


def _fox_attn_body(qt_ref, k_ref, fk_ref, vt_ref, frow_ref, o_ref, m_ref, acc_ref, *, tile):
    i = pl.program_id(1)
    dim = lax.broadcasted_iota(jnp.int32, (LANES, 1), 0)
    dh = FOX_HEAD_DIM
    m_ref[...] = jnp.full_like(m_ref, -jnp.inf)
    acc_ref[...] = jnp.zeros_like(acc_ref)
    ones = jnp.ones((ONES_ROWS, tile), BF16)

    def sweep(j, diagonal):
        key0 = pl.multiple_of(j * tile, tile)
        if diagonal:
            query = lax.broadcasted_iota(jnp.int32, (tile, 2 * tile), 1)
            query = jnp.where(query >= tile, query - tile, query)
            keep = lax.broadcasted_iota(jnp.int32, (tile, 2 * tile), 0) <= query

        def scores(pair):
            rows = slice(pair * LANES, (pair + 1) * LANES)
            q_t = qt_ref[0, rows, :]
            k_aug = jnp.concatenate([k_ref[0, pl.ds(key0, tile), rows],
                                     fk_ref[0, pl.ds(key0, tile), rows]], axis=1)
            q_aug = []
            for c in range(2):
                ones_rows = jnp.logical_and(dim >= F_TERMS * c, dim < F_TERMS * (c + 1))
                q_aug.append(jnp.concatenate(
                    [_keep_half_rows(q_t, c),
                     jnp.broadcast_to(jnp.where(ones_rows, 1.0, 0.0), q_t.shape).astype(BF16)], axis=0))
            return jnp.dot(k_aug, jnp.concatenate(q_aug, axis=1), preferred_element_type=F32)

        def weights(pair, s):
            s = s + jnp.concatenate([frow_ref[0, 2 * pair:2 * pair + 1, :],
                                     frow_ref[0, 2 * pair + 1:2 * pair + 2, :]], axis=1)
            if diagonal:
                s = jnp.where(keep, s, -jnp.inf)
            m_old = m_ref[pair]
            m_new = jnp.maximum(m_old, jnp.max(s, axis=0, keepdims=True))
            m_ref[pair] = m_new
            return jnp.exp2(m_old - m_new), jnp.exp2(s - m_new).astype(BF16)

        def accumulate(pair, alpha, p):
            for c in range(2):
                hd = 2 * pair + c
                cols = slice(c * tile, (c + 1) * tile)
                v_aug = jnp.concatenate([vt_ref[0, j, hd * dh:(hd + 1) * dh, :], ones], axis=0)
                acc_ref[hd] = alpha[:, cols] * acc_ref[hd] + jnp.dot(v_aug, p[:, cols], preferred_element_type=F32)

        _staged(FOX_HEADS // 2, scores, weights, accumulate)

    def off_diagonal(j, carry):
        sweep(j, False)
        return carry

    lax.fori_loop(0, i, off_diagonal, 0)
    sweep(i, True)

    for pair in range(FOX_HEADS // 2):
        a0, a1 = acc_ref[2 * pair], acc_ref[2 * pair + 1]
        o_t = jnp.concatenate([a0[:dh] / a0[dh:dh + 1], a1[:dh] / a1[dh:dh + 1]], axis=0)
        o_ref[0, :, pair * LANES:(pair + 1) * LANES] = o_t.T.astype(BF16)


def _fox_attention(q_t, k, fk, v_t, frow):
    b, s, _ = k.shape
    n_blk, _, t = v_t.shape[1:]
    whole = lambda bi, i: (bi, 0, 0)
    return pl.pallas_call(
        functools.partial(_fox_attn_body, tile=t),
        grid=(b, n_blk),
        in_specs=[pl.BlockSpec((1, HALF, t), lambda bi, i: (bi, 0, i)),
                  pl.BlockSpec((1, s, HALF), whole), pl.BlockSpec((1, s, HALF), whole),
                  pl.BlockSpec((1, n_blk, HALF, t), lambda bi, i: (bi, 0, 0, 0)),
                  pl.BlockSpec((1, FOX_HEADS, t), lambda bi, i: (bi, 0, i))],
        out_specs=pl.BlockSpec((1, t, HALF), lambda bi, i: (bi, i, 0)),
        out_shape=jax.ShapeDtypeStruct((b, s, HALF), BF16),
        scratch_shapes=[pltpu.VMEM((FOX_HEADS // 2, 1, 2 * t), F32),
                        pltpu.VMEM((FOX_HEADS, FOX_HEAD_DIM + ONES_ROWS, t), F32)],
        compiler_params=_params("parallel", "arbitrary"),
        name="fox_attn",
    )(q_t, k, fk, v_t, frow)


def kernel(x, l0_ffn1_norm, l0_ffn1_w_gate, l0_ffn1_w_up, l0_ffn1_w_down, l0_mix_norm, l0_w_in, l0_diff_lambda_q1, l0_diff_lambda_k1, l0_diff_lambda_q2, l0_diff_lambda_k2, l0_diff_subln, l0_s5_a_re, l0_s5_a_im, l0_s5_log_dt, l0_s5_b_re, l0_s5_b_im, l0_s5_c_re, l0_s5_c_im, l0_s5_d, l0_s5_w_glu, l0_s5_b_glu, l0_w_out, l0_ffn2_norm, l0_ffn2_w_gate, l0_ffn2_w_up, l0_ffn2_w_down, l1_ffn1_norm, l1_ffn1_w_gate, l1_ffn1_w_up, l1_ffn1_w_down, l1_mix_norm, l1_w_in, l1_mlstm_conv_w, l1_mlstm_conv_b, l1_mlstm_b_i, l1_mlstm_b_f, l1_mlstm_out_norm, l1_fox_b_f, l1_w_out, l1_ffn2_norm, l1_ffn2_w_gate, l1_ffn2_w_up, l1_ffn2_w_down, final_norm):
    b, s, d = x.shape
    n = b * s
    seq = lambda a: a.reshape(b, s, a.shape[-1])
    flat = lambda a: a.reshape(n, a.shape[-1])

    x = _ffn(x.reshape(n, d), l0_ffn1_norm, l0_ffn1_w_gate, l0_ffn1_w_up, l0_ffn1_w_down)
    q_t, k, v_t, u = _proj0(x, l0_mix_norm, l0_w_in, s)
    lam_vecs = jnp.stack([l0_diff_lambda_q1, l0_diff_lambda_k1, l0_diff_lambda_q2, l0_diff_lambda_k2])
    ya = _diff_attention(q_t, seq(k), v_t, lam_vecs, l0_diff_subln)
    yb = _s5(seq(u), l0_s5_a_re, l0_s5_a_im, l0_s5_log_dt, l0_s5_b_re, l0_s5_b_im,
             l0_s5_c_re, l0_s5_c_im, l0_s5_d, l0_s5_w_glu, l0_s5_b_glu)
    x = _ffn(x, l0_ffn2_norm, l0_ffn2_w_gate, l0_ffn2_w_up, l0_ffn2_w_down,
             mix=(flat(ya), flat(yb), l0_w_out[:HALF], l0_w_out[HALF:]))

    x = _ffn(x, l1_ffn1_norm, l1_ffn1_w_gate, l1_ffn1_w_up, l1_ffn1_w_down)
    qk_m, v_m, o_m, qf_t, k_f, vf_t, gates = _proj1(x, l1_mix_norm, l1_w_in, s)
    hm = _mlstm(seq(qk_m), seq(v_m), seq(o_m), seq(gates), l1_mlstm_conv_w, l1_mlstm_conv_b,
                l1_mlstm_b_i, l1_mlstm_b_f, l1_mlstm_out_norm)
    frow, fk = _fox_cumsum(seq(gates), l1_fox_b_f)
    hf = _fox_attention(qf_t, seq(k_f), fk, vf_t, frow)
    x = _ffn(x, l1_ffn2_norm, l1_ffn2_w_gate, l1_ffn2_w_up, l1_ffn2_w_down,
             mix=(flat(hm), flat(hf), l1_w_out[:HALF], l1_w_out[HALF:]), final_norm=final_norm)
    return x.reshape(b, s, d)
```

```python
import functools
import math

import jax
import jax.numpy as jnp
import numpy as np
from jax import lax
from jax.experimental import pallas as pl
from jax.experimental.pallas import tpu as pltpu

F32 = jnp.float32
BF16 = jnp.bfloat16

D_MODEL = 1024
D_FF = 2816
NORM_EPS = 1e-6
ROPE_THETA = 500000.0
DIFF_HEADS = 4
DIFF_HEAD_DIM = 64
ROT_DIM = DIFF_HEAD_DIM // 4
DIFF_LAMBDA_INIT = 0.8 - 0.6 * math.exp(-0.3 * 0)
S5_GROUPS = 32
S5_GROUP = 16
S5_STATE = 64
S5_NSTATE = S5_GROUPS * S5_STATE
MLSTM_HEADS = 4
MLSTM_QK_DIM = 64
MLSTM_V_DIM = 128
MLSTM_CONV = 4
FOX_HEADS = 8
FOX_HEAD_DIM = 64
HALF = 512

LANES = 128
SUBLANES = 8
VMEM_LIMIT_BYTES = 60 * 1024 * 1024

TOKEN_TILE = 1024
FFN_TILE = 512
FF_CHUNK = 256
ATTN_TILE = 512
S5_TILE = 1024
S5_CHUNK = 16
MLSTM_CHUNK = 512
GATE_COLS = LANES

NT_DIMS = (((1,), (1,)), ((), ()))
LOG2_E = math.log2(math.e)
ONES_ROWS = 16


def _params(*semantics):
    return pltpu.CompilerParams(dimension_semantics=semantics, vmem_limit_bytes=VMEM_LIMIT_BYTES)


def _resident(shape):
    return pl.BlockSpec(shape, lambda *_: (0,) * len(shape), pipeline_mode=pl.Buffered(1))


def _rms(x, g):
    return x * lax.rsqrt(jnp.mean(x * x, axis=-1, keepdims=True) + NORM_EPS) * g


def _log_sigmoid(x):
    return jnp.minimum(x, 0.0) - jnp.log1p(jnp.exp(-jnp.abs(x)))


def _cumsum_rows(x):
    n = x.shape[0]
    row = lax.broadcasted_iota(jnp.int32, x.shape, 0)
    k = 1
    while k < n:
        x = x + jnp.where(row >= k, pltpu.roll(x, k, 0), 0.0)
        k *= 2
    return x


def _keep_half_rows(x, half):
    rows = x.shape[0] // 2
    zeros = jnp.zeros((rows,) + x.shape[1:], x.dtype)
    return jnp.concatenate([x[:rows], zeros] if half == 0 else [zeros, x[rows:]], axis=0)


def _staged(n, scores, weights, accumulate):
    s, w = {}, {}
    for step in range(n + 2):
        if step < n:
            s[step] = scores(step)
        if step >= 2:
            accumulate(step - 2, *w.pop(step - 2))
        if 1 <= step <= n:
            w[step - 1] = weights(step - 1, s.pop(step - 1))


def _store_key_blocks(ref, value):
    width = ref.shape[3]
    for blk in range(ref.shape[1]):
        ref[0, blk] = value[:, blk * width:(blk + 1) * width]


def _ffn_body(*refs, has_mix, has_final):
    refs = list(refs)
    x_ref = refs.pop(0)
    if has_mix:
        a_ref, b_ref, wa_ref, wb_ref = refs[:4]
        refs = refs[4:]
    g_ref, wg_ref, wu_ref, wd_ref = refs[:4]
    refs = refs[4:]
    if has_final:
        gf_ref = refs.pop(0)
    o_ref = refs.pop(0)

    x = x_ref[...]
    if has_mix:
        x = x + jnp.dot(a_ref[...], wa_ref[...], preferred_element_type=F32)
        x = x + jnp.dot(b_ref[...], wb_ref[...], preferred_element_type=F32)
    h = _rms(x, g_ref[...]).astype(BF16)
    acc = jnp.zeros_like(x)
    for c in range(D_FF // FF_CHUNK):
        sl = slice(c * FF_CHUNK, (c + 1) * FF_CHUNK)
        gate = jnp.dot(h, wg_ref[:, sl].astype(BF16), preferred_element_type=F32)
        up = jnp.dot(h, wu_ref[:, sl].astype(BF16), preferred_element_type=F32)
        act = (gate * jax.nn.sigmoid(gate) * up).astype(BF16)
        acc = acc + jnp.dot(act, wd_ref[sl, :].astype(BF16), preferred_element_type=F32)
    y = x + 0.5 * acc
    if has_final:
        y = _rms(y, gf_ref[...])
    o_ref[...] = y


def _ffn(x, norm, wg, wu, wd, mix=None, final_norm=None):
    n, d = x.shape
    tm = min(FFN_TILE, n)
    row = lambda i: (i, 0)
    args = [x]
    specs = [pl.BlockSpec((tm, d), row)]
    if mix is not None:
        a, b, wa, wb = mix
        args += [a, b, wa.astype(BF16), wb.astype(BF16)]
        specs += [pl.BlockSpec((tm, HALF), row), pl.BlockSpec((tm, HALF), row),
                  _resident((HALF, d)), _resident((HALF, d))]
    args += [norm.reshape(1, d), wg, wu, wd]
    specs += [_resident((1, d)), _resident((d, D_FF)), _resident((d, D_FF)), _resident((D_FF, d))]
    if final_norm is not None:
        args.append(final_norm.reshape(1, d))
        specs.append(_resident((1, d)))
    body = functools.partial(_ffn_body, has_mix=mix is not None, has_final=final_norm is not None)
    return pl.pallas_call(
        body,
        grid=(n // tm,),
        in_specs=specs,
        out_specs=pl.BlockSpec((tm, d), row),
        out_shape=jax.ShapeDtypeStruct((n, d), F32),
        compiler_params=_params("parallel"),
        name="ffn",
    )(*args)


def _rope_angles(seq):
    pos = np.arange(seq, dtype=np.float32)
    inv = np.float32(ROPE_THETA) ** (-np.arange(0, ROT_DIM, 2, dtype=np.float32) / np.float32(ROT_DIM))
    ang = pos[:, None] * inv[None, :]
    return np.cos(ang), np.sin(ang)


def _rope_lane_tables(seq):
    half = ROT_DIM // 2
    cos, sin = _rope_angles(seq)
    one = np.ones((seq, DIFF_HEAD_DIM - ROT_DIM), np.float32)
    zero8 = np.zeros((seq, half), np.float32)
    zero48 = np.zeros((seq, DIFF_HEAD_DIM - ROT_DIM), np.float32)
    cos_t = np.concatenate([cos, cos, one] * 2, axis=1)
    sa_t = np.concatenate([-sin, zero8, zero48] * 2, axis=1)
    sb_t = np.concatenate([zero8, sin, zero48] * 2, axis=1)
    return cos_t, sa_t, sb_t


def _proj0_body(x_ref, g_ref, wt_ref, w_ref, cos_ref, sa_ref, sb_ref, cosr_ref, sinr_ref,
                qt_ref, k_ref, vt_ref, u_ref):
    h = _rms(x_ref[...], g_ref[...]).astype(BF16)
    pt = lax.dot_general(wt_ref[...], h, NT_DIMS, preferred_element_type=F32)
    p = jnp.dot(h, w_ref[...], preferred_element_type=F32)
    half = ROT_DIM // 2
    scale = DIFF_HEAD_DIM ** -0.5 * LOG2_E

    cos_r, sin_r = cosr_ref[...], sinr_ref[...]
    for comp in range(2 * DIFF_HEADS):
        r0 = comp * DIFF_HEAD_DIM
        x1, x2 = pt[r0:r0 + half], pt[r0 + half:r0 + ROT_DIM]
        qt_ref[0, r0:r0 + half, :] = ((x1 * cos_r - x2 * sin_r) * scale).astype(BF16)
        qt_ref[0, r0 + half:r0 + ROT_DIM, :] = ((x2 * cos_r + x1 * sin_r) * scale).astype(BF16)
        qt_ref[0, r0 + ROT_DIM:r0 + DIFF_HEAD_DIM, :] = (pt[r0 + ROT_DIM:r0 + DIFF_HEAD_DIM] * scale).astype(BF16)
    _store_key_blocks(vt_ref, pt[HALF:2 * HALF].astype(BF16))

    cos_t, sa_t, sb_t = cos_ref[...], sa_ref[...], sb_ref[...]
    for hd in range(DIFF_HEADS):
        sl = slice(hd * LANES, (hd + 1) * LANES)
        t = p[:, sl]
        k_ref[:, sl] = (t * cos_t + pltpu.roll(t, LANES - half, 1) * sa_t
                        + pltpu.roll(t, half, 1) * sb_t).astype(BF16)
    u_ref[...] = p[:, HALF:2 * HALF]


def _proj0(x, norm, w_in, seq):
    n, d = x.shape
    b = n // seq
    tm = min(TOKEN_TILE, seq)
    tps = seq // tm
    row = lambda i: (i, 0)
    pos = lambda i: (i % tps, 0)
    pos_t = lambda i: (0, i % tps)
    seq_t = lambda i: (i // tps, 0, i % tps)
    ta = min(ATTN_TILE, seq)
    key_blk = lambda i: (i // tps, i % tps, 0, 0)
    cos, sin = _rope_angles(seq)
    w_t = jnp.concatenate([w_in[:, :HALF], w_in[:, 2 * HALF:3 * HALF]], axis=1).T.astype(BF16)
    w_n = jnp.concatenate([w_in[:, HALF:2 * HALF], w_in[:, 3 * HALF:]], axis=1).astype(BF16)
    t_shape = jax.ShapeDtypeStruct((b, HALF, seq), BF16)
    return pl.pallas_call(
        _proj0_body,
        grid=(n // tm,),
        in_specs=[pl.BlockSpec((tm, d), row), _resident((1, d)), _resident((2 * HALF, d)),
                  _resident((d, 2 * HALF)),
                  pl.BlockSpec((tm, LANES), pos), pl.BlockSpec((tm, LANES), pos),
                  pl.BlockSpec((tm, LANES), pos),
                  pl.BlockSpec((ROT_DIM // 2, tm), pos_t), pl.BlockSpec((ROT_DIM // 2, tm), pos_t)],
        out_specs=[pl.BlockSpec((1, HALF, tm), seq_t), pl.BlockSpec((tm, HALF), row),
                   pl.BlockSpec((1, tm // ta, HALF, ta), key_blk), pl.BlockSpec((tm, HALF), row)],
        out_shape=[t_shape, jax.ShapeDtypeStruct((n, HALF), BF16),
                   jax.ShapeDtypeStruct((b, seq // ta, HALF, ta), BF16),
                   jax.ShapeDtypeStruct((n, HALF), F32)],
        compiler_params=_params("parallel"),
        name="proj0",
    )(x, norm.reshape(1, d), w_t, w_n, *_rope_lane_tables(seq), cos.T, sin.T)


def _diff_attn_body(lam_ref, subln_ref, qt_ref, k_ref, vt_ref, o_ref, m_ref, acc_ref, *, tile):
    i = pl.program_id(1)
    m_ref[...] = jnp.full_like(m_ref, -jnp.inf)
    acc_ref[...] = jnp.zeros_like(acc_ref)
    ones = jnp.ones((ONES_ROWS, tile), BF16)

    def sweep(j, diagonal):
        key0 = pl.multiple_of(j * tile, tile)
        if diagonal:
            query = lax.broadcasted_iota(jnp.int32, (tile, 2 * tile), 1)
            query = jnp.where(query >= tile, query - tile, query)
            keep = lax.broadcasted_iota(jnp.int32, (tile, 2 * tile), 0) <= query

        def scores(hd):
            rows = slice(hd * LANES, (hd + 1) * LANES)
            q_t = qt_ref[0, rows, :]
            q_both = jnp.concatenate([_keep_half_rows(q_t, 0), _keep_half_rows(q_t, 1)], axis=1)
            return jnp.dot(k_ref[0, pl.ds(key0, tile), rows], q_both, preferred_element_type=F32)

        def weights(hd, s):
            if diagonal:
                s = jnp.where(keep, s, -jnp.inf)
            m_old = m_ref[hd]
            m_new = jnp.maximum(m_old, jnp.max(s, axis=0, keepdims=True))
            m_ref[hd] = m_new
            return jnp.exp2(m_old - m_new), jnp.exp2(s - m_new).astype(BF16)

        def accumulate(hd, alpha, p):
            v_aug = jnp.concatenate([vt_ref[0, j, hd * LANES:(hd + 1) * LANES, :], ones], axis=0)
            acc_ref[hd] = alpha * acc_ref[hd] + jnp.dot(v_aug, p, preferred_element_type=F32)

        _staged(DIFF_HEADS, scores, weights, accumulate)

    def off_diagonal(j, carry):
        sweep(j, False)
        return carry

    lax.fori_loop(0, i, off_diagonal, 0)
    sweep(i, True)

    lam_v = lam_ref[...]
    lam = (jnp.exp(jnp.sum(lam_v[0:1] * lam_v[1:2], keepdims=True))
           - jnp.exp(jnp.sum(lam_v[2:3] * lam_v[3:4], keepdims=True)) + DIFF_LAMBDA_INIT)
    for hd in range(DIFF_HEADS):
        a1, a2 = acc_ref[hd, :, :tile], acc_ref[hd, :, tile:]
        o_t = (a1[:LANES] / a1[LANES:LANES + 1] - lam * (a2[:LANES] / a2[LANES:LANES + 1]))
        o_t = o_t * lax.rsqrt(jnp.mean(o_t * o_t, axis=0, keepdims=True) + NORM_EPS)
        o_t = o_t * subln_ref[...] * (1.0 - DIFF_LAMBDA_INIT)
        o_ref[0, :, hd * LANES:(hd + 1) * LANES] = o_t.T.astype(BF16)


def _diff_attention(q_t, k, v_t, lam_vecs, subln):
    b, s, _ = k.shape
    n_blk, _, t = v_t.shape[1:]
    const = lambda bi, i: (0, 0)
    return pl.pallas_call(
        functools.partial(_diff_attn_body, tile=t),
        grid=(b, n_blk),
        in_specs=[pl.BlockSpec((4, DIFF_HEAD_DIM), const), pl.BlockSpec((LANES, 1), const),
                  pl.BlockSpec((1, HALF, t), lambda bi, i: (bi, 0, i)),
                  pl.BlockSpec((1, s, HALF), lambda bi, i: (bi, 0, 0)),
                  pl.BlockSpec((1, n_blk, HALF, t), lambda bi, i: (bi, 0, 0, 0))],
        out_specs=pl.BlockSpec((1, t, HALF), lambda bi, i: (bi, i, 0)),
        out_shape=jax.ShapeDtypeStruct((b, s, HALF), BF16),
        scratch_shapes=[pltpu.VMEM((DIFF_HEADS, 1, 2 * t), F32),
                        pltpu.VMEM((DIFF_HEADS, LANES + ONES_ROWS, 2 * t), F32)],
        compiler_params=_params("parallel", "arbitrary"),
        name="diff_attn",
    )(lam_vecs, subln.reshape(LANES, 1), q_t, k, v_t)


def _s5_discretize_body(lr_ref, li_ref, ldt_ref, br_ref, bi_ref, k_ref, pr_ref, pi_ref, bbr_ref, bbi_ref):
    lr, li = lr_ref[...], li_ref[...]
    dt = jnp.exp(ldt_ref[...])
    mag = jnp.exp(lr * dt)
    ar, ai = mag * jnp.cos(li * dt), mag * jnp.sin(li * dt)
    den = lr * lr + li * li
    gr = ((ar - 1.0) * lr + ai * li) / den
    gi = (ai * lr - (ar - 1.0) * li) / den
    br, bi = br_ref[...], bi_ref[...]
    bbr_ref[...] = gr * br - gi * bi
    bbi_ref[...] = gr * bi + gi * br
    steps = k_ref[...]
    mag_k = jnp.exp(lr * dt * steps)
    pr_ref[...] = mag_k * jnp.cos(li * dt * steps)
    pi_ref[...] = mag_k * jnp.sin(li * dt * steps)


def _s5_discretize(a_re, a_im, log_dt, b_re, b_im):
    row = lambda a: a.reshape(1, S5_NSTATE)
    ldt = jnp.broadcast_to(log_dt[:, None], (S5_GROUPS, S5_STATE))
    b_t = lambda a: a.transpose(2, 0, 1).reshape(S5_GROUP, S5_NSTATE)
    steps = jnp.arange(1, S5_CHUNK + 1, dtype=F32).reshape(S5_CHUNK, 1)
    pshape = jax.ShapeDtypeStruct((S5_CHUNK, S5_NSTATE), F32)
    bshape = jax.ShapeDtypeStruct((S5_GROUP, S5_NSTATE), F32)
    return pl.pallas_call(
        _s5_discretize_body,
        out_shape=[pshape, pshape, bshape, bshape],
        name="s5_discretize",
    )(row(a_re), row(a_im), row(ldt), b_t(b_re), b_t(b_im), steps)


def _s5_body(u_ref, pr_ref, pi_ref, bbr_ref, bbi_ref, cr_ref, ci_ref, d_ref, wglu_ref, bglu_ref,
             o_ref, us_ref, ys_ref, xr_ref, xi_ref, hin_ref, carry_ref, *, tile):
    n_chunks = tile // S5_CHUNK
    slabs = HALF // LANES
    gw = S5_NSTATE // slabs

    @pl.when(pl.program_id(1) == 0)
    def _():
        carry_ref[...] = jnp.zeros_like(carry_ref)

    u = u_ref[0]
    for g in range(slabs):
        us_ref[g] = u[:, g * LANES:(g + 1) * LANES]
    u_perm = jnp.concatenate(
        [jnp.concatenate([us_ref[g, pl.ds(st, n_chunks, stride=S5_CHUNK), :] for g in range(slabs)], axis=1)
         for st in range(S5_CHUNK)], axis=0).astype(BF16)

    for g in range(slabs):
        ug = u_perm[:, g * LANES:(g + 1) * LANES]
        xr_ref[:, g * gw:(g + 1) * gw] = jnp.dot(ug, bbr_ref[g], preferred_element_type=F32)
        xi_ref[:, g * gw:(g + 1) * gw] = jnp.dot(ug, bbi_ref[g], preferred_element_type=F32)

    ar, ai = pr_ref[0:1, :], pi_ref[0:1, :]
    hr, hi = xr_ref[0:n_chunks, :], xi_ref[0:n_chunks, :]
    for st in range(1, S5_CHUNK):
        rows = slice(st * n_chunks, (st + 1) * n_chunks)
        hr, hi = ar * hr - ai * hi + xr_ref[rows, :], ar * hi + ai * hr + xi_ref[rows, :]
        xr_ref[rows, :] = hr
        xi_ref[rows, :] = hi

    last = (S5_CHUNK - 1) * n_chunks
    ac_r, ac_i = pr_ref[S5_CHUNK - 1:S5_CHUNK, :], pi_ref[S5_CHUNK - 1:S5_CHUNK, :]

    def carry_step(c, carry):
        cr, ci = carry
        hin_ref[0, pl.ds(c, 1), :] = cr
        hin_ref[1, pl.ds(c, 1), :] = ci
        zr, zi = xr_ref[pl.ds(last + c, 1), :], xi_ref[pl.ds(last + c, 1), :]
        return ac_r * cr - ac_i * ci + zr, ac_r * ci + ac_i * cr + zi

    cr, ci = lax.fori_loop(0, n_chunks, carry_step, (carry_ref[0:1, :], carry_ref[1:2, :]), unroll=4)
    carry_ref[0:1, :] = cr
    carry_ref[1:2, :] = ci

    hin_r, hin_i = hin_ref[0], hin_ref[1]
    for st in range(S5_CHUNK):
        rows = slice(st * n_chunks, (st + 1) * n_chunks)
        p_r, p_i = pr_ref[st:st + 1, :], pi_ref[st:st + 1, :]
        xr_ref[rows, :] = xr_ref[rows, :] + (p_r * hin_r - p_i * hin_i)
        xi_ref[rows, :] = xi_ref[rows, :] + (p_r * hin_i + p_i * hin_r)
    for g in range(slabs):
        cols = slice(g * gw, (g + 1) * gw)
        yg = (jnp.dot(xr_ref[:, cols].astype(BF16), cr_ref[g], preferred_element_type=F32)
              - jnp.dot(xi_ref[:, cols].astype(BF16), ci_ref[g], preferred_element_type=F32))
        for st in range(S5_CHUNK):
            ys_ref[g, pl.ds(st, n_chunks, stride=S5_CHUNK), :] = yg[st * n_chunks:(st + 1) * n_chunks]

    y = jnp.concatenate([ys_ref[g] for g in range(slabs)], axis=1) + d_ref[...] * u
    z = jax.nn.gelu(y, approximate=True)
    gate = jnp.dot(z.astype(BF16), wglu_ref[...], preferred_element_type=F32) + bglu_ref[...]
    o_ref[0] = (z * jax.nn.sigmoid(gate)).astype(BF16)


def _s5(u, a_re, a_im, log_dt, b_re, b_im, c_re, c_im, d_skip, w_glu, b_glu):
    b, s, w = u.shape
    t = min(S5_TILE, s)
    slabs = w // LANES
    per = S5_GROUPS // slabs
    gw = S5_NSTATE // slabs
    pw_r, pw_i, bbr, bbi = _s5_discretize(a_re, a_im, log_dt, b_re, b_im)
    eye = jnp.eye(per, dtype=F32)

    def block_in(m):
        m = m.reshape(S5_GROUP, slabs, per, S5_STATE)
        return jnp.einsum('csgp,gh->sgchp', m, eye).reshape(slabs, LANES, gw).astype(BF16)

    def block_out(m):
        m = m.reshape(slabs, per, S5_GROUP, S5_STATE)
        return jnp.einsum('sgcp,gh->sgphc', m, eye).reshape(slabs, gw, LANES).astype(BF16)

    seq_blk = lambda bi, ti: (bi, ti, 0)
    n_chunks = t // S5_CHUNK
    return pl.pallas_call(
        functools.partial(_s5_body, tile=t),
        grid=(b, s // t),
        in_specs=[pl.BlockSpec((1, t, w), seq_blk),
                  _resident((S5_CHUNK, S5_NSTATE)), _resident((S5_CHUNK, S5_NSTATE)),
                  _resident((slabs, LANES, gw)), _resident((slabs, LANES, gw)),
                  _resident((slabs, gw, LANES)), _resident((slabs, gw, LANES)),
                  _resident((1, w)), _resident((w, w)), _resident((1, w))],
        out_specs=pl.BlockSpec((1, t, w), seq_blk),
        out_shape=jax.ShapeDtypeStruct((b, s, w), BF16),
        scratch_shapes=[pltpu.VMEM((slabs, t, LANES), F32), pltpu.VMEM((slabs, t, LANES), F32),
                        pltpu.VMEM((t, S5_NSTATE), F32), pltpu.VMEM((t, S5_NSTATE), F32),
                        pltpu.VMEM((2, n_chunks, S5_NSTATE), F32), pltpu.VMEM((2, S5_NSTATE), F32)],
        compiler_params=_params("parallel", "arbitrary"),
        name="s5",
    )(u, pw_r, pw_i, block_in(bbr), block_in(bbi),
      block_out(c_re), block_out(c_im), d_skip.reshape(1, w), w_glu.astype(BF16), b_glu.reshape(1, w))


def _proj1_body(x_ref, g_ref, wt_ref, wm_ref, wo_ref, wk_ref, wg_ref,
                qk_ref, vm_ref, om_ref, qft_ref, kf_ref, vft_ref, gt_ref):
    h = _rms(x_ref[...], g_ref[...]).astype(BF16)
    pt = lax.dot_general(wt_ref[...], h, NT_DIMS, preferred_element_type=F32)
    qft_ref[0] = (pt[:HALF] * (FOX_HEAD_DIM ** -0.5 * LOG2_E)).astype(BF16)
    _store_key_blocks(vft_ref, pt[HALF:].astype(BF16))
    pm = jnp.dot(h, wm_ref[...], preferred_element_type=F32)
    qk_ref[...] = pm[:, 0:HALF]
    vm_ref[...] = pm[:, HALF:2 * HALF].astype(BF16)
    om_ref[...] = jnp.dot(h, wo_ref[...], preferred_element_type=F32)
    kf_ref[...] = jnp.dot(h, wk_ref[...], preferred_element_type=F32).astype(BF16)
    gt_ref[...] = jnp.dot(h, wg_ref[...], preferred_element_type=F32)


def _proj1(x, norm, w_in, seq):
    n, d = x.shape
    b = n // seq
    tm = min(TOKEN_TILE, seq)
    tps = seq // tm
    row = lambda i: (i, 0)
    seq_t = lambda i: (i // tps, 0, i % tps)
    ta = min(ATTN_TILE, seq)
    key_blk = lambda i: (i // tps, i % tps, 0, 0)
    o = 2 * HALF
    g = 2 * MLSTM_HEADS
    om = o + g
    w_m = w_in[:, :o].astype(BF16)
    w_o = w_in[:, om:om + HALF].astype(BF16)
    w_k = w_in[:, om + 2 * HALF:om + 3 * HALF].astype(BF16)
    w_g = jnp.concatenate([w_in[:, o:om], w_in[:, om + 4 * HALF:],
                           jnp.zeros((d, GATE_COLS - g - FOX_HEADS), w_in.dtype)], axis=1).astype(BF16)
    w_t = jnp.concatenate([w_in[:, om + HALF:om + 2 * HALF], w_in[:, om + 3 * HALF:om + 4 * HALF]],
                          axis=1).T.astype(BF16)
    f32_half = jax.ShapeDtypeStruct((n, HALF), F32)
    bf_half = jax.ShapeDtypeStruct((n, HALF), BF16)
    t_shape = jax.ShapeDtypeStruct((b, HALF, seq), BF16)
    half_spec = pl.BlockSpec((tm, HALF), row)
    t_spec = pl.BlockSpec((1, HALF, tm), seq_t)
    return pl.pallas_call(
        _proj1_body,
        grid=(n // tm,),
        in_specs=[pl.BlockSpec((tm, d), row), _resident((1, d)), _resident((2 * HALF, d)),
                  _resident((d, 2 * HALF)), _resident((d, HALF)), _resident((d, HALF)),
                  _resident((d, GATE_COLS))],
        out_specs=[half_spec, half_spec, half_spec, t_spec, half_spec,
                   pl.BlockSpec((1, tm // ta, HALF, ta), key_blk), pl.BlockSpec((tm, GATE_COLS), row)],
        out_shape=[f32_half, bf_half, f32_half, t_shape, bf_half,
                   jax.ShapeDtypeStruct((b, seq // ta, HALF, ta), BF16),
                   jax.ShapeDtypeStruct((n, GATE_COLS), F32)],
        compiler_params=_params("parallel"),
        name="proj1",
    )(x, norm.reshape(1, d), w_t, w_m, w_o, w_k, w_g)


def _mlstm_body(qk_ref, v_ref, og_ref, gt_ref, cw_ref, cb_ref, gb_ref, on_ref, out_ref,
                xbuf_ref, cn_ref, m_ref, *, chunk):
    pad = SUBLANES
    dk, dv = MLSTM_QK_DIM, MLSTM_V_DIM

    @pl.when(pl.program_id(1) == 0)
    def _():
        xbuf_ref[0:pad, :] = jnp.zeros((pad, HALF), F32)
        cn_ref[...] = jnp.zeros_like(cn_ref)
        m_ref[...] = jnp.zeros_like(m_ref)

    x = qk_ref[0]
    xbuf_ref[pad:pad + chunk, :] = x
    a = jnp.zeros_like(x) + cb_ref[...]
    for tap in range(MLSTM_CONV):
        a = a + cw_ref[tap:tap + 1, :] * xbuf_ref[pl.ds(pad - (MLSTM_CONV - 1) + tap, chunk), :]
    xbuf_ref[0:pad, :] = x[chunk - pad:chunk, :]
    a = a * jax.nn.sigmoid(a)
    q = a[:, :MLSTM_HEADS * dk] * dk ** -0.5
    k = a[:, MLSTM_HEADS * dk:]
    k_t = k.T

    gi = gt_ref[0] + gb_ref[...]
    bcum = _cumsum_rows(_log_sigmoid(gi))
    gi_t = gi.T
    bcum_t = bcum.T
    tri = (lax.broadcasted_iota(jnp.int32, (chunk, chunk), 1)
           <= lax.broadcasted_iota(jnp.int32, (chunk, chunk), 0))
    lane = lax.broadcasted_iota(jnp.int32, (1, LANES), 1)
    v_all = v_ref[0]
    o_gate = og_ref[0]
    ones = jnp.ones((chunk, dv), BF16)

    states = [cn_ref[pair] for pair in range(MLSTM_HEADS // 2)]

    def products(hd):
        pair, odd = hd // 2, hd % 2
        sel = (lane >= dk) if odd else (lane < dk)
        q2 = jnp.where(sel, q[:, pair * LANES:(pair + 1) * LANES], 0.0).astype(BF16)
        k2 = k[:, pair * LANES:(pair + 1) * LANES].astype(BF16)
        qk = lax.dot_general(q2, k2, NT_DIMS, preferred_element_type=F32)
        return qk, jnp.dot(q2, states[pair].astype(BF16), preferred_element_type=F32)

    def gate_weights(hd, prods):
        qk, q_state = prods
        f_lane = MLSTM_HEADS + hd
        b_col = bcum[:, f_lane:f_lane + 1]
        b_row = bcum_t[f_lane:f_lane + 1, :]
        i_row = gi_t[hd:hd + 1, :]
        m_prev = m_ref[hd]
        dm = jnp.where(tri, b_col - b_row + i_row, -jnp.inf)
        g = b_col + m_prev
        mt = jnp.maximum(g, jnp.max(dm, axis=-1, keepdims=True))
        s = (qk * jnp.exp(dm - mt)).astype(BF16)
        w_inter = jnp.exp(g - mt)

        b_last = bcum[chunk - 1:chunk, f_lane:f_lane + 1]
        dec = b_last - b_row + i_row
        m_new = jnp.maximum(b_last + m_prev, jnp.max(dec, axis=-1, keepdims=True))
        k_w = (k_t[hd * dk:(hd + 1) * dk, :] * jnp.exp(dec - m_new)).astype(BF16)
        carry_w = jnp.exp(b_last + m_prev - m_new)
        m_ref[hd] = m_new
        return s, k_w, q_state, w_inter, mt, carry_w

    def outputs(hd, s, k_w, q_state, w_inter, mt, carry_w):
        pair, odd = hd // 2, hd % 2
        v_aug = jnp.concatenate([v_all[:, hd * dv:(hd + 1) * dv], ones], axis=1)
        s_v = jnp.dot(s, v_aug, preferred_element_type=F32)
        num = w_inter * q_state[:, :dv] + s_v[:, :dv]
        den = w_inter * q_state[:, dv:] + s_v[:, dv:]
        h_out = num / jnp.maximum(jnp.abs(den), jnp.exp(-mt))
        rows = slice(odd * dk, (odd + 1) * dk)
        cn_ref[pair, rows, :] = (carry_w * states[pair][rows, :]
                                 + jnp.dot(k_w, v_aug, preferred_element_type=F32))
        cols = slice(hd * dv, (hd + 1) * dv)
        h_norm = _rms(h_out, on_ref[hd:hd + 1, :])
        out_ref[0, :, cols] = (jax.nn.sigmoid(o_gate[:, cols]) * h_norm).astype(BF16)

    _staged(MLSTM_HEADS, products, gate_weights, outputs)


def _mlstm(qk, v, o_gate, gates, conv_w, conv_b, b_i, b_f, out_norm):
    b, s, _ = qk.shape
    c = min(MLSTM_CHUNK, s)
    gate_bias = jnp.concatenate([b_i, b_f, jnp.zeros((GATE_COLS - 2 * MLSTM_HEADS,), F32)]).reshape(1, GATE_COLS)
    seq_blk = lambda bi, ci: (bi, ci, 0)
    half_spec = pl.BlockSpec((1, c, HALF), seq_blk)
    return pl.pallas_call(
        functools.partial(_mlstm_body, chunk=c),
        grid=(b, s // c),
        in_specs=[half_spec, half_spec, half_spec, pl.BlockSpec((1, c, GATE_COLS), seq_blk),
                  _resident((MLSTM_CONV, HALF)), _resident((1, HALF)), _resident((1, GATE_COLS)),
                  _resident((MLSTM_HEADS, MLSTM_V_DIM))],
        out_specs=half_spec,
        out_shape=jax.ShapeDtypeStruct((b, s, HALF), BF16),
        scratch_shapes=[pltpu.VMEM((c + SUBLANES, HALF), F32),
                        pltpu.VMEM((MLSTM_HEADS // 2, 2 * MLSTM_QK_DIM, 2 * MLSTM_V_DIM), F32),
                        pltpu.VMEM((MLSTM_HEADS, 1, 1), F32)],
        compiler_params=_params("parallel", "arbitrary"),
        name="mlstm",
    )(qk, v, o_gate, gates, conv_w, conv_b.reshape(1, HALF), gate_bias, out_norm)


FOX_GATE_LANE = 2 * MLSTM_HEADS
F_TERMS = 3


def _fox_cumsum_body(gt_ref, gb_ref, sel_ref, frow_ref, fk_ref):
    f = _cumsum_rows(_log_sigmoid(gt_ref[0] + gb_ref[...])) * LOG2_E
    hi = f.astype(BF16)
    rest = f - hi.astype(F32)
    mid = rest.astype(BF16)
    lo = (rest - mid.astype(F32)).astype(BF16)
    terms = jnp.concatenate([hi, mid, lo], axis=1)
    fk_ref[0] = jnp.dot(terms, sel_ref[...], preferred_element_type=F32).astype(BF16)
    frow_ref[0] = f.T[FOX_GATE_LANE:FOX_GATE_LANE + FOX_HEADS, :]


def _fox_cumsum(gates, fox_b_f):
    b, s, _ = gates.shape
    bias = jnp.concatenate([jnp.zeros((FOX_GATE_LANE,), F32), fox_b_f,
                            jnp.zeros((GATE_COLS - FOX_GATE_LANE - FOX_HEADS,), F32)]).reshape(1, GATE_COLS)
    src, dst = [], []
    for head in range(FOX_HEADS):
        for term in range(F_TERMS):
            src.append(term * GATE_COLS + FOX_GATE_LANE + head)
            dst.append((head // 2) * LANES + F_TERMS * (head % 2) + term)
    sel = jnp.zeros((F_TERMS * GATE_COLS, HALF), BF16).at[jnp.asarray(src), jnp.asarray(dst)].set(-1.0)
    return pl.pallas_call(
        _fox_cumsum_body,
        grid=(b,),
        in_specs=[pl.BlockSpec((1, s, GATE_COLS), lambda bi: (bi, 0, 0)), _resident((1, GATE_COLS)),
                  _resident((F_TERMS * GATE_COLS, HALF))],
        out_specs=[pl.BlockSpec((1, FOX_HEADS, s), lambda bi: (bi, 0, 0)),
                   pl.BlockSpec((1, s, HALF), lambda bi: (bi, 0, 0))],
        out_shape=[jax.ShapeDtypeStruct((b, FOX_HEADS, s), F32),
                   jax.ShapeDtypeStruct((b, s, HALF), BF16)],
        compiler_params=_params("parallel"),
        name="fox_cumsum",
    )(gates, bias, sel)


def _fox_attn_body(qt_ref, k_ref, fk_ref, vt_ref, frow_ref, o_ref, m_ref, acc_ref, *, tile):
    i = pl.program_id(1)
    dim = lax.broadcasted_iota(jnp.int32, (LANES, 1), 0)
    dh = FOX_HEAD_DIM
    m_ref[...] = jnp.full_like(m_ref, -jnp.inf)
    acc_ref[...] = jnp.zeros_like(acc_ref)
    ones = jnp.ones((ONES_ROWS, tile), BF16)

    def sweep(j, diagonal):
        key0 = pl.multiple_of(j * tile, tile)
        if diagonal:
            query = lax.broadcasted_iota(jnp.int32, (tile, 2 * tile), 1)
            query = jnp.where(query >= tile, query - tile, query)
            keep = lax.broadcasted_iota(jnp.int32, (tile, 2 * tile), 0) <= query

        def scores(pair):
            rows = slice(pair * LANES, (pair + 1) * LANES)
            q_t = qt_ref[0, rows, :]
            k_aug = jnp.concatenate([k_ref[0, pl.ds(key0, tile), rows],
                                     fk_ref[0, pl.ds(key0, tile), rows]], axis=1)
            q_aug = []
            for c in range(2):
                ones_rows = jnp.logical_and(dim >= F_TERMS * c, dim < F_TERMS * (c + 1))
                q_aug.append(jnp.concatenate(
                    [_keep_half_rows(q_t, c),
                     jnp.broadcast_to(jnp.where(ones_rows, 1.0, 0.0), q_t.shape).astype(BF16)], axis=0))
            return jnp.dot(k_aug, jnp.concatenate(q_aug, axis=1), preferred_element_type=F32)

        def weights(pair, s):
            s = s + jnp.concatenate([frow_ref[0, 2 * pair:2 * pair + 1, :],
                                     frow_ref[0, 2 * pair + 1:2 * pair + 2, :]], axis=1)
            if diagonal:
                s = jnp.where(keep, s, -jnp.inf)
            m_old = m_ref[pair]
            m_new = jnp.maximum(m_old, jnp.max(s, axis=0, keepdims=True))
            m_ref[pair] = m_new
            return jnp.exp2(m_old - m_new), jnp.exp2(s - m_new).astype(BF16)

        def accumulate(pair, alpha, p):
            for c in range(2):
                hd = 2 * pair + c
                cols = slice(c * tile, (c + 1) * tile)
                v_aug = jnp.concatenate([vt_ref[0, j, hd * dh:(hd + 1) * dh, :], ones], axis=0)
                acc_ref[hd] = alpha[:, cols] * acc_ref[hd] + jnp.dot(v_aug, p[:, cols], preferred_element_type=F32)

        _staged(FOX_HEADS // 2, scores, weights, accumulate)

    def off_diagonal(j, carry):
        sweep(j, False)
        return carry

    lax.fori_loop(0, i, off_diagonal, 0)
    sweep(i, True)

    for pair in range(FOX_HEADS // 2):
        a0, a1 = acc_ref[2 * pair], acc_ref[2 * pair + 1]
        o_t = jnp.concatenate([a0[:dh] / a0[dh:dh + 1], a1[:dh] / a1[dh:dh + 1]], axis=0)
        o_ref[0, :, pair * LANES:(pair + 1) * LANES] = o_t.T.astype(BF16)


def _fox_attention(q_t, k, fk, v_t, frow):
    b, s, _ = k.shape
    n_blk, _, t = v_t.shape[1:]
    whole = lambda bi, i: (bi, 0, 0)
    return pl.pallas_call(
        functools.partial(_fox_attn_body, tile=t),
        grid=(b, n_blk),
        in_specs=[pl.BlockSpec((1, HALF, t), lambda bi, i: (bi, 0, i)),
                  pl.BlockSpec((1, s, HALF), whole), pl.BlockSpec((1, s, HALF), whole),
                  pl.BlockSpec((1, n_blk, HALF, t), lambda bi, i: (bi, 0, 0, 0)),
                  pl.BlockSpec((1, FOX_HEADS, t), lambda bi, i: (bi, 0, i))],
        out_specs=pl.BlockSpec((1, t, HALF), lambda bi, i: (bi, i, 0)),
        out_shape=jax.ShapeDtypeStruct((b, s, HALF), BF16),
        scratch_shapes=[pltpu.VMEM((FOX_HEADS // 2, 1, 2 * t), F32),
                        pltpu.VMEM((FOX_HEADS, FOX_HEAD_DIM + ONES_ROWS, t), F32)],
        compiler_params=_params("parallel", "arbitrary"),
        name="fox_attn",
    )(q_t, k, fk, v_t, frow)


def kernel(x, l0_ffn1_norm, l0_ffn1_w_gate, l0_ffn1_w_up, l0_ffn1_w_down, l0_mix_norm, l0_w_in, l0_diff_lambda_q1, l0_diff_lambda_k1, l0_diff_lambda_q2, l0_diff_lambda_k2, l0_diff_subln, l0_s5_a_re, l0_s5_a_im, l0_s5_log_dt, l0_s5_b_re, l0_s5_b_im, l0_s5_c_re, l0_s5_c_im, l0_s5_d, l0_s5_w_glu, l0_s5_b_glu, l0_w_out, l0_ffn2_norm, l0_ffn2_w_gate, l0_ffn2_w_up, l0_ffn2_w_down, l1_ffn1_norm, l1_ffn1_w_gate, l1_ffn1_w_up, l1_ffn1_w_down, l1_mix_norm, l1_w_in, l1_mlstm_conv_w, l1_mlstm_conv_b, l1_mlstm_b_i, l1_mlstm_b_f, l1_mlstm_out_norm, l1_fox_b_f, l1_w_out, l1_ffn2_norm, l1_ffn2_w_gate, l1_ffn2_w_up, l1_ffn2_w_down, final_norm):
    b, s, d = x.shape
    n = b * s
    seq = lambda a: a.reshape(b, s, a.shape[-1])
    flat = lambda a: a.reshape(n, a.shape[-1])

    x = _ffn(x.reshape(n, d), l0_ffn1_norm, l0_ffn1_w_gate, l0_ffn1_w_up, l0_ffn1_w_down)
    q_t, k, v_t, u = _proj0(x, l0_mix_norm, l0_w_in, s)
    lam_vecs = jnp.stack([l0_diff_lambda_q1, l0_diff_lambda_k1, l0_diff_lambda_q2, l0_diff_lambda_k2])
    ya = _diff_attention(q_t, seq(k), v_t, lam_vecs, l0_diff_subln)
    yb = _s5(seq(u), l0_s5_a_re, l0_s5_a_im, l0_s5_log_dt, l0_s5_b_re, l0_s5_b_im,
             l0_s5_c_re, l0_s5_c_im, l0_s5_d, l0_s5_w_glu, l0_s5_b_glu)
    x = _ffn(x, l0_ffn2_norm, l0_ffn2_w_gate, l0_ffn2_w_up, l0_ffn2_w_down,
             mix=(flat(ya), flat(yb), l0_w_out[:HALF], l0_w_out[HALF:]))

    x = _ffn(x, l1_ffn1_norm, l1_ffn1_w_gate, l1_ffn1_w_up, l1_ffn1_w_down)
    qk_m, v_m, o_m, qf_t, k_f, vf_t, gates = _proj1(x, l1_mix_norm, l1_w_in, s)
    hm = _mlstm(seq(qk_m), seq(v_m), seq(o_m), seq(gates), l1_mlstm_conv_w, l1_mlstm_conv_b,
                l1_mlstm_b_i, l1_mlstm_b_f, l1_mlstm_out_norm)
    frow, fk = _fox_cumsum(seq(gates), l1_fox_b_f)
    hf = _fox_attention(qf_t, seq(k_f), fk, vf_t, frow)
    x = _ffn(x, l1_ffn2_norm, l1_ffn2_w_gate, l1_ffn2_w_up, l1_ffn2_w_down,
             mix=(flat(hm), flat(hf), l1_w_out[:HALF], l1_w_out[HALF:]), final_norm=final_norm)
    return x.reshape(b, s, d)
```

```python
import functools
import math

import jax
import jax.numpy as jnp
import numpy as np
from jax import lax
from jax.experimental import pallas as pl
from jax.experimental.pallas import tpu as pltpu

F32 = jnp.float32
BF16 = jnp.bfloat16

D_MODEL = 1024
D_FF = 2816
NORM_EPS = 1e-6
ROPE_THETA = 500000.0
DIFF_HEADS = 4
DIFF_HEAD_DIM = 64
ROT_DIM = DIFF_HEAD_DIM // 4
DIFF_LAMBDA_INIT = 0.8 - 0.6 * math.exp(-0.3 * 0)
S5_GROUPS = 32
S5_GROUP = 16
S5_STATE = 64
S5_NSTATE = S5_GROUPS * S5_STATE
MLSTM_HEADS = 4
MLSTM_QK_DIM = 64
MLSTM_V_DIM = 128
MLSTM_CONV = 4
FOX_HEADS = 8
FOX_HEAD_DIM = 64
HALF = 512

LANES = 128
SUBLANES = 8
VMEM_LIMIT_BYTES = 60 * 1024 * 1024

TOKEN_TILE = 1024
FFN_TILE = 512
FF_CHUNK = 256
ATTN_TILE = 512
S5_TILE = 1024
S5_CHUNK = 16
MLSTM_CHUNK = 512
GATE_COLS = LANES

NT_DIMS = (((1,), (1,)), ((), ()))
LOG2_E = math.log2(math.e)
ONES_ROWS = 16


def _params(*semantics):
    return pltpu.CompilerParams(dimension_semantics=semantics, vmem_limit_bytes=VMEM_LIMIT_BYTES)


def _resident(shape):
    return pl.BlockSpec(shape, lambda *_: (0,) * len(shape), pipeline_mode=pl.Buffered(1))


def _rms(x, g):
    return x * lax.rsqrt(jnp.mean(x * x, axis=-1, keepdims=True) + NORM_EPS) * g


def _log_sigmoid(x):
    return jnp.minimum(x, 0.0) - jnp.log1p(jnp.exp(-jnp.abs(x)))


def _cumsum_rows(x):
    n = x.shape[0]
    row = lax.broadcasted_iota(jnp.int32, x.shape, 0)
    k = 1
    while k < n:
        x = x + jnp.where(row >= k, pltpu.roll(x, k, 0), 0.0)
        k *= 2
    return x


def _keep_half_rows(x, half):
    rows = x.shape[0] // 2
    zeros = jnp.zeros((rows,) + x.shape[1:], x.dtype)
    return jnp.concatenate([x[:rows], zeros] if half == 0 else [zeros, x[rows:]], axis=0)


def _staged(n, scores, weights, accumulate):
    s, w = {}, {}
    for step in range(n + 2):
        if step < n:
            s[step] = scores(step)
        if step >= 2:
            accumulate(step - 2, *w.pop(step - 2))
        if 1 <= step <= n:
            w[step - 1] = weights(step - 1, s.pop(step - 1))


def _store_key_blocks(ref, value):
    width = ref.shape[3]
    for blk in range(ref.shape[1]):
        ref[0, blk] = value[:, blk * width:(blk + 1) * width]


def _ffn_body(*refs, has_mix, has_final):
    refs = list(refs)
    x_ref = refs.pop(0)
    if has_mix:
        a_ref, b_ref, wa_ref, wb_ref = refs[:4]
        refs = refs[4:]
    g_ref, wg_ref, wu_ref, wd_ref = refs[:4]
    refs = refs[4:]
    if has_final:
        gf_ref = refs.pop(0)
    o_ref = refs.pop(0)

    x = x_ref[...]
    if has_mix:
        x = x + jnp.dot(a_ref[...], wa_ref[...], preferred_element_type=F32)
        x = x + jnp.dot(b_ref[...], wb_ref[...], preferred_element_type=F32)
    h = _rms(x, g_ref[...]).astype(BF16)
    acc = jnp.zeros_like(x)
    for c in range(D_FF // FF_CHUNK):
        sl = slice(c * FF_CHUNK, (c + 1) * FF_CHUNK)
        gate = jnp.dot(h, wg_ref[:, sl].astype(BF16), preferred_element_type=F32)
        up = jnp.dot(h, wu_ref[:, sl].astype(BF16), preferred_element_type=F32)
        act = (gate * jax.nn.sigmoid(gate) * up).astype(BF16)
        acc = acc + jnp.dot(act, wd_ref[sl, :].astype(BF16), preferred_element_type=F32)
    y = x + 0.5 * acc
    if has_final:
        y = _rms(y, gf_ref[...])
    o_ref[...] = y


def _ffn(x, norm, wg, wu, wd, mix=None, final_norm=None):
    n, d = x.shape
    tm = min(FFN_TILE, n)
    row = lambda i: (i, 0)
    args = [x]
    specs = [pl.BlockSpec((tm, d), row)]
    if mix is not None:
        a, b, wa, wb = mix
        args += [a, b, wa.astype(BF16), wb.astype(BF16)]
        specs += [pl.BlockSpec((tm, HALF), row), pl.BlockSpec((tm, HALF), row),
                  _resident((HALF, d)), _resident((HALF, d))]
    args += [norm.reshape(1, d), wg, wu, wd]
    specs += [_resident((1, d)), _resident((d, D_FF)), _resident((d, D_FF)), _resident((D_FF, d))]
    if final_norm is not None:
        args.append(final_norm.reshape(1, d))
        specs.append(_resident((1, d)))
    body = functools.partial(_ffn_body, has_mix=mix is not None, has_final=final_norm is not None)
    return pl.pallas_call(
        body,
        grid=(n // tm,),
        in_specs=specs,
        out_specs=pl.BlockSpec((tm, d), row),
        out_shape=jax.ShapeDtypeStruct((n, d), F32),
        compiler_params=_params("parallel"),
        name="ffn",
    )(*args)


def _rope_angles(seq):
    pos = np.arange(seq, dtype=np.float32)
    inv = np.float32(ROPE_THETA) ** (-np.arange(0, ROT_DIM, 2, dtype=np.float32) / np.float32(ROT_DIM))
    ang = pos[:, None] * inv[None, :]
    return np.cos(ang), np.sin(ang)


def _rope_lane_tables(seq):
    half = ROT_DIM // 2
    cos, sin = _rope_angles(seq)
    one = np.ones((seq, DIFF_HEAD_DIM - ROT_DIM), np.float32)
    zero8 = np.zeros((seq, half), np.float32)
    zero48 = np.zeros((seq, DIFF_HEAD_DIM - ROT_DIM), np.float32)
    cos_t = np.concatenate([cos, cos, one] * 2, axis=1)
    sa_t = np.concatenate([-sin, zero8, zero48] * 2, axis=1)
    sb_t = np.concatenate([zero8, sin, zero48] * 2, axis=1)
    return cos_t, sa_t, sb_t


def _proj0_body(x_ref, g_ref, wt_ref, wk_ref, wu_ref, cos_ref, sa_ref, sb_ref, cosr_ref, sinr_ref,
                qt_ref, k_ref, vt_ref, u_ref):
    h = _rms(x_ref[...], g_ref[...]).astype(BF16)
    pt = lax.dot_general(wt_ref[...], h, NT_DIMS, preferred_element_type=F32)
    pk = jnp.dot(h, wk_ref[...].astype(BF16), preferred_element_type=F32)
    half = ROT_DIM // 2
    scale = DIFF_HEAD_DIM ** -0.5 * LOG2_E

    cos_r, sin_r = cosr_ref[...], sinr_ref[...]
    for comp in range(2 * DIFF_HEADS):
        r0 = comp * DIFF_HEAD_DIM
        x1, x2 = pt[r0:r0 + half], pt[r0 + half:r0 + ROT_DIM]
        qt_ref[0, r0:r0 + half, :] = ((x1 * cos_r - x2 * sin_r) * scale).astype(BF16)
        qt_ref[0, r0 + half:r0 + ROT_DIM, :] = ((x2 * cos_r + x1 * sin_r) * scale).astype(BF16)
        qt_ref[0, r0 + ROT_DIM:r0 + DIFF_HEAD_DIM, :] = (pt[r0 + ROT_DIM:r0 + DIFF_HEAD_DIM] * scale).astype(BF16)
    _store_key_blocks(vt_ref, pt[HALF:2 * HALF].astype(BF16))

    cos_t, sa_t, sb_t = cos_ref[...], sa_ref[...], sb_ref[...]
    for hd in range(DIFF_HEADS):
        sl = slice(hd * LANES, (hd + 1) * LANES)
        t = pk[:, sl]
        k_ref[:, sl] = (t * cos_t + pltpu.roll(t, LANES - half, 1) * sa_t
                        + pltpu.roll(t, half, 1) * sb_t).astype(BF16)
    u_ref[...] = jnp.dot(h, wu_ref[...].astype(BF16), preferred_element_type=F32)


def _proj0(x, norm, w_in, seq):
    n, d = x.shape
    b = n // seq
    tm = min(TOKEN_TILE, seq)
    tps = seq // tm
    row = lambda i: (i, 0)
    pos = lambda i: (i % tps, 0)
    pos_t = lambda i: (0, i % tps)
    seq_t = lambda i: (i // tps, 0, i % tps)
    ta = min(ATTN_TILE, seq)
    key_blk = lambda i: (i // tps, i % tps, 0, 0)
    cos, sin = _rope_angles(seq)
    w_t = jnp.concatenate([w_in[:, :HALF], w_in[:, 2 * HALF:3 * HALF]], axis=1).T.astype(BF16)
    w_cols = lambda blk: pl.BlockSpec((d, HALF), lambda i: (0, blk), pipeline_mode=pl.Buffered(1))
    t_shape = jax.ShapeDtypeStruct((b, HALF, seq), BF16)
    return pl.pallas_call(
        _proj0_body,
        grid=(n // tm,),
        in_specs=[pl.BlockSpec((tm, d), row), _resident((1, d)), _resident((2 * HALF, d)),
                  w_cols(1), w_cols(3),
                  pl.BlockSpec((tm, LANES), pos), pl.BlockSpec((tm, LANES), pos),
                  pl.BlockSpec((tm, LANES), pos),
                  pl.BlockSpec((ROT_DIM // 2, tm), pos_t), pl.BlockSpec((ROT_DIM // 2, tm), pos_t)],
        out_specs=[pl.BlockSpec((1, HALF, tm), seq_t), pl.BlockSpec((tm, HALF), row),
                   pl.BlockSpec((1, tm // ta, HALF, ta), key_blk), pl.BlockSpec((tm, HALF), row)],
        out_shape=[t_shape, jax.ShapeDtypeStruct((n, HALF), BF16),
                   jax.ShapeDtypeStruct((b, seq // ta, HALF, ta), BF16),
                   jax.ShapeDtypeStruct((n, HALF), F32)],
        compiler_params=_params("parallel"),
        name="proj0",
    )(x, norm.reshape(1, d), w_t, w_in, w_in, *_rope_lane_tables(seq), cos.T, sin.T)


def _diff_attn_body(lam_ref, subln_ref, qt_ref, k_ref, vt_ref, o_ref, m_ref, acc_ref, *, tile):
    i = pl.program_id(1)
    m_ref[...] = jnp.full_like(m_ref, -jnp.inf)
    acc_ref[...] = jnp.zeros_like(acc_ref)
    ones = jnp.ones((ONES_ROWS, tile), BF16)

    def sweep(j, diagonal):
        key0 = pl.multiple_of(j * tile, tile)
        if diagonal:
            query = lax.broadcasted_iota(jnp.int32, (tile, 2 * tile), 1)
            query = jnp.where(query >= tile, query - tile, query)
            keep = lax.broadcasted_iota(jnp.int32, (tile, 2 * tile), 0) <= query

        def scores(hd):
            rows = slice(hd * LANES, (hd + 1) * LANES)
            q_t = qt_ref[0, rows, :]
            q_both = jnp.concatenate([_keep_half_rows(q_t, 0), _keep_half_rows(q_t, 1)], axis=1)
            return jnp.dot(k_ref[0, pl.ds(key0, tile), rows], q_both, preferred_element_type=F32)

        def weights(hd, s):
            if diagonal:
                s = jnp.where(keep, s, -jnp.inf)
            m_old = m_ref[hd]
            m_new = jnp.maximum(m_old, jnp.max(s, axis=0, keepdims=True))
            m_ref[hd] = m_new
            return jnp.exp2(m_old - m_new), jnp.exp2(s - m_new).astype(BF16)

        def accumulate(hd, alpha, p):
            v_aug = jnp.concatenate([vt_ref[0, j, hd * LANES:(hd + 1) * LANES, :], ones], axis=0)
            acc_ref[hd] = alpha * acc_ref[hd] + jnp.dot(v_aug, p, preferred_element_type=F32)

        _staged(DIFF_HEADS, scores, weights, accumulate)

    def off_diagonal(j, carry):
        sweep(j, False)
        return carry

    lax.fori_loop(0, i, off_diagonal, 0)
    sweep(i, True)

    lam_v = lam_ref[...]
    lam = (jnp.exp(jnp.sum(lam_v[0:1] * lam_v[1:2], keepdims=True))
           - jnp.exp(jnp.sum(lam_v[2:3] * lam_v[3:4], keepdims=True)) + DIFF_LAMBDA_INIT)
    for hd in range(DIFF_HEADS):
        a1, a2 = acc_ref[hd, :, :tile], acc_ref[hd, :, tile:]
        o_t = (a1[:LANES] / a1[LANES:LANES + 1] - lam * (a2[:LANES] / a2[LANES:LANES + 1]))
        o_t = o_t * lax.rsqrt(jnp.mean(o_t * o_t, axis=0, keepdims=True) + NORM_EPS)
        o_t = o_t * subln_ref[...] * (1.0 - DIFF_LAMBDA_INIT)
        o_ref[0, :, hd * LANES:(hd + 1) * LANES] = o_t.T.astype(BF16)


def _diff_attention(q_t, k, v_t, lam_vecs, subln):
    b, s, _ = k.shape
    n_blk, _, t = v_t.shape[1:]
    const = lambda bi, i: (0, 0)
    return pl.pallas_call(
        functools.partial(_diff_attn_body, tile=t),
        grid=(b, n_blk),
        in_specs=[pl.BlockSpec((4, DIFF_HEAD_DIM), const), pl.BlockSpec((LANES, 1), const),
                  pl.BlockSpec((1, HALF, t), lambda bi, i: (bi, 0, i)),
                  pl.BlockSpec((1, s, HALF), lambda bi, i: (bi, 0, 0)),
                  pl.BlockSpec((1, n_blk, HALF, t), lambda bi, i: (bi, 0, 0, 0))],
        out_specs=pl.BlockSpec((1, t, HALF), lambda bi, i: (bi, i, 0)),
        out_shape=jax.ShapeDtypeStruct((b, s, HALF), BF16),
        scratch_shapes=[pltpu.VMEM((DIFF_HEADS, 1, 2 * t), F32),
                        pltpu.VMEM((DIFF_HEADS, LANES + ONES_ROWS, 2 * t), F32)],
        compiler_params=_params("parallel", "arbitrary"),
        name="diff_attn",
    )(lam_vecs, subln.reshape(LANES, 1), q_t, k, v_t)


def _s5_discretize_body(lr_ref, li_ref, ldt_ref, br_ref, bi_ref, k_ref, pr_ref, pi_ref, bbr_ref, bbi_ref):
    lr, li = lr_ref[...], li_ref[...]
    dt = jnp.exp(ldt_ref[...])
    mag = jnp.exp(lr * dt)
    ar, ai = mag * jnp.cos(li * dt), mag * jnp.sin(li * dt)
    den = lr * lr + li * li
    gr = ((ar - 1.0) * lr + ai * li) / den
    gi = (ai * lr - (ar - 1.0) * li) / den
    br, bi = br_ref[...], bi_ref[...]
    bbr_ref[...] = gr * br - gi * bi
    bbi_ref[...] = gr * bi + gi * br
    steps = k_ref[...]
    mag_k = jnp.exp(lr * dt * steps)
    pr_ref[...] = mag_k * jnp.cos(li * dt * steps)
    pi_ref[...] = mag_k * jnp.sin(li * dt * steps)


def _s5_discretize(a_re, a_im, log_dt, b_re, b_im):
    row = lambda a: a.reshape(1, S5_NSTATE)
    ldt = jnp.broadcast_to(log_dt[:, None], (S5_GROUPS, S5_STATE))
    b_t = lambda a: a.transpose(2, 0, 1).reshape(S5_GROUP, S5_NSTATE)
    steps = jnp.arange(1, S5_CHUNK + 1, dtype=F32).reshape(S5_CHUNK, 1)
    pshape = jax.ShapeDtypeStruct((S5_CHUNK, S5_NSTATE), F32)
    bshape = jax.ShapeDtypeStruct((S5_GROUP, S5_NSTATE), F32)
    return pl.pallas_call(
        _s5_discretize_body,
        out_shape=[pshape, pshape, bshape, bshape],
        name="s5_discretize",
    )(row(a_re), row(a_im), row(ldt), b_t(b_re), b_t(b_im), steps)


def _s5_body(u_ref, pr_ref, pi_ref, bbr_ref, bbi_ref, cr_ref, ci_ref, d_ref, wglu_ref, bglu_ref,
             o_ref, us_ref, ys_ref, xr_ref, xi_ref, hin_ref, carry_ref, *, tile):
    n_chunks = tile // S5_CHUNK
    slabs = HALF // LANES
    gw = S5_NSTATE // slabs

    @pl.when(pl.program_id(1) == 0)
    def _():
        carry_ref[...] = jnp.zeros_like(carry_ref)

    u = u_ref[0]
    for g in range(slabs):
        us_ref[g] = u[:, g * LANES:(g + 1) * LANES]
    u_perm = jnp.concatenate(
        [jnp.concatenate([us_ref[g, pl.ds(st, n_chunks, stride=S5_CHUNK), :] for g in range(slabs)], axis=1)
         for st in range(S5_CHUNK)], axis=0).astype(BF16)

    for g in range(slabs):
        ug = u_perm[:, g * LANES:(g + 1) * LANES]
        xr_ref[:, g * gw:(g + 1) * gw] = jnp.dot(ug, bbr_ref[g], preferred_element_type=F32)
        xi_ref[:, g * gw:(g + 1) * gw] = jnp.dot(ug, bbi_ref[g], preferred_element_type=F32)

    ar, ai = pr_ref[0:1, :], pi_ref[0:1, :]
    hr, hi = xr_ref[0:n_chunks, :], xi_ref[0:n_chunks, :]
    for st in range(1, S5_CHUNK):
        rows = slice(st * n_chunks, (st + 1) * n_chunks)
        hr, hi = ar * hr - ai * hi + xr_ref[rows, :], ar * hi + ai * hr + xi_ref[rows, :]
        xr_ref[rows, :] = hr
        xi_ref[rows, :] = hi

    last = (S5_CHUNK - 1) * n_chunks
    ac_r, ac_i = pr_ref[S5_CHUNK - 1:S5_CHUNK, :], pi_ref[S5_CHUNK - 1:S5_CHUNK, :]

    def carry_step(c, carry):
        cr, ci = carry
        hin_ref[0, pl.ds(c, 1), :] = cr
        hin_ref[1, pl.ds(c, 1), :] = ci
        zr, zi = xr_ref[pl.ds(last + c, 1), :], xi_ref[pl.ds(last + c, 1), :]
        return ac_r * cr - ac_i * ci + zr, ac_r * ci + ac_i * cr + zi

    cr, ci = lax.fori_loop(0, n_chunks, carry_step, (carry_ref[0:1, :], carry_ref[1:2, :]), unroll=4)
    carry_ref[0:1, :] = cr
    carry_ref[1:2, :] = ci

    hin_r, hin_i = hin_ref[0], hin_ref[1]
    for st in range(S5_CHUNK):
        rows = slice(st * n_chunks, (st + 1) * n_chunks)
        p_r, p_i = pr_ref[st:st + 1, :], pi_ref[st:st + 1, :]
        xr_ref[rows, :] = xr_ref[rows, :] + (p_r * hin_r - p_i * hin_i)
        xi_ref[rows, :] = xi_ref[rows, :] + (p_r * hin_i + p_i * hin_r)
    for g in range(slabs):
        cols = slice(g * gw, (g + 1) * gw)
        yg = (jnp.dot(xr_ref[:, cols].astype(BF16), cr_ref[g], preferred_element_type=F32)
              - jnp.dot(xi_ref[:, cols].astype(BF16), ci_ref[g], preferred_element_type=F32))
        for st in range(S5_CHUNK):
            ys_ref[g, pl.ds(st, n_chunks, stride=S5_CHUNK), :] = yg[st * n_chunks:(st + 1) * n_chunks]

    y = jnp.concatenate([ys_ref[g] for g in range(slabs)], axis=1) + d_ref[...] * u
    z = jax.nn.gelu(y, approximate=True)
    gate = jnp.dot(z.astype(BF16), wglu_ref[...], preferred_element_type=F32) + bglu_ref[...]
    o_ref[0] = (z * jax.nn.sigmoid(gate)).astype(BF16)


def _s5(u, a_re, a_im, log_dt, b_re, b_im, c_re, c_im, d_skip, w_glu, b_glu):
    b, s, w = u.shape
    t = min(S5_TILE, s)
    slabs = w // LANES
    per = S5_GROUPS // slabs
    gw = S5_NSTATE // slabs
    pw_r, pw_i, bbr, bbi = _s5_discretize(a_re, a_im, log_dt, b_re, b_im)
    eye = jnp.eye(per, dtype=F32)

    def block_in(m):
        m = m.reshape(S5_GROUP, slabs, per, S5_STATE)
        return jnp.einsum('csgp,gh->sgchp', m, eye).reshape(slabs, LANES, gw).astype(BF16)

    def block_out(m):
        m = m.reshape(slabs, per, S5_GROUP, S5_STATE)
        return jnp.einsum('sgcp,gh->sgphc', m, eye).reshape(slabs, gw, LANES).astype(BF16)

    seq_blk = lambda bi, ti: (bi, ti, 0)
    n_chunks = t // S5_CHUNK
    return pl.pallas_call(
        functools.partial(_s5_body, tile=t),
        grid=(b, s // t),
        in_specs=[pl.BlockSpec((1, t, w), seq_blk),
                  _resident((S5_CHUNK, S5_NSTATE)), _resident((S5_CHUNK, S5_NSTATE)),
                  _resident((slabs, LANES, gw)), _resident((slabs, LANES, gw)),
                  _resident((slabs, gw, LANES)), _resident((slabs, gw, LANES)),
                  _resident((1, w)), _resident((w, w)), _resident((1, w))],
        out_specs=pl.BlockSpec((1, t, w), seq_blk),
        out_shape=jax.ShapeDtypeStruct((b, s, w), BF16),
        scratch_shapes=[pltpu.VMEM((slabs, t, LANES), F32), pltpu.VMEM((slabs, t, LANES), F32),
                        pltpu.VMEM((t, S5_NSTATE), F32), pltpu.VMEM((t, S5_NSTATE), F32),
                        pltpu.VMEM((2, n_chunks, S5_NSTATE), F32), pltpu.VMEM((2, S5_NSTATE), F32)],
        compiler_params=_params("parallel", "arbitrary"),
        name="s5",
    )(u, pw_r, pw_i, block_in(bbr), block_in(bbi),
      block_out(c_re), block_out(c_im), d_skip.reshape(1, w), w_glu.astype(BF16), b_glu.reshape(1, w))


def _proj1_body(x_ref, g_ref, wt_ref, wm_ref, wo_ref, wk_ref, wg_ref,
                qk_ref, vm_ref, om_ref, qft_ref, kf_ref, vft_ref, gt_ref):
    h = _rms(x_ref[...], g_ref[...]).astype(BF16)
    pt = lax.dot_general(wt_ref[...], h, NT_DIMS, preferred_element_type=F32)
    qft_ref[0] = (pt[:HALF] * (FOX_HEAD_DIM ** -0.5 * LOG2_E)).astype(BF16)
    _store_key_blocks(vft_ref, pt[HALF:].astype(BF16))
    pm = jnp.dot(h, wm_ref[...].astype(BF16), preferred_element_type=F32)
    qk_ref[...] = pm[:, 0:HALF]
    vm_ref[...] = pm[:, HALF:2 * HALF].astype(BF16)
    om_ref[...] = jnp.dot(h, wo_ref[...].astype(BF16), preferred_element_type=F32)
    kf_ref[...] = jnp.dot(h, wk_ref[...].astype(BF16), preferred_element_type=F32).astype(BF16)
    gt_ref[...] = jnp.dot(h, wg_ref[...].astype(BF16), preferred_element_type=F32)


def _proj1(x, norm, w_in, seq):
    n, d = x.shape
    b = n // seq
    tm = min(TOKEN_TILE, seq)
    tps = seq // tm
    row = lambda i: (i, 0)
    seq_t = lambda i: (i // tps, 0, i % tps)
    ta = min(ATTN_TILE, seq)
    key_blk = lambda i: (i // tps, i % tps, 0, 0)
    o = 2 * HALF
    g = 2 * MLSTM_HEADS
    om = o + g
    w_o = w_in[:, om:om + HALF]
    w_k = w_in[:, om + 2 * HALF:om + 3 * HALF]
    w_g = jnp.concatenate([w_in[:, o:om], w_in[:, om + 4 * HALF:],
                           jnp.zeros((d, GATE_COLS - g - FOX_HEADS), w_in.dtype)], axis=1)
    w_t = jnp.concatenate([w_in[:, om + HALF:om + 2 * HALF], w_in[:, om + 3 * HALF:om + 4 * HALF]],
                          axis=1).T.astype(BF16)
    f32_half = jax.ShapeDtypeStruct((n, HALF), F32)
    bf_half = jax.ShapeDtypeStruct((n, HALF), BF16)
    t_shape = jax.ShapeDtypeStruct((b, HALF, seq), BF16)
    half_spec = pl.BlockSpec((tm, HALF), row)
    t_spec = pl.BlockSpec((1, HALF, tm), seq_t)
    return pl.pallas_call(
        _proj1_body,
        grid=(n // tm,),
        in_specs=[pl.BlockSpec((tm, d), row), _resident((1, d)), _resident((2 * HALF, d)),
                  pl.BlockSpec((d, o), lambda i: (0, 0), pipeline_mode=pl.Buffered(1)),
                  _resident((d, HALF)), _resident((d, HALF)), _resident((d, GATE_COLS))],
        out_specs=[half_spec, half_spec, half_spec, t_spec, half_spec,
                   pl.BlockSpec((1, tm // ta, HALF, ta), key_blk), pl.BlockSpec((tm, GATE_COLS), row)],
        out_shape=[f32_half, bf_half, f32_half, t_shape, bf_half,
                   jax.ShapeDtypeStruct((b, seq // ta, HALF, ta), BF16),
                   jax.ShapeDtypeStruct((n, GATE_COLS), F32)],
        compiler_params=_params("parallel"),
        name="proj1",
    )(x, norm.reshape(1, d), w_t, w_in, w_o, w_k, w_g)


def _mlstm_body(qk_ref, v_ref, og_ref, gt_ref, cw_ref, cb_ref, gb_ref, on_ref, out_ref,
                xbuf_ref, cn_ref, m_ref, *, chunk):
    pad = SUBLANES
    dk, dv = MLSTM_QK_DIM, MLSTM_V_DIM

    @pl.when(pl.program_id(1) == 0)
    def _():
        xbuf_ref[0:pad, :] = jnp.zeros((pad, HALF), F32)
        cn_ref[...] = jnp.zeros_like(cn_ref)
        m_ref[...] = jnp.zeros_like(m_ref)

    x = qk_ref[0]
    xbuf_ref[pad:pad + chunk, :] = x
    a = jnp.zeros_like(x) + cb_ref[...]
    for tap in range(MLSTM_CONV):
        a = a + cw_ref[tap:tap + 1, :] * xbuf_ref[pl.ds(pad - (MLSTM_CONV - 1) + tap, chunk), :]
    xbuf_ref[0:pad, :] = x[chunk - pad:chunk, :]
    a = a * jax.nn.sigmoid(a)
    q = a[:, :MLSTM_HEADS * dk] * dk ** -0.5
    k = a[:, MLSTM_HEADS * dk:]
    k_t = k.T

    gi = gt_ref[0] + gb_ref[...]
    bcum = _cumsum_rows(_log_sigmoid(gi))
    gi_t = gi.T
    bcum_t = bcum.T
    tri = (lax.broadcasted_iota(jnp.int32, (chunk, chunk), 1)
           <= lax.broadcasted_iota(jnp.int32, (chunk, chunk), 0))
    lane = lax.broadcasted_iota(jnp.int32, (1, LANES), 1)
    v_all = v_ref[0]
    o_gate = og_ref[0]
    ones = jnp.ones((chunk, dv), BF16)

    states = [cn_ref[pair] for pair in range(MLSTM_HEADS // 2)]

    def products(hd):
        pair, odd = hd // 2, hd % 2
        sel = (lane >= dk) if odd else (lane < dk)
        q2 = jnp.where(sel, q[:, pair * LANES:(pair + 1) * LANES], 0.0).astype(BF16)
        k2 = k[:, pair * LANES:(pair + 1) * LANES].astype(BF16)
        qk = lax.dot_general(q2, k2, NT_DIMS, preferred_element_type=F32)
        return qk, jnp.dot(q2, states[pair].astype(BF16), preferred_element_type=F32)

    def gate_weights(hd, prods):
        qk, q_state = prods
        f_lane = MLSTM_HEADS + hd
        b_col = bcum[:, f_lane:f_lane + 1]
        b_row = bcum_t[f_lane:f_lane + 1, :]
        i_row = gi_t[hd:hd + 1, :]
        m_prev = m_ref[hd]
        dm = jnp.where(tri, b_col - b_row + i_row, -jnp.inf)
        g = b_col + m_prev
        mt = jnp.maximum(g, jnp.max(dm, axis=-1, keepdims=True))
        s = (qk * jnp.exp(dm - mt)).astype(BF16)
        w_inter = jnp.exp(g - mt)

        b_last = bcum[chunk - 1:chunk, f_lane:f_lane + 1]
        dec = b_last - b_row + i_row
        m_new = jnp.maximum(b_last + m_prev, jnp.max(dec, axis=-1, keepdims=True))
        k_w = (k_t[hd * dk:(hd + 1) * dk, :] * jnp.exp(dec - m_new)).astype(BF16)
        carry_w = jnp.exp(b_last + m_prev - m_new)
        m_ref[hd] = m_new
        return s, k_w, q_state, w_inter, mt, carry_w

    def outputs(hd, s, k_w, q_state, w_inter, mt, carry_w):
        pair, odd = hd // 2, hd % 2
        v_aug = jnp.concatenate([v_all[:, hd * dv:(hd + 1) * dv], ones], axis=1)
        s_v = jnp.dot(s, v_aug, preferred_element_type=F32)
        num = w_inter * q_state[:, :dv] + s_v[:, :dv]
        den = w_inter * q_state[:, dv:] + s_v[:, dv:]
        h_out = num / jnp.maximum(jnp.abs(den), jnp.exp(-mt))
        rows = slice(odd * dk, (odd + 1) * dk)
        cn_ref[pair, rows, :] = (carry_w * states[pair][rows, :]
                                 + jnp.dot(k_w, v_aug, preferred_element_type=F32))
        cols = slice(hd * dv, (hd + 1) * dv)
        h_norm = _rms(h_out, on_ref[hd:hd + 1, :])
        out_ref[0, :, cols] = (jax.nn.sigmoid(o_gate[:, cols]) * h_norm).astype(BF16)

    _staged(MLSTM_HEADS, products, gate_weights, outputs)


def _mlstm(qk, v, o_gate, gates, conv_w, conv_b, b_i, b_f, out_norm):
    b, s, _ = qk.shape
    c = min(MLSTM_CHUNK, s)
    gate_bias = jnp.concatenate([b_i, b_f, jnp.zeros((GATE_COLS - 2 * MLSTM_HEADS,), F32)]).reshape(1, GATE_COLS)
    seq_blk = lambda bi, ci: (bi, ci, 0)
    half_spec = pl.BlockSpec((1, c, HALF), seq_blk)
    return pl.pallas_call(
        functools.partial(_mlstm_body, chunk=c),
        grid=(b, s // c),
        in_specs=[half_spec, half_spec, half_spec, pl.BlockSpec((1, c, GATE_COLS), seq_blk),
                  _resident((MLSTM_CONV, HALF)), _resident((1, HALF)), _resident((1, GATE_COLS)),
                  _resident((MLSTM_HEADS, MLSTM_V_DIM))],
        out_specs=half_spec,
        out_shape=jax.ShapeDtypeStruct((b, s, HALF), BF16),
        scratch_shapes=[pltpu.VMEM((c + SUBLANES, HALF), F32),
                        pltpu.VMEM((MLSTM_HEADS // 2, 2 * MLSTM_QK_DIM, 2 * MLSTM_V_DIM), F32),
                        pltpu.VMEM((MLSTM_HEADS, 1, 1), F32)],
        compiler_params=_params("parallel", "arbitrary"),
        name="mlstm",
    )(qk, v, o_gate, gates, conv_w, conv_b.reshape(1, HALF), gate_bias, out_norm)


FOX_GATE_LANE = 2 * MLSTM_HEADS
F_TERMS = 3


def _fox_cumsum_body(gt_ref, gb_ref, sel_ref, frow_ref, fk_ref):
    f = _cumsum_rows(_log_sigmoid(gt_ref[0] + gb_ref[...])) * LOG2_E
    hi = f.astype(BF16)
    rest = f - hi.astype(F32)
    mid = rest.astype(BF16)
    lo = (rest - mid.astype(F32)).astype(BF16)
    terms = jnp.concatenate([hi, mid, lo], axis=1)
    fk_ref[0] = jnp.dot(terms, sel_ref[...], preferred_element_type=F32).astype(BF16)
    frow_ref[0] = f.T[FOX_GATE_LANE:FOX_GATE_LANE + FOX_HEADS, :]


def _fox_cumsum(gates, fox_b_f):
    b, s, _ = gates.shape
    bias = jnp.concatenate([jnp.zeros((FOX_GATE_LANE,), F32), fox_b_f,
                            jnp.zeros((GATE_COLS - FOX_GATE_LANE - FOX_HEADS,), F32)]).reshape(1, GATE_COLS)
    src, dst = [], []
    for head in range(FOX_HEADS):
        for term in range(F_TERMS):
            src.append(term * GATE_COLS + FOX_GATE_LANE + head)
            dst.append((head // 2) * LANES + F_TERMS * (head % 2) + term)
    sel = jnp.zeros((F_TERMS * GATE_COLS, HALF), BF16).at[jnp.asarray(src), jnp.asarray(dst)].set(-1.0)
    return pl.pallas_call(
        _fox_cumsum_body,
        grid=(b,),
        in_specs=[pl.BlockSpec((1, s, GATE_COLS), lambda bi: (bi, 0, 0)), _resident((1, GATE_COLS)),
                  _resident((F_TERMS * GATE_COLS, HALF))],
        out_specs=[pl.BlockSpec((1, FOX_HEADS, s), lambda bi: (bi, 0, 0)),
                   pl.BlockSpec((1, s, HALF), lambda bi: (bi, 0, 0))],
        out_shape=[jax.ShapeDtypeStruct((b, FOX_HEADS, s), F32),
                   jax.ShapeDtypeStruct((b, s, HALF), BF16)],
        compiler_params=_params("parallel"),
        name="fox_cumsum",
    )(gates, bias, sel)


def _fox_attn_body(qt_ref, k_ref, fk_ref, vt_ref, frow_ref, o_ref, m_ref, acc_ref, *, tile):
    i = pl.program_id(1)
    dim = lax.broadcasted_iota(jnp.int32, (LANES, 1), 0)
    dh = FOX_HEAD_DIM
    m_ref[...] = jnp.full_like(m_ref, -jnp.inf)
    acc_ref[...] = jnp.zeros_like(acc_ref)
    ones = jnp.ones((ONES_ROWS, tile), BF16)

    def sweep(j, diagonal):
        key0 = pl.multiple_of(j * tile, tile)
        if diagonal:
            query = lax.broadcasted_iota(jnp.int32, (tile, 2 * tile), 1)
            query = jnp.where(query >= tile, query - tile, query)
            keep = lax.broadcasted_iota(jnp.int32, (tile, 2 * tile), 0) <= query

        def scores(pair):
            rows = slice(pair * LANES, (pair + 1) * LANES)
            q_t = qt_ref[0, rows, :]
            k_aug = jnp.concatenate([k_ref[0, pl.ds(key0, tile), rows],
                                     fk_ref[0, pl.ds(key0, tile), rows]], axis=1)
            q_aug = []
            for c in range(2):
                ones_rows = jnp.logical_and(dim >= F_TERMS * c, dim < F_TERMS * (c + 1))
                q_aug.append(jnp.concatenate(
                    [_keep_half_rows(q_t, c),
                     jnp.broadcast_to(jnp.where(ones_rows, 1.0, 0.0), q_t.shape).astype(BF16)], axis=0))
            return jnp.dot(k_aug, jnp.concatenate(q_aug, axis=1), preferred_element_type=F32)

        def weights(pair, s):
            s = s + jnp.concatenate([frow_ref[0, 2 * pair:2 * pair + 1, :],
                                     frow_ref[0, 2 * pair + 1:2 * pair + 2, :]], axis=1)
            if diagonal:
                s = jnp.where(keep, s, -jnp.inf)
            m_old = m_ref[pair]
            m_new = jnp.maximum(m_old, jnp.max(s, axis=0, keepdims=True))
            m_ref[pair] = m_new
            return jnp.exp2(m_old - m_new), jnp.exp2(s - m_new).astype(BF16)

        def accumulate(pair, alpha, p):
            for c in range(2):
                hd = 2 * pair + c
                cols = slice(c * tile, (c + 1) * tile)
                v_aug = jnp.concatenate([vt_ref[0, j, hd * dh:(hd + 1) * dh, :], ones], axis=0)
                acc_ref[hd] = alpha[:, cols] * acc_ref[hd] + jnp.dot(v_aug, p[:, cols], preferred_element_type=F32)

        _staged(FOX_HEADS // 2, scores, weights, accumulate)

    def off_diagonal(j, carry):
        sweep(j, False)
        return carry

    lax.fori_loop(0, i, off_diagonal, 0)
    sweep(i, True)

    for pair in range(FOX_HEADS // 2):
        a0, a1 = acc_ref[2 * pair], acc_ref[2 * pair + 1]
        o_t = jnp.concatenate([a0[:dh] / a0[dh:dh + 1], a1[:dh] / a1[dh:dh + 1]], axis=0)
        o_ref[0, :, pair * LANES:(pair + 1) * LANES] = o_t.T.astype(BF16)


def _fox_attention(q_t, k, fk, v_t, frow):
    b, s, _ = k.shape
    n_blk, _, t = v_t.shape[1:]
    whole = lambda bi, i: (bi, 0, 0)
    return pl.pallas_call(
        functools.partial(_fox_attn_body, tile=t),
        grid=(b, n_blk),
        in_specs=[pl.BlockSpec((1, HALF, t), lambda bi, i: (bi, 0, i)),
                  pl.BlockSpec((1, s, HALF), whole), pl.BlockSpec((1, s, HALF), whole),
                  pl.BlockSpec((1, n_blk, HALF, t), lambda bi, i: (bi, 0, 0, 0)),
                  pl.BlockSpec((1, FOX_HEADS, t), lambda bi, i: (bi, 0, i))],
        out_specs=pl.BlockSpec((1, t, HALF), lambda bi, i: (bi, i, 0)),
        out_shape=jax.ShapeDtypeStruct((b, s, HALF), BF16),
        scratch_shapes=[pltpu.VMEM((FOX_HEADS // 2, 1, 2 * t), F32),
                        pltpu.VMEM((FOX_HEADS, FOX_HEAD_DIM + ONES_ROWS, t), F32)],
        compiler_params=_params("parallel", "arbitrary"),
        name="fox_attn",
    )(q_t, k, fk, v_t, frow)


def kernel(x, l0_ffn1_norm, l0_ffn1_w_gate, l0_ffn1_w_up, l0_ffn1_w_down, l0_mix_norm, l0_w_in, l0_diff_lambda_q1, l0_diff_lambda_k1, l0_diff_lambda_q2, l0_diff_lambda_k2, l0_diff_subln, l0_s5_a_re, l0_s5_a_im, l0_s5_log_dt, l0_s5_b_re, l0_s5_b_im, l0_s5_c_re, l0_s5_c_im, l0_s5_d, l0_s5_w_glu, l0_s5_b_glu, l0_w_out, l0_ffn2_norm, l0_ffn2_w_gate, l0_ffn2_w_up, l0_ffn2_w_down, l1_ffn1_norm, l1_ffn1_w_gate, l1_ffn1_w_up, l1_ffn1_w_down, l1_mix_norm, l1_w_in, l1_mlstm_conv_w, l1_mlstm_conv_b, l1_mlstm_b_i, l1_mlstm_b_f, l1_mlstm_out_norm, l1_fox_b_f, l1_w_out, l1_ffn2_norm, l1_ffn2_w_gate, l1_ffn2_w_up, l1_ffn2_w_down, final_norm):
    b, s, d = x.shape
    n = b * s
    seq = lambda a: a.reshape(b, s, a.shape[-1])
    flat = lambda a: a.reshape(n, a.shape[-1])

    x = _ffn(x.reshape(n, d), l0_ffn1_norm, l0_ffn1_w_gate, l0_ffn1_w_up, l0_ffn1_w_down)
    q_t, k, v_t, u = _proj0(x, l0_mix_norm, l0_w_in, s)
    lam_vecs = jnp.stack([l0_diff_lambda_q1, l0_diff_lambda_k1, l0_diff_lambda_q2, l0_diff_lambda_k2])
    ya = _diff_attention(q_t, seq(k), v_t, lam_vecs, l0_diff_subln)
    yb = _s5(seq(u), l0_s5_a_re, l0_s5_a_im, l0_s5_log_dt, l0_s5_b_re, l0_s5_b_im,
             l0_s5_c_re, l0_s5_c_im, l0_s5_d, l0_s5_w_glu, l0_s5_b_glu)
    x = _ffn(x, l0_ffn2_norm, l0_ffn2_w_gate, l0_ffn2_w_up, l0_ffn2_w_down,
             mix=(flat(ya), flat(yb), l0_w_out[:HALF], l0_w_out[HALF:]))

    x = _ffn(x, l1_ffn1_norm, l1_ffn1_w_gate, l1_ffn1_w_up, l1_ffn1_w_down)
    qk_m, v_m, o_m, qf_t, k_f, vf_t, gates = _proj1(x, l1_mix_norm, l1_w_in, s)
    hm = _mlstm(seq(qk_m), seq(v_m), seq(o_m), seq(gates), l1_mlstm_conv_w, l1_mlstm_conv_b,
                l1_mlstm_b_i, l1_mlstm_b_f, l1_mlstm_out_norm)
    frow, fk = _fox_cumsum(seq(gates), l1_fox_b_f)
    hf = _fox_attention(qf_t, seq(k_f), fk, vf_t, frow)
    x = _ffn(x, l1_ffn2_norm, l1_ffn2_w_gate, l1_ffn2_w_up, l1_ffn2_w_down,
             mix=(flat(hm), flat(hf), l1_w_out[:HALF], l1_w_out[HALF:]), final_norm=final_norm)
    return x.reshape(b, s, d)
```

```python
import functools
import math

import jax
import jax.numpy as jnp
import numpy as np
from jax import lax
from jax.experimental import pallas as pl
from jax.experimental.pallas import tpu as pltpu

F32 = jnp.float32
BF16 = jnp.bfloat16

D_MODEL = 1024
D_FF = 2816
NORM_EPS = 1e-6
ROPE_THETA = 500000.0
DIFF_HEADS = 4
DIFF_HEAD_DIM = 64
ROT_DIM = DIFF_HEAD_DIM // 4
DIFF_LAMBDA_INIT = 0.8 - 0.6 * math.exp(-0.3 * 0)
S5_GROUPS = 32
S5_GROUP = 16
S5_STATE = 64
S5_NSTATE = S5_GROUPS * S5_STATE
MLSTM_HEADS = 4
MLSTM_QK_DIM = 64
MLSTM_V_DIM = 128
MLSTM_CONV = 4
FOX_HEADS = 8
FOX_HEAD_DIM = 64
HALF = 512

LANES = 128
SUBLANES = 8
VMEM_LIMIT_BYTES = 60 * 1024 * 1024

TOKEN_TILE = 1024
FFN_TILE = 512
FF_CHUNK = 256
ATTN_TILE = 512
S5_TILE = 1024
S5_CHUNK = 16
MLSTM_CHUNK = 512
GATE_COLS = LANES

NT_DIMS = (((1,), (1,)), ((), ()))
LOG2_E = math.log2(math.e)
ONES_ROWS = 16


def _params(*semantics):
    return pltpu.CompilerParams(dimension_semantics=semantics, vmem_limit_bytes=VMEM_LIMIT_BYTES)


def _resident(shape):
    return pl.BlockSpec(shape, lambda *_: (0,) * len(shape), pipeline_mode=pl.Buffered(1))


def _rms(x, g):
    return x * lax.rsqrt(jnp.mean(x * x, axis=-1, keepdims=True) + NORM_EPS) * g


def _log_sigmoid(x):
    return jnp.minimum(x, 0.0) - jnp.log1p(jnp.exp(-jnp.abs(x)))


def _cumsum_rows(x):
    n = x.shape[0]
    row = lax.broadcasted_iota(jnp.int32, x.shape, 0)
    k = 1
    while k < n:
        x = x + jnp.where(row >= k, pltpu.roll(x, k, 0), 0.0)
        k *= 2
    return x


def _keep_half_rows(x, half):
    rows = x.shape[0] // 2
    zeros = jnp.zeros((rows,) + x.shape[1:], x.dtype)
    return jnp.concatenate([x[:rows], zeros] if half == 0 else [zeros, x[rows:]], axis=0)


def _staged(n, scores, weights, accumulate):
    s, w = {}, {}
    for step in range(n + 2):
        if step < n:
            s[step] = scores(step)
        if step >= 2:
            accumulate(step - 2, *w.pop(step - 2))
        if 1 <= step <= n:
            w[step - 1] = weights(step - 1, s.pop(step - 1))


def _store_key_blocks(ref, value):
    width = ref.shape[3]
    for blk in range(ref.shape[1]):
        ref[0, blk] = value[:, blk * width:(blk + 1) * width]


def _ffn_body(*refs, has_mix, has_final):
    refs = list(refs)
    x_ref = refs.pop(0)
    if has_mix:
        a_ref, b_ref, wa_ref, wb_ref = refs[:4]
        refs = refs[4:]
    g_ref, wg_hbm, wu_hbm, wd_hbm = refs[:4]
    refs = refs[4:]
    if has_final:
        gf_ref = refs.pop(0)
    o_ref, wg_ref, wu_ref, wd_ref, sem = refs

    n_chunks = D_FF // FF_CHUNK
    chunk = lambda c: slice(c * FF_CHUNK, (c + 1) * FF_CHUNK)

    def chunk_copies(c):
        return (pltpu.make_async_copy(wg_hbm.at[:, chunk(c)], wg_ref.at[:, chunk(c)], sem.at[0, c]),
                pltpu.make_async_copy(wu_hbm.at[:, chunk(c)], wu_ref.at[:, chunk(c)], sem.at[1, c]),
                pltpu.make_async_copy(wd_hbm.at[chunk(c), :], wd_ref.at[chunk(c), :], sem.at[2, c]))

    first = pl.program_id(0) == 0

    @pl.when(first)
    def _():
        for c in range(n_chunks):
            for copy in chunk_copies(c):
                copy.start()

    def tile(wait_for_weights):
        x = x_ref[...]
        if has_mix:
            x = x + jnp.dot(a_ref[...], wa_ref[...], preferred_element_type=F32)
            x = x + jnp.dot(b_ref[...], wb_ref[...], preferred_element_type=F32)
        h = _rms(x, g_ref[...]).astype(BF16)
        acc = jnp.zeros_like(x)
        for c in range(n_chunks):
            if wait_for_weights:
                for copy in chunk_copies(c):
                    copy.wait()
            gate = jnp.dot(h, wg_ref[:, chunk(c)].astype(BF16), preferred_element_type=F32)
            up = jnp.dot(h, wu_ref[:, chunk(c)].astype(BF16), preferred_element_type=F32)
            act = (gate * jax.nn.sigmoid(gate) * up).astype(BF16)
            acc = acc + jnp.dot(act, wd_ref[chunk(c), :].astype(BF16), preferred_element_type=F32)
        y = x + 0.5 * acc
        if has_final:
            y = _rms(y, gf_ref[...])
        o_ref[...] = y

    @pl.when(first)
    def _():
        tile(True)

    @pl.when(jnp.logical_not(first))
    def _():
        tile(False)


def _ffn(x, norm, wg, wu, wd, mix=None, final_norm=None):
    n, d = x.shape
    tm = min(FFN_TILE, n)
    row = lambda i: (i, 0)
    args = [x]
    specs = [pl.BlockSpec((tm, d), row)]
    if mix is not None:
        a, b, wa, wb = mix
        args += [a, b, wa.astype(BF16), wb.astype(BF16)]
        specs += [pl.BlockSpec((tm, HALF), row), pl.BlockSpec((tm, HALF), row),
                  _resident((HALF, d)), _resident((HALF, d))]
    args += [norm.reshape(1, d), wg, wu, wd]
    specs += [_resident((1, d))] + [pl.BlockSpec(memory_space=pl.ANY)] * 3
    if final_norm is not None:
        args.append(final_norm.reshape(1, d))
        specs.append(_resident((1, d)))
    body = functools.partial(_ffn_body, has_mix=mix is not None, has_final=final_norm is not None)
    return pl.pallas_call(
        body,
        grid=(n // tm,),
        in_specs=specs,
        out_specs=pl.BlockSpec((tm, d), row),
        out_shape=jax.ShapeDtypeStruct((n, d), F32),
        scratch_shapes=[pltpu.VMEM((d, D_FF), F32), pltpu.VMEM((d, D_FF), F32), pltpu.VMEM((D_FF, d), F32),
                        pltpu.SemaphoreType.DMA((3, D_FF // FF_CHUNK))],
        compiler_params=_params("arbitrary"),
        name="ffn",
    )(*args)


def _rope_angles(seq):
    pos = np.arange(seq, dtype=np.float32)
    inv = np.float32(ROPE_THETA) ** (-np.arange(0, ROT_DIM, 2, dtype=np.float32) / np.float32(ROT_DIM))
    ang = pos[:, None] * inv[None, :]
    return np.cos(ang), np.sin(ang)


def _rope_lane_tables(seq):
    half = ROT_DIM // 2
    cos, sin = _rope_angles(seq)
    one = np.ones((seq, DIFF_HEAD_DIM - ROT_DIM), np.float32)
    zero8 = np.zeros((seq, half), np.float32)
    zero48 = np.zeros((seq, DIFF_HEAD_DIM - ROT_DIM), np.float32)
    cos_t = np.concatenate([cos, cos, one] * 2, axis=1)
    sa_t = np.concatenate([-sin, zero8, zero48] * 2, axis=1)
    sb_t = np.concatenate([zero8, sin, zero48] * 2, axis=1)
    return cos_t, sa_t, sb_t


def _proj0_body(x_ref, g_ref, wt_ref, wk_ref, wu_ref, cos_ref, sa_ref, sb_ref, cosr_ref, sinr_ref,
                qt_ref, k_ref, vt_ref, u_ref):
    h = _rms(x_ref[...], g_ref[...]).astype(BF16)
    pt = lax.dot_general(wt_ref[...], h, NT_DIMS, preferred_element_type=F32)
    pk = jnp.dot(h, wk_ref[...].astype(BF16), preferred_element_type=F32)
    half = ROT_DIM // 2
    scale = DIFF_HEAD_DIM ** -0.5 * LOG2_E

    cos_r, sin_r = cosr_ref[...], sinr_ref[...]
    for comp in range(2 * DIFF_HEADS):
        r0 = comp * DIFF_HEAD_DIM
        x1, x2 = pt[r0:r0 + half], pt[r0 + half:r0 + ROT_DIM]
        qt_ref[0, r0:r0 + half, :] = ((x1 * cos_r - x2 * sin_r) * scale).astype(BF16)
        qt_ref[0, r0 + half:r0 + ROT_DIM, :] = ((x2 * cos_r + x1 * sin_r) * scale).astype(BF16)
        qt_ref[0, r0 + ROT_DIM:r0 + DIFF_HEAD_DIM, :] = (pt[r0 + ROT_DIM:r0 + DIFF_HEAD_DIM] * scale).astype(BF16)
    _store_key_blocks(vt_ref, pt[HALF:2 * HALF].astype(BF16))

    cos_t, sa_t, sb_t = cos_ref[...], sa_ref[...], sb_ref[...]
    for hd in range(DIFF_HEADS):
        sl = slice(hd * LANES, (hd + 1) * LANES)
        t = pk[:, sl]
        k_ref[:, sl] = (t * cos_t + pltpu.roll(t, LANES - half, 1) * sa_t
                        + pltpu.roll(t, half, 1) * sb_t).astype(BF16)
    u_ref[...] = jnp.dot(h, wu_ref[...].astype(BF16), preferred_element_type=F32)


def _proj0(x, norm, w_in, seq):
    n, d = x.shape
    b = n // seq
    tm = min(TOKEN_TILE, seq)
    tps = seq // tm
    row = lambda i: (i, 0)
    pos = lambda i: (i % tps, 0)
    pos_t = lambda i: (0, i % tps)
    seq_t = lambda i: (i // tps, 0, i % tps)
    ta = min(ATTN_TILE, seq)
    key_blk = lambda i: (i // tps, i % tps, 0, 0)
    cos, sin = _rope_angles(seq)
    w_t = jnp.concatenate([w_in[:, :HALF], w_in[:, 2 * HALF:3 * HALF]], axis=1).T.astype(BF16)
    w_cols = lambda blk: pl.BlockSpec((d, HALF), lambda i: (0, blk), pipeline_mode=pl.Buffered(1))
    t_shape = jax.ShapeDtypeStruct((b, HALF, seq), BF16)
    return pl.pallas_call(
        _proj0_body,
        grid=(n // tm,),
        in_specs=[pl.BlockSpec((tm, d), row), _resident((1, d)), _resident((2 * HALF, d)),
                  w_cols(1), w_cols(3),
                  pl.BlockSpec((tm, LANES), pos), pl.BlockSpec((tm, LANES), pos),
                  pl.BlockSpec((tm, LANES), pos),
                  pl.BlockSpec((ROT_DIM // 2, tm), pos_t), pl.BlockSpec((ROT_DIM // 2, tm), pos_t)],
        out_specs=[pl.BlockSpec((1, HALF, tm), seq_t), pl.BlockSpec((tm, HALF), row),
                   pl.BlockSpec((1, tm // ta, HALF, ta), key_blk), pl.BlockSpec((tm, HALF), row)],
        out_shape=[t_shape, jax.ShapeDtypeStruct((n, HALF), BF16),
                   jax.ShapeDtypeStruct((b, seq // ta, HALF, ta), BF16),
                   jax.ShapeDtypeStruct((n, HALF), F32)],
        compiler_params=_params("parallel"),
        name="proj0",
    )(x, norm.reshape(1, d), w_t, w_in, w_in, *_rope_lane_tables(seq), cos.T, sin.T)


def _diff_attn_body(lam_ref, subln_ref, qt_ref, k_ref, vt_ref, o_ref, m_ref, acc_ref, *, tile):
    i = pl.program_id(1)
    m_ref[...] = jnp.full_like(m_ref, -jnp.inf)
    acc_ref[...] = jnp.zeros_like(acc_ref)
    ones = jnp.ones((ONES_ROWS, tile), BF16)

    def sweep(j, diagonal):
        key0 = pl.multiple_of(j * tile, tile)
        if diagonal:
            query = lax.broadcasted_iota(jnp.int32, (tile, 2 * tile), 1)
            query = jnp.where(query >= tile, query - tile, query)
            keep = lax.broadcasted_iota(jnp.int32, (tile, 2 * tile), 0) <= query

        def scores(hd):
            rows = slice(hd * LANES, (hd + 1) * LANES)
            q_t = qt_ref[0, rows, :]
            q_both = jnp.concatenate([_keep_half_rows(q_t, 0), _keep_half_rows(q_t, 1)], axis=1)
            return jnp.dot(k_ref[0, pl.ds(key0, tile), rows], q_both, preferred_element_type=F32)

        def weights(hd, s):
            if diagonal:
                s = jnp.where(keep, s, -jnp.inf)
            m_old = m_ref[hd]
            m_new = jnp.maximum(m_old, jnp.max(s, axis=0, keepdims=True))
            m_ref[hd] = m_new
            return jnp.exp2(m_old - m_new), jnp.exp2(s - m_new).astype(BF16)

        def accumulate(hd, alpha, p):
            v_aug = jnp.concatenate([vt_ref[0, j, hd * LANES:(hd + 1) * LANES, :], ones], axis=0)
            acc_ref[hd] = alpha * acc_ref[hd] + jnp.dot(v_aug, p, preferred_element_type=F32)

        _staged(DIFF_HEADS, scores, weights, accumulate)

    def off_diagonal(j, carry):
        sweep(j, False)
        return carry

    lax.fori_loop(0, i, off_diagonal, 0)
    sweep(i, True)

    lam_v = lam_ref[...]
    lam = (jnp.exp(jnp.sum(lam_v[0:1] * lam_v[1:2], keepdims=True))
           - jnp.exp(jnp.sum(lam_v[2:3] * lam_v[3:4], keepdims=True)) + DIFF_LAMBDA_INIT)
    for hd in range(DIFF_HEADS):
        a1, a2 = acc_ref[hd, :, :tile], acc_ref[hd, :, tile:]
        o_t = (a1[:LANES] / a1[LANES:LANES + 1] - lam * (a2[:LANES] / a2[LANES:LANES + 1]))
        o_t = o_t * lax.rsqrt(jnp.mean(o_t * o_t, axis=0, keepdims=True) + NORM_EPS)
        o_t = o_t * subln_ref[...] * (1.0 - DIFF_LAMBDA_INIT)
        o_ref[0, :, hd * LANES:(hd + 1) * LANES] = o_t.T.astype(BF16)


def _diff_attention(q_t, k, v_t, lam_vecs, subln):
    b, s, _ = k.shape
    n_blk, _, t = v_t.shape[1:]
    const = lambda bi, i: (0, 0)
    return pl.pallas_call(
        functools.partial(_diff_attn_body, tile=t),
        grid=(b, n_blk),
        in_specs=[pl.BlockSpec((4, DIFF_HEAD_DIM), const), pl.BlockSpec((LANES, 1), const),
                  pl.BlockSpec((1, HALF, t), lambda bi, i: (bi, 0, i)),
                  pl.BlockSpec((1, s, HALF), lambda bi, i: (bi, 0, 0)),
                  pl.BlockSpec((1, n_blk, HALF, t), lambda bi, i: (bi, 0, 0, 0))],
        out_specs=pl.BlockSpec((1, t, HALF), lambda bi, i: (bi, i, 0)),
        out_shape=jax.ShapeDtypeStruct((b, s, HALF), BF16),
        scratch_shapes=[pltpu.VMEM((DIFF_HEADS, 1, 2 * t), F32),
                        pltpu.VMEM((DIFF_HEADS, LANES + ONES_ROWS, 2 * t), F32)],
        compiler_params=_params("parallel", "arbitrary"),
        name="diff_attn",
    )(lam_vecs, subln.reshape(LANES, 1), q_t, k, v_t)


def _s5_discretize_body(lr_ref, li_ref, ldt_ref, br_ref, bi_ref, k_ref, pr_ref, pi_ref, bbr_ref, bbi_ref):
    lr, li = lr_ref[...], li_ref[...]
    dt = jnp.exp(ldt_ref[...])
    mag = jnp.exp(lr * dt)
    ar, ai = mag * jnp.cos(li * dt), mag * jnp.sin(li * dt)
    den = lr * lr + li * li
    gr = ((ar - 1.0) * lr + ai * li) / den
    gi = (ai * lr - (ar - 1.0) * li) / den
    br, bi = br_ref[...], bi_ref[...]
    bbr_ref[...] = gr * br - gi * bi
    bbi_ref[...] = gr * bi + gi * br
    steps = k_ref[...]
    mag_k = jnp.exp(lr * dt * steps)
    pr_ref[...] = mag_k * jnp.cos(li * dt * steps)
    pi_ref[...] = mag_k * jnp.sin(li * dt * steps)


def _s5_discretize(a_re, a_im, log_dt, b_re, b_im):
    row = lambda a: a.reshape(1, S5_NSTATE)
    ldt = jnp.broadcast_to(log_dt[:, None], (S5_GROUPS, S5_STATE))
    b_t = lambda a: a.transpose(2, 0, 1).reshape(S5_GROUP, S5_NSTATE)
    steps = jnp.arange(1, S5_CHUNK + 1, dtype=F32).reshape(S5_CHUNK, 1)
    pshape = jax.ShapeDtypeStruct((S5_CHUNK, S5_NSTATE), F32)
    bshape = jax.ShapeDtypeStruct((S5_GROUP, S5_NSTATE), F32)
    return pl.pallas_call(
        _s5_discretize_body,
        out_shape=[pshape, pshape, bshape, bshape],
        name="s5_discretize",
    )(row(a_re), row(a_im), row(ldt), b_t(b_re), b_t(b_im), steps)


def _s5_body(u_ref, pr_ref, pi_ref, bbr_ref, bbi_ref, cr_ref, ci_ref, d_ref, wglu_ref, bglu_ref,
             o_ref, us_ref, ys_ref, xr_ref, xi_ref, hin_ref, carry_ref, *, tile):
    n_chunks = tile // S5_CHUNK
    slabs = HALF // LANES
    gw = S5_NSTATE // slabs

    @pl.when(pl.program_id(1) == 0)
    def _():
        carry_ref[...] = jnp.zeros_like(carry_ref)

    u = u_ref[0]
    for g in range(slabs):
        us_ref[g] = u[:, g * LANES:(g + 1) * LANES]
    u_perm = jnp.concatenate(
        [jnp.concatenate([us_ref[g, pl.ds(st, n_chunks, stride=S5_CHUNK), :] for g in range(slabs)], axis=1)
         for st in range(S5_CHUNK)], axis=0).astype(BF16)

    for g in range(slabs):
        ug = u_perm[:, g * LANES:(g + 1) * LANES]
        xr_ref[:, g * gw:(g + 1) * gw] = jnp.dot(ug, bbr_ref[g], preferred_element_type=F32)
        xi_ref[:, g * gw:(g + 1) * gw] = jnp.dot(ug, bbi_ref[g], preferred_element_type=F32)

    ar, ai = pr_ref[0:1, :], pi_ref[0:1, :]
    hr, hi = xr_ref[0:n_chunks, :], xi_ref[0:n_chunks, :]
    for st in range(1, S5_CHUNK):
        rows = slice(st * n_chunks, (st + 1) * n_chunks)
        hr, hi = ar * hr - ai * hi + xr_ref[rows, :], ar * hi + ai * hr + xi_ref[rows, :]
        xr_ref[rows, :] = hr
        xi_ref[rows, :] = hi

    last = (S5_CHUNK - 1) * n_chunks
    ac_r, ac_i = pr_ref[S5_CHUNK - 1:S5_CHUNK, :], pi_ref[S5_CHUNK - 1:S5_CHUNK, :]

    def carry_step(c, carry):
        cr, ci = carry
        hin_ref[0, pl.ds(c, 1), :] = cr
        hin_ref[1, pl.ds(c, 1), :] = ci
        zr, zi = xr_ref[pl.ds(last + c, 1), :], xi_ref[pl.ds(last + c, 1), :]
        return ac_r * cr - ac_i * ci + zr, ac_r * ci + ac_i * cr + zi

    cr, ci = lax.fori_loop(0, n_chunks, carry_step, (carry_ref[0:1, :], carry_ref[1:2, :]), unroll=4)
    carry_ref[0:1, :] = cr
    carry_ref[1:2, :] = ci

    hin_r, hin_i = hin_ref[0], hin_ref[1]
    for st in range(S5_CHUNK):
        rows = slice(st * n_chunks, (st + 1) * n_chunks)
        p_r, p_i = pr_ref[st:st + 1, :], pi_ref[st:st + 1, :]
        xr_ref[rows, :] = xr_ref[rows, :] + (p_r * hin_r - p_i * hin_i)
        xi_ref[rows, :] = xi_ref[rows, :] + (p_r * hin_i + p_i * hin_r)
    for g in range(slabs):
        cols = slice(g * gw, (g + 1) * gw)
        yg = (jnp.dot(xr_ref[:, cols].astype(BF16), cr_ref[g], preferred_element_type=F32)
              - jnp.dot(xi_ref[:, cols].astype(BF16), ci_ref[g], preferred_element_type=F32))
        for st in range(S5_CHUNK):
            ys_ref[g, pl.ds(st, n_chunks, stride=S5_CHUNK), :] = yg[st * n_chunks:(st + 1) * n_chunks]

    y = jnp.concatenate([ys_ref[g] for g in range(slabs)], axis=1) + d_ref[...] * u
    z = jax.nn.gelu(y, approximate=True)
    gate = jnp.dot(z.astype(BF16), wglu_ref[...], preferred_element_type=F32) + bglu_ref[...]
    o_ref[0] = (z * jax.nn.sigmoid(gate)).astype(BF16)


def _s5(u, a_re, a_im, log_dt, b_re, b_im, c_re, c_im, d_skip, w_glu, b_glu):
    b, s, w = u.shape
    t = min(S5_TILE, s)
    slabs = w // LANES
    per = S5_GROUPS // slabs
    gw = S5_NSTATE // slabs
    pw_r, pw_i, bbr, bbi = _s5_discretize(a_re, a_im, log_dt, b_re, b_im)
    eye = jnp.eye(per, dtype=F32)

    def block_in(m):
        m = m.reshape(S5_GROUP, slabs, per, S5_STATE)
        return jnp.einsum('csgp,gh->sgchp', m, eye).reshape(slabs, LANES, gw).astype(BF16)

    def block_out(m):
        m = m.reshape(slabs, per, S5_GROUP, S5_STATE)
        return jnp.einsum('sgcp,gh->sgphc', m, eye).reshape(slabs, gw, LANES).astype(BF16)

    seq_blk = lambda bi, ti: (bi, ti, 0)
    n_chunks = t // S5_CHUNK
    return pl.pallas_call(
        functools.partial(_s5_body, tile=t),
        grid=(b, s // t),
        in_specs=[pl.BlockSpec((1, t, w), seq_blk),
                  _resident((S5_CHUNK, S5_NSTATE)), _resident((S5_CHUNK, S5_NSTATE)),
                  _resident((slabs, LANES, gw)), _resident((slabs, LANES, gw)),
                  _resident((slabs, gw, LANES)), _resident((slabs, gw, LANES)),
                  _resident((1, w)), _resident((w, w)), _resident((1, w))],
        out_specs=pl.BlockSpec((1, t, w), seq_blk),
        out_shape=jax.ShapeDtypeStruct((b, s, w), BF16),
        scratch_shapes=[pltpu.VMEM((slabs, t, LANES), F32), pltpu.VMEM((slabs, t, LANES), F32),
                        pltpu.VMEM((t, S5_NSTATE), F32), pltpu.VMEM((t, S5_NSTATE), F32),
                        pltpu.VMEM((2, n_chunks, S5_NSTATE), F32), pltpu.VMEM((2, S5_NSTATE), F32)],
        compiler_params=_params("parallel", "arbitrary"),
        name="s5",
    )(u, pw_r, pw_i, block_in(bbr), block_in(bbi),
      block_out(c_re), block_out(c_im), d_skip.reshape(1, w), w_glu.astype(BF16), b_glu.reshape(1, w))


def _proj1_body(x_ref, g_ref, wt_ref, wm_ref, wo_ref, wk_ref, wg_ref,
                qk_ref, vm_ref, om_ref, qft_ref, kf_ref, vft_ref, gt_ref):
    h = _rms(x_ref[...], g_ref[...]).astype(BF16)
    pt = lax.dot_general(wt_ref[...], h, NT_DIMS, preferred_element_type=F32)
    qft_ref[0] = (pt[:HALF] * (FOX_HEAD_DIM ** -0.5 * LOG2_E)).astype(BF16)
    _store_key_blocks(vft_ref, pt[HALF:].astype(BF16))
    pm = jnp.dot(h, wm_ref[...].astype(BF16), preferred_element_type=F32)
    qk_ref[...] = pm[:, 0:HALF]
    vm_ref[...] = pm[:, HALF:2 * HALF].astype(BF16)
    om_ref[...] = jnp.dot(h, wo_ref[...].astype(BF16), preferred_element_type=F32)
    kf_ref[...] = jnp.dot(h, wk_ref[...].astype(BF16), preferred_element_type=F32).astype(BF16)
    gt_ref[...] = jnp.dot(h, wg_ref[...].astype(BF16), preferred_element_type=F32)


def _proj1(x, norm, w_in, seq):
    n, d = x.shape
    b = n // seq
    tm = min(TOKEN_TILE, seq)
    tps = seq // tm
    row = lambda i: (i, 0)
    seq_t = lambda i: (i // tps, 0, i % tps)
    ta = min(ATTN_TILE, seq)
    key_blk = lambda i: (i // tps, i % tps, 0, 0)
    o = 2 * HALF
    g = 2 * MLSTM_HEADS
    om = o + g
    w_o = w_in[:, om:om + HALF]
    w_k = w_in[:, om + 2 * HALF:om + 3 * HALF]
    w_g = jnp.concatenate([w_in[:, o:om], w_in[:, om + 4 * HALF:],
                           jnp.zeros((d, GATE_COLS - g - FOX_HEADS), w_in.dtype)], axis=1)
    w_t = jnp.concatenate([w_in[:, om + HALF:om + 2 * HALF], w_in[:, om + 3 * HALF:om + 4 * HALF]],
                          axis=1).T.astype(BF16)
    f32_half = jax.ShapeDtypeStruct((n, HALF), F32)
    bf_half = jax.ShapeDtypeStruct((n, HALF), BF16)
    t_shape = jax.ShapeDtypeStruct((b, HALF, seq), BF16)
    half_spec = pl.BlockSpec((tm, HALF), row)
    t_spec = pl.BlockSpec((1, HALF, tm), seq_t)
    return pl.pallas_call(
        _proj1_body,
        grid=(n // tm,),
        in_specs=[pl.BlockSpec((tm, d), row), _resident((1, d)), _resident((2 * HALF, d)),
                  pl.BlockSpec((d, o), lambda i: (0, 0), pipeline_mode=pl.Buffered(1)),
                  _resident((d, HALF)), _resident((d, HALF)), _resident((d, GATE_COLS))],
        out_specs=[half_spec, half_spec, half_spec, t_spec, half_spec,
                   pl.BlockSpec((1, tm // ta, HALF, ta), key_blk), pl.BlockSpec((tm, GATE_COLS), row)],
        out_shape=[f32_half, bf_half, f32_half, t_shape, bf_half,
                   jax.ShapeDtypeStruct((b, seq // ta, HALF, ta), BF16),
                   jax.ShapeDtypeStruct((n, GATE_COLS), F32)],
        compiler_params=_params("parallel"),
        name="proj1",
    )(x, norm.reshape(1, d), w_t, w_in, w_o, w_k, w_g)


def _mlstm_body(qk_ref, v_ref, og_ref, gt_ref, cw_ref, cb_ref, gb_ref, on_ref, out_ref,
                xbuf_ref, cn_ref, m_ref, *, chunk):
    pad = SUBLANES
    dk, dv = MLSTM_QK_DIM, MLSTM_V_DIM

    @pl.when(pl.program_id(1) == 0)
    def _():
        xbuf_ref[0:pad, :] = jnp.zeros((pad, HALF), F32)
        cn_ref[...] = jnp.zeros_like(cn_ref)
        m_ref[...] = jnp.zeros_like(m_ref)

    x = qk_ref[0]
    xbuf_ref[pad:pad + chunk, :] = x
    a = jnp.zeros_like(x) + cb_ref[...]
    for tap in range(MLSTM_CONV):
        a = a + cw_ref[tap:tap + 1, :] * xbuf_ref[pl.ds(pad - (MLSTM_CONV - 1) + tap, chunk), :]
    xbuf_ref[0:pad, :] = x[chunk - pad:chunk, :]
    a = a * jax.nn.sigmoid(a)
    q = a[:, :MLSTM_HEADS * dk] * dk ** -0.5
    k = a[:, MLSTM_HEADS * dk:]
    k_t = k.T

    gi = gt_ref[0] + gb_ref[...]
    bcum = _cumsum_rows(_log_sigmoid(gi))
    gi_t = gi.T
    bcum_t = bcum.T
    tri = (lax.broadcasted_iota(jnp.int32, (chunk, chunk), 1)
           <= lax.broadcasted_iota(jnp.int32, (chunk, chunk), 0))
    lane = lax.broadcasted_iota(jnp.int32, (1, LANES), 1)
    v_all = v_ref[0]
    o_gate = og_ref[0]
    ones = jnp.ones((chunk, dv), BF16)

    states = [cn_ref[pair] for pair in range(MLSTM_HEADS // 2)]

    def products(hd):
        pair, odd = hd // 2, hd % 2
        sel = (lane >= dk) if odd else (lane < dk)
        q2 = jnp.where(sel, q[:, pair * LANES:(pair + 1) * LANES], 0.0).astype(BF16)
        k2 = k[:, pair * LANES:(pair + 1) * LANES].astype(BF16)
        qk = lax.dot_general(q2, k2, NT_DIMS, preferred_element_type=F32)
        return qk, jnp.dot(q2, states[pair].astype(BF16), preferred_element_type=F32)

    def gate_weights(hd, prods):
        qk, q_state = prods
        f_lane = MLSTM_HEADS + hd
        b_col = bcum[:, f_lane:f_lane + 1]
        b_row = bcum_t[f_lane:f_lane + 1, :]
        i_row = gi_t[hd:hd + 1, :]
        m_prev = m_ref[hd]
        dm = jnp.where(tri, b_col - b_row + i_row, -jnp.inf)
        g = b_col + m_prev
        mt = jnp.maximum(g, jnp.max(dm, axis=-1, keepdims=True))
        s = (qk * jnp.exp(dm - mt)).astype(BF16)
        w_inter = jnp.exp(g - mt)

        b_last = bcum[chunk - 1:chunk, f_lane:f_lane + 1]
        dec = b_last - b_row + i_row
        m_new = jnp.maximum(b_last + m_prev, jnp.max(dec, axis=-1, keepdims=True))
        k_w = (k_t[hd * dk:(hd + 1) * dk, :] * jnp.exp(dec - m_new)).astype(BF16)
        carry_w = jnp.exp(b_last + m_prev - m_new)
        m_ref[hd] = m_new
        return s, k_w, q_state, w_inter, mt, carry_w

    def outputs(hd, s, k_w, q_state, w_inter, mt, carry_w):
        pair, odd = hd // 2, hd % 2
        v_aug = jnp.concatenate([v_all[:, hd * dv:(hd + 1) * dv], ones], axis=1)
        s_v = jnp.dot(s, v_aug, preferred_element_type=F32)
        num = w_inter * q_state[:, :dv] + s_v[:, :dv]
        den = w_inter * q_state[:, dv:] + s_v[:, dv:]
        h_out = num / jnp.maximum(jnp.abs(den), jnp.exp(-mt))
        rows = slice(odd * dk, (odd + 1) * dk)
        cn_ref[pair, rows, :] = (carry_w * states[pair][rows, :]
                                 + jnp.dot(k_w, v_aug, preferred_element_type=F32))
        cols = slice(hd * dv, (hd + 1) * dv)
        h_norm = _rms(h_out, on_ref[hd:hd + 1, :])
        out_ref[0, :, cols] = (jax.nn.sigmoid(o_gate[:, cols]) * h_norm).astype(BF16)

    _staged(MLSTM_HEADS, products, gate_weights, outputs)


def _mlstm(qk, v, o_gate, gates, conv_w, conv_b, b_i, b_f, out_norm):
    b, s, _ = qk.shape
    c = min(MLSTM_CHUNK, s)
    gate_bias = jnp.concatenate([b_i, b_f, jnp.zeros((GATE_COLS - 2 * MLSTM_HEADS,), F32)]).reshape(1, GATE_COLS)
    seq_blk = lambda bi, ci: (bi, ci, 0)
    half_spec = pl.BlockSpec((1, c, HALF), seq_blk)
    return pl.pallas_call(
        functools.partial(_mlstm_body, chunk=c),
        grid=(b, s // c),
        in_specs=[half_spec, half_spec, half_spec, pl.BlockSpec((1, c, GATE_COLS), seq_blk),
                  _resident((MLSTM_CONV, HALF)), _resident((1, HALF)), _resident((1, GATE_COLS)),
                  _resident((MLSTM_HEADS, MLSTM_V_DIM))],
        out_specs=half_spec,
        out_shape=jax.ShapeDtypeStruct((b, s, HALF), BF16),
        scratch_shapes=[pltpu.VMEM((c + SUBLANES, HALF), F32),
                        pltpu.VMEM((MLSTM_HEADS // 2, 2 * MLSTM_QK_DIM, 2 * MLSTM_V_DIM), F32),
                        pltpu.VMEM((MLSTM_HEADS, 1, 1), F32)],
        compiler_params=_params("parallel", "arbitrary"),
        name="mlstm",
    )(qk, v, o_gate, gates, conv_w, conv_b.reshape(1, HALF), gate_bias, out_norm)


FOX_GATE_LANE = 2 * MLSTM_HEADS
F_TERMS = 3


def _fox_cumsum_body(gt_ref, gb_ref, sel_ref, frow_ref, fk_ref):
    f = _cumsum_rows(_log_sigmoid(gt_ref[0] + gb_ref[...])) * LOG2_E
    hi = f.astype(BF16)
    rest = f - hi.astype(F32)
    mid = rest.astype(BF16)
    lo = (rest - mid.astype(F32)).astype(BF16)
    terms = jnp.concatenate([hi, mid, lo], axis=1)
    fk_ref[0] = jnp.dot(terms, sel_ref[...], preferred_element_type=F32).astype(BF16)
    frow_ref[0] = f.T[FOX_GATE_LANE:FOX_GATE_LANE + FOX_HEADS, :]


def _fox_cumsum(gates, fox_b_f):
    b, s, _ = gates.shape
    bias = jnp.concatenate([jnp.zeros((FOX_GATE_LANE,), F32), fox_b_f,
                            jnp.zeros((GATE_COLS - FOX_GATE_LANE - FOX_HEADS,), F32)]).reshape(1, GATE_COLS)
    src, dst = [], []
    for head in range(FOX_HEADS):
        for term in range(F_TERMS):
            src.append(term * GATE_COLS + FOX_GATE_LANE + head)
            dst.append((head // 2) * LANES + F_TERMS * (head % 2) + term)
    sel = jnp.zeros((F_TERMS * GATE_COLS, HALF), BF16).at[jnp.asarray(src), jnp.asarray(dst)].set(-1.0)
    return pl.pallas_call(
        _fox_cumsum_body,
        grid=(b,),
        in_specs=[pl.BlockSpec((1, s, GATE_COLS), lambda bi: (bi, 0, 0)), _resident((1, GATE_COLS)),
                  _resident((F_TERMS * GATE_COLS, HALF))],
        out_specs=[pl.BlockSpec((1, FOX_HEADS, s), lambda bi: (bi, 0, 0)),
                   pl.BlockSpec((1, s, HALF), lambda bi: (bi, 0, 0))],
        out_shape=[jax.ShapeDtypeStruct((b, FOX_HEADS, s), F32),
                   jax.ShapeDtypeStruct((b, s, HALF), BF16)],
        compiler_params=_params("parallel"),
        name="fox_cumsum",
    )(gates, bias, sel)


def _fox_attn_body(qt_ref, k_ref, fk_ref, vt_ref, frow_ref, o_ref, m_ref, acc_ref, *, tile):
    i = pl.program_id(1)
    dim = lax.broadcasted_iota(jnp.int32, (LANES, 1), 0)
    dh = FOX_HEAD_DIM
    m_ref[...] = jnp.full_like(m_ref, -jnp.inf)
    acc_ref[...] = jnp.zeros_like(acc_ref)
    ones = jnp.ones((ONES_ROWS, tile), BF16)

    def sweep(j, diagonal):
        key0 = pl.multiple_of(j * tile, tile)
        if diagonal:
            query = lax.broadcasted_iota(jnp.int32, (tile, 2 * tile), 1)
            query = jnp.where(query >= tile, query - tile, query)
            keep = lax.broadcasted_iota(jnp.int32, (tile, 2 * tile), 0) <= query

        def scores(pair):
            rows = slice(pair * LANES, (pair + 1) * LANES)
            q_t = qt_ref[0, rows, :]
            k_aug = jnp.concatenate([k_ref[0, pl.ds(key0, tile), rows],
                                     fk_ref[0, pl.ds(key0, tile), rows]], axis=1)
            q_aug = []
            for c in range(2):
                ones_rows = jnp.logical_and(dim >= F_TERMS * c, dim < F_TERMS * (c + 1))
                q_aug.append(jnp.concatenate(
                    [_keep_half_rows(q_t, c),
                     jnp.broadcast_to(jnp.where(ones_rows, 1.0, 0.0), q_t.shape).astype(BF16)], axis=0))
            return jnp.dot(k_aug, jnp.concatenate(q_aug, axis=1), preferred_element_type=F32)

        def weights(pair, s):
            s = s + jnp.concatenate([frow_ref[0, 2 * pair:2 * pair + 1, :],
                                     frow_ref[0, 2 * pair + 1:2 * pair + 2, :]], axis=1)
            if diagonal:
                s = jnp.where(keep, s, -jnp.inf)
            m_old = m_ref[pair]
            m_new = jnp.maximum(m_old, jnp.max(s, axis=0, keepdims=True))
            m_ref[pair] = m_new
            return jnp.exp2(m_old - m_new), jnp.exp2(s - m_new).astype(BF16)

        def accumulate(pair, alpha, p):
            for c in range(2):
                hd = 2 * pair + c
                cols = slice(c * tile, (c + 1) * tile)
                v_aug = jnp.concatenate([vt_ref[0, j, hd * dh:(hd + 1) * dh, :], ones], axis=0)
                acc_ref[hd] = alpha[:, cols] * acc_ref[hd] + jnp.dot(v_aug, p[:, cols], preferred_element_type=F32)

        _staged(FOX_HEADS // 2, scores, weights, accumulate)

    def off_diagonal(j, carry):
        sweep(j, False)
        return carry

    lax.fori_loop(0, i, off_diagonal, 0)
    sweep(i, True)

    for pair in range(FOX_HEADS // 2):
        a0, a1 = acc_ref[2 * pair], acc_ref[2 * pair + 1]
        o_t = jnp.concatenate([a0[:dh] / a0[dh:dh + 1], a1[:dh] / a1[dh:dh + 1]], axis=0)
        o_ref[0, :, pair * LANES:(pair + 1) * LANES] = o_t.T.astype(BF16)


def _fox_attention(q_t, k, fk, v_t, frow):
    b, s, _ = k.shape
    n_blk, _, t = v_t.shape[1:]
    whole = lambda bi, i: (bi, 0, 0)
    return pl.pallas_call(
        functools.partial(_fox_attn_body, tile=t),
        grid=(b, n_blk),
        in_specs=[pl.BlockSpec((1, HALF, t), lambda bi, i: (bi, 0, i)),
                  pl.BlockSpec((1, s, HALF), whole), pl.BlockSpec((1, s, HALF), whole),
                  pl.BlockSpec((1, n_blk, HALF, t), lambda bi, i: (bi, 0, 0, 0)),
                  pl.BlockSpec((1, FOX_HEADS, t), lambda bi, i: (bi, 0, i))],
        out_specs=pl.BlockSpec((1, t, HALF), lambda bi, i: (bi, i, 0)),
        out_shape=jax.ShapeDtypeStruct((b, s, HALF), BF16),
        scratch_shapes=[pltpu.VMEM((FOX_HEADS // 2, 1, 2 * t), F32),
                        pltpu.VMEM((FOX_HEADS, FOX_HEAD_DIM + ONES_ROWS, t), F32)],
        compiler_params=_params("parallel", "arbitrary"),
        name="fox_attn",
    )(q_t, k, fk, v_t, frow)


def kernel(x, l0_ffn1_norm, l0_ffn1_w_gate, l0_ffn1_w_up, l0_ffn1_w_down, l0_mix_norm, l0_w_in, l0_diff_lambda_q1, l0_diff_lambda_k1, l0_diff_lambda_q2, l0_diff_lambda_k2, l0_diff_subln, l0_s5_a_re, l0_s5_a_im, l0_s5_log_dt, l0_s5_b_re, l0_s5_b_im, l0_s5_c_re, l0_s5_c_im, l0_s5_d, l0_s5_w_glu, l0_s5_b_glu, l0_w_out, l0_ffn2_norm, l0_ffn2_w_gate, l0_ffn2_w_up, l0_ffn2_w_down, l1_ffn1_norm, l1_ffn1_w_gate, l1_ffn1_w_up, l1_ffn1_w_down, l1_mix_norm, l1_w_in, l1_mlstm_conv_w, l1_mlstm_conv_b, l1_mlstm_b_i, l1_mlstm_b_f, l1_mlstm_out_norm, l1_fox_b_f, l1_w_out, l1_ffn2_norm, l1_ffn2_w_gate, l1_ffn2_w_up, l1_ffn2_w_down, final_norm):
    b, s, d = x.shape
    n = b * s
    seq = lambda a: a.reshape(b, s, a.shape[-1])
    flat = lambda a: a.reshape(n, a.shape[-1])

    x = _ffn(x.reshape(n, d), l0_ffn1_norm, l0_ffn1_w_gate, l0_ffn1_w_up, l0_ffn1_w_down)
    q_t, k, v_t, u = _proj0(x, l0_mix_norm, l0_w_in, s)
    lam_vecs = jnp.stack([l0_diff_lambda_q1, l0_diff_lambda_k1, l0_diff_lambda_q2, l0_diff_lambda_k2])
    ya = _diff_attention(q_t, seq(k), v_t, lam_vecs, l0_diff_subln)
    yb = _s5(seq(u), l0_s5_a_re, l0_s5_a_im, l0_s5_log_dt, l0_s5_b_re, l0_s5_b_im,
             l0_s5_c_re, l0_s5_c_im, l0_s5_d, l0_s5_w_glu, l0_s5_b_glu)
    x = _ffn(x, l0_ffn2_norm, l0_ffn2_w_gate, l0_ffn2_w_up, l0_ffn2_w_down,
             mix=(flat(ya), flat(yb), l0_w_out[:HALF], l0_w_out[HALF:]))

    x = _ffn(x, l1_ffn1_norm, l1_ffn1_w_gate, l1_ffn1_w_up, l1_ffn1_w_down)
    qk_m, v_m, o_m, qf_t, k_f, vf_t, gates = _proj1(x, l1_mix_norm, l1_w_in, s)
    hm = _mlstm(seq(qk_m), seq(v_m), seq(o_m), seq(gates), l1_mlstm_conv_w, l1_mlstm_conv_b,
                l1_mlstm_b_i, l1_mlstm_b_f, l1_mlstm_out_norm)
    frow, fk = _fox_cumsum(seq(gates), l1_fox_b_f)
    hf = _fox_attention(qf_t, seq(k_f), fk, vf_t, frow)
    x = _ffn(x, l1_ffn2_norm, l1_ffn2_w_gate, l1_ffn2_w_up, l1_ffn2_w_down,
             mix=(flat(hm), flat(hf), l1_w_out[:HALF], l1_w_out[HALF:]), final_norm=final_norm)
    return x.reshape(b, s, d)
```

```python
import functools
import math

import jax
import jax.numpy as jnp
import numpy as np
from jax import lax
from jax.experimental import pallas as pl
from jax.experimental.pallas import tpu as pltpu

F32 = jnp.float32
BF16 = jnp.bfloat16

D_MODEL = 1024
D_FF = 2816
NORM_EPS = 1e-6
ROPE_THETA = 500000.0
DIFF_HEADS = 4
DIFF_HEAD_DIM = 64
ROT_DIM = DIFF_HEAD_DIM // 4
DIFF_LAMBDA_INIT = 0.8 - 0.6 * math.exp(-0.3 * 0)
S5_GROUPS = 32
S5_GROUP = 16
S5_STATE = 64
S5_NSTATE = S5_GROUPS * S5_STATE
MLSTM_HEADS = 4
MLSTM_QK_DIM = 64
MLSTM_V_DIM = 128
MLSTM_CONV = 4
FOX_HEADS = 8
FOX_HEAD_DIM = 64
HALF = 512

LANES = 128
SUBLANES = 8
VMEM_LIMIT_BYTES = 60 * 1024 * 1024

TOKEN_TILE = 1024
FFN_TILE = 512
FF_CHUNK = 256
ATTN_TILE = 512
S5_TILE = 1024
S5_CHUNK = 16
MLSTM_CHUNK = 512
GATE_COLS = LANES

NT_DIMS = (((1,), (1,)), ((), ()))
LOG2_E = math.log2(math.e)
ONES_ROWS = 16


def _params(*semantics):
    return pltpu.CompilerParams(dimension_semantics=semantics, vmem_limit_bytes=VMEM_LIMIT_BYTES)


def _resident(shape):
    return pl.BlockSpec(shape, lambda *_: (0,) * len(shape), pipeline_mode=pl.Buffered(1))


def _rms(x, g):
    return x * lax.rsqrt(jnp.mean(x * x, axis=-1, keepdims=True) + NORM_EPS) * g


def _log_sigmoid(x):
    return jnp.minimum(x, 0.0) - jnp.log1p(jnp.exp(-jnp.abs(x)))


def _cumsum_rows(x):
    n = x.shape[0]
    row = lax.broadcasted_iota(jnp.int32, x.shape, 0)
    k = 1
    while k < n:
        x = x + jnp.where(row >= k, pltpu.roll(x, k, 0), 0.0)
        k *= 2
    return x


def _keep_half_rows(x, half):
    rows = x.shape[0] // 2
    zeros = jnp.zeros((rows,) + x.shape[1:], x.dtype)
    return jnp.concatenate([x[:rows], zeros] if half == 0 else [zeros, x[rows:]], axis=0)


def _staged(n, scores, weights, accumulate):
    s, w = {}, {}
    for step in range(n + 2):
        if step < n:
            s[step] = scores(step)
        if step >= 2:
            accumulate(step - 2, *w.pop(step - 2))
        if 1 <= step <= n:
            w[step - 1] = weights(step - 1, s.pop(step - 1))


def _store_key_blocks(ref, value):
    width = ref.shape[3]
    for blk in range(ref.shape[1]):
        ref[0, blk] = value[:, blk * width:(blk + 1) * width]


def _ffn_body(*refs, has_mix, has_final):
    refs = list(refs)
    x_ref = refs.pop(0)
    if has_mix:
        a_ref, b_ref, wa_ref, wb_ref = refs[:4]
        refs = refs[4:]
    g_ref, wg_hbm, wu_hbm, wd_hbm = refs[:4]
    refs = refs[4:]
    if has_final:
        gf_ref = refs.pop(0)
    o_ref, wg_ref, wu_ref, wd_ref, sem = refs

    n_chunks = D_FF // FF_CHUNK
    chunk = lambda c: slice(c * FF_CHUNK, (c + 1) * FF_CHUNK)

    def chunk_copies(c):
        return (pltpu.make_async_copy(wg_hbm.at[:, chunk(c)], wg_ref.at[:, chunk(c)], sem.at[0, c]),
                pltpu.make_async_copy(wu_hbm.at[:, chunk(c)], wu_ref.at[:, chunk(c)], sem.at[1, c]),
                pltpu.make_async_copy(wd_hbm.at[chunk(c), :], wd_ref.at[chunk(c), :], sem.at[2, c]))

    first = pl.program_id(0) == 0

    @pl.when(first)
    def _():
        for c in range(n_chunks):
            for copy in chunk_copies(c):
                copy.start()

    def tile(wait_for_weights):
        x = x_ref[...]
        if has_mix:
            x = x + jnp.dot(a_ref[...], wa_ref[...], preferred_element_type=F32)
            x = x + jnp.dot(b_ref[...], wb_ref[...], preferred_element_type=F32)
        h = _rms(x, g_ref[...]).astype(BF16)
        acc = jnp.zeros_like(x)
        for c in range(n_chunks):
            if wait_for_weights:
                for copy in chunk_copies(c):
                    copy.wait()
            gate = jnp.dot(h, wg_ref[:, chunk(c)].astype(BF16), preferred_element_type=F32)
            up = jnp.dot(h, wu_ref[:, chunk(c)].astype(BF16), preferred_element_type=F32)
            act = (gate * jax.nn.sigmoid(gate) * up).astype(BF16)
            acc = acc + jnp.dot(act, wd_ref[chunk(c), :].astype(BF16), preferred_element_type=F32)
        y = x + 0.5 * acc
        if has_final:
            y = _rms(y, gf_ref[...])
        o_ref[...] = y

    @pl.when(first)
    def _():
        tile(True)

    @pl.when(jnp.logical_not(first))
    def _():
        tile(False)


def _ffn(x, norm, wg, wu, wd, mix=None, final_norm=None):
    n, d = x.shape
    tm = min(FFN_TILE, n)
    row = lambda i: (i, 0)
    args = [x]
    specs = [pl.BlockSpec((tm, d), row)]
    if mix is not None:
        a, b, wa, wb = mix
        args += [a, b, wa.astype(BF16), wb.astype(BF16)]
        specs += [pl.BlockSpec((tm, HALF), row), pl.BlockSpec((tm, HALF), row),
                  _resident((HALF, d)), _resident((HALF, d))]
    args += [norm.reshape(1, d), wg, wu, wd]
    specs += [_resident((1, d))] + [pl.BlockSpec(memory_space=pl.ANY)] * 3
    if final_norm is not None:
        args.append(final_norm.reshape(1, d))
        specs.append(_resident((1, d)))
    body = functools.partial(_ffn_body, has_mix=mix is not None, has_final=final_norm is not None)
    return pl.pallas_call(
        body,
        grid=(n // tm,),
        in_specs=specs,
        out_specs=pl.BlockSpec((tm, d), row),
        out_shape=jax.ShapeDtypeStruct((n, d), F32),
        scratch_shapes=[pltpu.VMEM((d, D_FF), F32), pltpu.VMEM((d, D_FF), F32), pltpu.VMEM((D_FF, d), F32),
                        pltpu.SemaphoreType.DMA((3, D_FF // FF_CHUNK))],
        compiler_params=_params("arbitrary"),
        name="ffn",
    )(*args)


def _rope_angles(seq):
    pos = np.arange(seq, dtype=np.float32)
    inv = np.float32(ROPE_THETA) ** (-np.arange(0, ROT_DIM, 2, dtype=np.float32) / np.float32(ROT_DIM))
    ang = pos[:, None] * inv[None, :]
    return np.cos(ang), np.sin(ang)


def _rope_lane_tables(seq):
    half = ROT_DIM // 2
    cos, sin = _rope_angles(seq)
    one = np.ones((seq, DIFF_HEAD_DIM - ROT_DIM), np.float32)
    zero8 = np.zeros((seq, half), np.float32)
    zero48 = np.zeros((seq, DIFF_HEAD_DIM - ROT_DIM), np.float32)
    cos_t = np.concatenate([cos, cos, one] * 2, axis=1)
    sa_t = np.concatenate([-sin, zero8, zero48] * 2, axis=1)
    sb_t = np.concatenate([zero8, sin, zero48] * 2, axis=1)
    return cos_t, sa_t, sb_t


def _proj0_body(x_ref, g_ref, wt_ref, wk_ref, wu_ref, cos_ref, sa_ref, sb_ref, cosr_ref, sinr_ref,
                qt_ref, k_ref, vt_ref, u_ref):
    h = _rms(x_ref[...], g_ref[...]).astype(BF16)
    pt = lax.dot_general(wt_ref[...], h, NT_DIMS, preferred_element_type=F32)
    pk = jnp.dot(h, wk_ref[...].astype(BF16), preferred_element_type=F32)
    half = ROT_DIM // 2
    scale = DIFF_HEAD_DIM ** -0.5 * LOG2_E

    cos_r, sin_r = cosr_ref[...], sinr_ref[...]
    for comp in range(2 * DIFF_HEADS):
        r0 = comp * DIFF_HEAD_DIM
        x1, x2 = pt[r0:r0 + half], pt[r0 + half:r0 + ROT_DIM]
        qt_ref[0, r0:r0 + half, :] = ((x1 * cos_r - x2 * sin_r) * scale).astype(BF16)
        qt_ref[0, r0 + half:r0 + ROT_DIM, :] = ((x2 * cos_r + x1 * sin_r) * scale).astype(BF16)
        qt_ref[0, r0 + ROT_DIM:r0 + DIFF_HEAD_DIM, :] = (pt[r0 + ROT_DIM:r0 + DIFF_HEAD_DIM] * scale).astype(BF16)
    _store_key_blocks(vt_ref, pt[HALF:2 * HALF].astype(BF16))

    cos_t, sa_t, sb_t = cos_ref[...], sa_ref[...], sb_ref[...]
    for hd in range(DIFF_HEADS):
        sl = slice(hd * LANES, (hd + 1) * LANES)
        t = pk[:, sl]
        k_ref[:, sl] = (t * cos_t + pltpu.roll(t, LANES - half, 1) * sa_t
                        + pltpu.roll(t, half, 1) * sb_t).astype(BF16)
    u_ref[...] = jnp.dot(h, wu_ref[...].astype(BF16), preferred_element_type=F32)


def _proj0(x, norm, w_in, seq):
    n, d = x.shape
    b = n // seq
    tm = min(TOKEN_TILE, seq)
    tps = seq // tm
    row = lambda i: (i, 0)
    pos = lambda i: (i % tps, 0)
    pos_t = lambda i: (0, i % tps)
    seq_t = lambda i: (i // tps, 0, i % tps)
    ta = min(ATTN_TILE, seq)
    key_blk = lambda i: (i // tps, i % tps, 0, 0)
    cos, sin = _rope_angles(seq)
    w_t = jnp.concatenate([w_in[:, :HALF], w_in[:, 2 * HALF:3 * HALF]], axis=1).T.astype(BF16)
    w_cols = lambda blk: pl.BlockSpec((d, HALF), lambda i: (0, blk), pipeline_mode=pl.Buffered(1))
    t_shape = jax.ShapeDtypeStruct((b, HALF, seq), BF16)
    return pl.pallas_call(
        _proj0_body,
        grid=(n // tm,),
        in_specs=[pl.BlockSpec((tm, d), row), _resident((1, d)), _resident((2 * HALF, d)),
                  w_cols(1), w_cols(3),
                  pl.BlockSpec((tm, LANES), pos), pl.BlockSpec((tm, LANES), pos),
                  pl.BlockSpec((tm, LANES), pos),
                  pl.BlockSpec((ROT_DIM // 2, tm), pos_t), pl.BlockSpec((ROT_DIM // 2, tm), pos_t)],
        out_specs=[pl.BlockSpec((1, HALF, tm), seq_t), pl.BlockSpec((tm, HALF), row),
                   pl.BlockSpec((1, tm // ta, HALF, ta), key_blk), pl.BlockSpec((tm, HALF), row)],
        out_shape=[t_shape, jax.ShapeDtypeStruct((n, HALF), BF16),
                   jax.ShapeDtypeStruct((b, seq // ta, HALF, ta), BF16),
                   jax.ShapeDtypeStruct((n, HALF), F32)],
        compiler_params=_params("parallel"),
        name="proj0",
    )(x, norm.reshape(1, d), w_t, w_in, w_in, *_rope_lane_tables(seq), cos.T, sin.T)


def _diff_attn_body(lam_ref, subln_ref, qt_ref, k_ref, vt_ref, o_ref, m_ref, acc_ref, *, tile):
    i = pl.program_id(1)
    m_ref[...] = jnp.full_like(m_ref, -jnp.inf)
    acc_ref[...] = jnp.zeros_like(acc_ref)
    ones = jnp.ones((ONES_ROWS, tile), BF16)

    def sweep(blocks):
        if any(diagonal for _, diagonal in blocks):
            query = lax.broadcasted_iota(jnp.int32, (tile, 2 * tile), 1)
            query = jnp.where(query >= tile, query - tile, query)
            keep = lax.broadcasted_iota(jnp.int32, (tile, 2 * tile), 0) <= query

        def scores(u):
            blk, hd = divmod(u, DIFF_HEADS)
            key0 = pl.multiple_of(blocks[blk][0] * tile, tile)
            rows = slice(hd * LANES, (hd + 1) * LANES)
            q_t = qt_ref[0, rows, :]
            q_both = jnp.concatenate([_keep_half_rows(q_t, 0), _keep_half_rows(q_t, 1)], axis=1)
            return jnp.dot(k_ref[0, pl.ds(key0, tile), rows], q_both, preferred_element_type=F32)

        def weights(u, s):
            blk, hd = divmod(u, DIFF_HEADS)
            if blocks[blk][1]:
                s = jnp.where(keep, s, -jnp.inf)
            m_old = m_ref[hd]
            m_new = jnp.maximum(m_old, jnp.max(s, axis=0, keepdims=True))
            m_ref[hd] = m_new
            return jnp.exp2(m_old - m_new), jnp.exp2(s - m_new).astype(BF16)

        def accumulate(u, alpha, p):
            blk, hd = divmod(u, DIFF_HEADS)
            v_aug = jnp.concatenate([vt_ref[0, blocks[blk][0], hd * LANES:(hd + 1) * LANES, :], ones], axis=0)
            acc_ref[hd] = alpha * acc_ref[hd] + jnp.dot(v_aug, p, preferred_element_type=F32)

        _staged(DIFF_HEADS * len(blocks), scores, weights, accumulate)

    def two_blocks(jj, carry):
        sweep([(2 * jj, False), (2 * jj + 1, False)])
        return carry

    lax.fori_loop(0, lax.shift_right_logical(i, 1), two_blocks, 0)
    odd = lax.bitwise_and(i, 1) == 1

    @pl.when(odd)
    def _():
        sweep([(i - 1, False), (i, True)])

    @pl.when(jnp.logical_not(odd))
    def _():
        sweep([(i, True)])

    lam_v = lam_ref[...]
    lam = (jnp.exp(jnp.sum(lam_v[0:1] * lam_v[1:2], keepdims=True))
           - jnp.exp(jnp.sum(lam_v[2:3] * lam_v[3:4], keepdims=True)) + DIFF_LAMBDA_INIT)
    for hd in range(DIFF_HEADS):
        a1, a2 = acc_ref[hd, :, :tile], acc_ref[hd, :, tile:]
        o_t = (a1[:LANES] / a1[LANES:LANES + 1] - lam * (a2[:LANES] / a2[LANES:LANES + 1]))
        o_t = o_t * lax.rsqrt(jnp.mean(o_t * o_t, axis=0, keepdims=True) + NORM_EPS)
        o_t = o_t * subln_ref[...] * (1.0 - DIFF_LAMBDA_INIT)
        o_ref[0, :, hd * LANES:(hd + 1) * LANES] = o_t.T.astype(BF16)


def _diff_attention(q_t, k, v_t, lam_vecs, subln):
    b, s, _ = k.shape
    n_blk, _, t = v_t.shape[1:]
    const = lambda bi, i: (0, 0)
    return pl.pallas_call(
        functools.partial(_diff_attn_body, tile=t),
        grid=(b, n_blk),
        in_specs=[pl.BlockSpec((4, DIFF_HEAD_DIM), const), pl.BlockSpec((LANES, 1), const),
                  pl.BlockSpec((1, HALF, t), lambda bi, i: (bi, 0, i)),
                  pl.BlockSpec((1, s, HALF), lambda bi, i: (bi, 0, 0)),
                  pl.BlockSpec((1, n_blk, HALF, t), lambda bi, i: (bi, 0, 0, 0))],
        out_specs=pl.BlockSpec((1, t, HALF), lambda bi, i: (bi, i, 0)),
        out_shape=jax.ShapeDtypeStruct((b, s, HALF), BF16),
        scratch_shapes=[pltpu.VMEM((DIFF_HEADS, 1, 2 * t), F32),
                        pltpu.VMEM((DIFF_HEADS, LANES + ONES_ROWS, 2 * t), F32)],
        compiler_params=_params("parallel", "arbitrary"),
        name="diff_attn",
    )(lam_vecs, subln.reshape(LANES, 1), q_t, k, v_t)


def _s5_discretize_body(lr_ref, li_ref, ldt_ref, br_ref, bi_ref, k_ref, pr_ref, pi_ref, bbr_ref, bbi_ref):
    lr, li = lr_ref[...], li_ref[...]
    dt = jnp.exp(ldt_ref[...])
    mag = jnp.exp(lr * dt)
    ar, ai = mag * jnp.cos(li * dt), mag * jnp.sin(li * dt)
    den = lr * lr + li * li
    gr = ((ar - 1.0) * lr + ai * li) / den
    gi = (ai * lr - (ar - 1.0) * li) / den
    br, bi = br_ref[...], bi_ref[...]
    bbr_ref[...] = gr * br - gi * bi
    bbi_ref[...] = gr * bi + gi * br
    steps = k_ref[...]
    mag_k = jnp.exp(lr * dt * steps)
    pr_ref[...] = mag_k * jnp.cos(li * dt * steps)
    pi_ref[...] = mag_k * jnp.sin(li * dt * steps)


def _s5_discretize(a_re, a_im, log_dt, b_re, b_im):
    row = lambda a: a.reshape(1, S5_NSTATE)
    ldt = jnp.broadcast_to(log_dt[:, None], (S5_GROUPS, S5_STATE))
    b_t = lambda a: a.transpose(2, 0, 1).reshape(S5_GROUP, S5_NSTATE)
    steps = jnp.arange(1, S5_CHUNK + 1, dtype=F32).reshape(S5_CHUNK, 1)
    pshape = jax.ShapeDtypeStruct((S5_CHUNK, S5_NSTATE), F32)
    bshape = jax.ShapeDtypeStruct((S5_GROUP, S5_NSTATE), F32)
    return pl.pallas_call(
        _s5_discretize_body,
        out_shape=[pshape, pshape, bshape, bshape],
        name="s5_discretize",
    )(row(a_re), row(a_im), row(ldt), b_t(b_re), b_t(b_im), steps)


def _s5_body(u_ref, pr_ref, pi_ref, bbr_ref, bbi_ref, cr_ref, ci_ref, d_ref, wglu_ref, bglu_ref,
             o_ref, us_ref, ys_ref, xr_ref, xi_ref, hin_ref, carry_ref, *, tile):
    n_chunks = tile // S5_CHUNK
    slabs = HALF // LANES
    gw = S5_NSTATE // slabs

    @pl.when(pl.program_id(1) == 0)
    def _():
        carry_ref[...] = jnp.zeros_like(carry_ref)

    u = u_ref[0]
    for g in range(slabs):
        us_ref[g] = u[:, g * LANES:(g + 1) * LANES]
    u_perm = jnp.concatenate(
        [jnp.concatenate([us_ref[g, pl.ds(st, n_chunks, stride=S5_CHUNK), :] for g in range(slabs)], axis=1)
         for st in range(S5_CHUNK)], axis=0).astype(BF16)

    for g in range(slabs):
        ug = u_perm[:, g * LANES:(g + 1) * LANES]
        xr_ref[:, g * gw:(g + 1) * gw] = jnp.dot(ug, bbr_ref[g], preferred_element_type=F32)
        xi_ref[:, g * gw:(g + 1) * gw] = jnp.dot(ug, bbi_ref[g], preferred_element_type=F32)

    ar, ai = pr_ref[0:1, :], pi_ref[0:1, :]
    hr, hi = xr_ref[0:n_chunks, :], xi_ref[0:n_chunks, :]
    for st in range(1, S5_CHUNK):
        rows = slice(st * n_chunks, (st + 1) * n_chunks)
        hr, hi = ar * hr - ai * hi + xr_ref[rows, :], ar * hi + ai * hr + xi_ref[rows, :]
        xr_ref[rows, :] = hr
        xi_ref[rows, :] = hi

    last = (S5_CHUNK - 1) * n_chunks
    ac_r, ac_i = pr_ref[S5_CHUNK - 1:S5_CHUNK, :], pi_ref[S5_CHUNK - 1:S5_CHUNK, :]

    def carry_step(c, carry):
        cr, ci = carry
        hin_ref[0, pl.ds(c, 1), :] = cr
        hin_ref[1, pl.ds(c, 1), :] = ci
        zr, zi = xr_ref[pl.ds(last + c, 1), :], xi_ref[pl.ds(last + c, 1), :]
        return ac_r * cr - ac_i * ci + zr, ac_r * ci + ac_i * cr + zi

    cr, ci = lax.fori_loop(0, n_chunks, carry_step, (carry_ref[0:1, :], carry_ref[1:2, :]), unroll=4)
    carry_ref[0:1, :] = cr
    carry_ref[1:2, :] = ci

    hin_r, hin_i = hin_ref[0], hin_ref[1]
    for st in range(S5_CHUNK):
        rows = slice(st * n_chunks, (st + 1) * n_chunks)
        p_r, p_i = pr_ref[st:st + 1, :], pi_ref[st:st + 1, :]
        xr_ref[rows, :] = xr_ref[rows, :] + (p_r * hin_r - p_i * hin_i)
        xi_ref[rows, :] = xi_ref[rows, :] + (p_r * hin_i + p_i * hin_r)
    for g in range(slabs):
        cols = slice(g * gw, (g + 1) * gw)
        yg = (jnp.dot(xr_ref[:, cols].astype(BF16), cr_ref[g], preferred_element_type=F32)
              - jnp.dot(xi_ref[:, cols].astype(BF16), ci_ref[g], preferred_element_type=F32))
        for st in range(S5_CHUNK):
            ys_ref[g, pl.ds(st, n_chunks, stride=S5_CHUNK), :] = yg[st * n_chunks:(st + 1) * n_chunks]

    y = jnp.concatenate([ys_ref[g] for g in range(slabs)], axis=1) + d_ref[...] * u
    z = jax.nn.gelu(y, approximate=True)
    gate = jnp.dot(z.astype(BF16), wglu_ref[...], preferred_element_type=F32) + bglu_ref[...]
    o_ref[0] = (z * jax.nn.sigmoid(gate)).astype(BF16)


def _s5(u, a_re, a_im, log_dt, b_re, b_im, c_re, c_im, d_skip, w_glu, b_glu):
    b, s, w = u.shape
    t = min(S5_TILE, s)
    slabs = w // LANES
    per = S5_GROUPS // slabs
    gw = S5_NSTATE // slabs
    pw_r, pw_i, bbr, bbi = _s5_discretize(a_re, a_im, log_dt, b_re, b_im)
    eye = jnp.eye(per, dtype=F32)

    def block_in(m):
        m = m.reshape(S5_GROUP, slabs, per, S5_STATE)
        return jnp.einsum('csgp,gh->sgchp', m, eye).reshape(slabs, LANES, gw).astype(BF16)

    def block_out(m):
        m = m.reshape(slabs, per, S5_GROUP, S5_STATE)
        return jnp.einsum('sgcp,gh->sgphc', m, eye).reshape(slabs, gw, LANES).astype(BF16)

    seq_blk = lambda bi, ti: (bi, ti, 0)
    n_chunks = t // S5_CHUNK
    return pl.pallas_call(
        functools.partial(_s5_body, tile=t),
        grid=(b, s // t),
        in_specs=[pl.BlockSpec((1, t, w), seq_blk),
                  _resident((S5_CHUNK, S5_NSTATE)), _resident((S5_CHUNK, S5_NSTATE)),
                  _resident((slabs, LANES, gw)), _resident((slabs, LANES, gw)),
                  _resident((slabs, gw, LANES)), _resident((slabs, gw, LANES)),
                  _resident((1, w)), _resident((w, w)), _resident((1, w))],
        out_specs=pl.BlockSpec((1, t, w), seq_blk),
        out_shape=jax.ShapeDtypeStruct((b, s, w), BF16),
        scratch_shapes=[pltpu.VMEM((slabs, t, LANES), F32), pltpu.VMEM((slabs, t, LANES), F32),
                        pltpu.VMEM((t, S5_NSTATE), F32), pltpu.VMEM((t, S5_NSTATE), F32),
                        pltpu.VMEM((2, n_chunks, S5_NSTATE), F32), pltpu.VMEM((2, S5_NSTATE), F32)],
        compiler_params=_params("parallel", "arbitrary"),
        name="s5",
    )(u, pw_r, pw_i, block_in(bbr), block_in(bbi),
      block_out(c_re), block_out(c_im), d_skip.reshape(1, w), w_glu.astype(BF16), b_glu.reshape(1, w))


def _proj1_body(x_ref, g_ref, wt_ref, wm_ref, wo_ref, wk_ref, wg_ref,
                qk_ref, vm_ref, om_ref, qft_ref, kf_ref, vft_ref, gt_ref):
    h = _rms(x_ref[...], g_ref[...]).astype(BF16)
    pt = lax.dot_general(wt_ref[...], h, NT_DIMS, preferred_element_type=F32)
    qft_ref[0] = (pt[:HALF] * (FOX_HEAD_DIM ** -0.5 * LOG2_E)).astype(BF16)
    _store_key_blocks(vft_ref, pt[HALF:].astype(BF16))
    pm = jnp.dot(h, wm_ref[...].astype(BF16), preferred_element_type=F32)
    qk_ref[...] = pm[:, 0:HALF]
    vm_ref[...] = pm[:, HALF:2 * HALF].astype(BF16)
    om_ref[...] = jnp.dot(h, wo_ref[...].astype(BF16), preferred_element_type=F32)
    kf_ref[...] = jnp.dot(h, wk_ref[...].astype(BF16), preferred_element_type=F32).astype(BF16)
    gt_ref[...] = jnp.dot(h, wg_ref[...].astype(BF16), preferred_element_type=F32)


def _proj1(x, norm, w_in, seq):
    n, d = x.shape
    b = n // seq
    tm = min(TOKEN_TILE, seq)
    tps = seq // tm
    row = lambda i: (i, 0)
    seq_t = lambda i: (i // tps, 0, i % tps)
    ta = min(ATTN_TILE, seq)
    key_blk = lambda i: (i // tps, i % tps, 0, 0)
    o = 2 * HALF
    g = 2 * MLSTM_HEADS
    om = o + g
    w_o = w_in[:, om:om + HALF]
    w_k = w_in[:, om + 2 * HALF:om + 3 * HALF]
    w_g = jnp.concatenate([w_in[:, o:om], w_in[:, om + 4 * HALF:],
                           jnp.zeros((d, GATE_COLS - g - FOX_HEADS), w_in.dtype)], axis=1)
    w_t = jnp.concatenate([w_in[:, om + HALF:om + 2 * HALF], w_in[:, om + 3 * HALF:om + 4 * HALF]],
                          axis=1).T.astype(BF16)
    f32_half = jax.ShapeDtypeStruct((n, HALF), F32)
    bf_half = jax.ShapeDtypeStruct((n, HALF), BF16)
    t_shape = jax.ShapeDtypeStruct((b, HALF, seq), BF16)
    half_spec = pl.BlockSpec((tm, HALF), row)
    t_spec = pl.BlockSpec((1, HALF, tm), seq_t)
    return pl.pallas_call(
        _proj1_body,
        grid=(n // tm,),
        in_specs=[pl.BlockSpec((tm, d), row), _resident((1, d)), _resident((2 * HALF, d)),
                  pl.BlockSpec((d, o), lambda i: (0, 0), pipeline_mode=pl.Buffered(1)),
                  _resident((d, HALF)), _resident((d, HALF)), _resident((d, GATE_COLS))],
        out_specs=[half_spec, half_spec, half_spec, t_spec, half_spec,
                   pl.BlockSpec((1, tm // ta, HALF, ta), key_blk), pl.BlockSpec((tm, GATE_COLS), row)],
        out_shape=[f32_half, bf_half, f32_half, t_shape, bf_half,
                   jax.ShapeDtypeStruct((b, seq // ta, HALF, ta), BF16),
                   jax.ShapeDtypeStruct((n, GATE_COLS), F32)],
        compiler_params=_params("parallel"),
        name="proj1",
    )(x, norm.reshape(1, d), w_t, w_in, w_o, w_k, w_g)


def _mlstm_body(qk_ref, v_ref, og_ref, gt_ref, cw_ref, cb_ref, gb_ref, on_ref, out_ref,
                xbuf_ref, cn_ref, m_ref, *, chunk):
    pad = SUBLANES
    dk, dv = MLSTM_QK_DIM, MLSTM_V_DIM

    @pl.when(pl.program_id(1) == 0)
    def _():
        xbuf_ref[0:pad, :] = jnp.zeros((pad, HALF), F32)
        cn_ref[...] = jnp.zeros_like(cn_ref)
        m_ref[...] = jnp.zeros_like(m_ref)

    x = qk_ref[0]
    xbuf_ref[pad:pad + chunk, :] = x
    a = jnp.zeros_like(x) + cb_ref[...]
    for tap in range(MLSTM_CONV):
        a = a + cw_ref[tap:tap + 1, :] * xbuf_ref[pl.ds(pad - (MLSTM_CONV - 1) + tap, chunk), :]
    xbuf_ref[0:pad, :] = x[chunk - pad:chunk, :]
    a = a * jax.nn.sigmoid(a)
    q = a[:, :MLSTM_HEADS * dk] * dk ** -0.5
    k = a[:, MLSTM_HEADS * dk:]
    k_t = k.T

    gi = gt_ref[0] + gb_ref[...]
    bcum = _cumsum_rows(_log_sigmoid(gi))
    gi_t = gi.T
    bcum_t = bcum.T
    tri = (lax.broadcasted_iota(jnp.int32, (chunk, chunk), 1)
           <= lax.broadcasted_iota(jnp.int32, (chunk, chunk), 0))
    lane = lax.broadcasted_iota(jnp.int32, (1, LANES), 1)
    v_all = v_ref[0]
    o_gate = og_ref[0]
    ones = jnp.ones((chunk, dv), BF16)

    states = [cn_ref[pair] for pair in range(MLSTM_HEADS // 2)]

    def products(hd):
        pair, odd = hd // 2, hd % 2
        sel = (lane >= dk) if odd else (lane < dk)
        q2 = jnp.where(sel, q[:, pair * LANES:(pair + 1) * LANES], 0.0).astype(BF16)
        k2 = k[:, pair * LANES:(pair + 1) * LANES].astype(BF16)
        qk = lax.dot_general(q2, k2, NT_DIMS, preferred_element_type=F32)
        return qk, jnp.dot(q2, states[pair].astype(BF16), preferred_element_type=F32)

    def gate_weights(hd, prods):
        qk, q_state = prods
        f_lane = MLSTM_HEADS + hd
        b_col = bcum[:, f_lane:f_lane + 1]
        b_row = bcum_t[f_lane:f_lane + 1, :]
        i_row = gi_t[hd:hd + 1, :]
        m_prev = m_ref[hd]
        dm = jnp.where(tri, b_col - b_row + i_row, -jnp.inf)
        g = b_col + m_prev
        mt = jnp.maximum(g, jnp.max(dm, axis=-1, keepdims=True))
        s = (qk * jnp.exp(dm - mt)).astype(BF16)
        w_inter = jnp.exp(g - mt)

        b_last = bcum[chunk - 1:chunk, f_lane:f_lane + 1]
        dec = b_last - b_row + i_row
        m_new = jnp.maximum(b_last + m_prev, jnp.max(dec, axis=-1, keepdims=True))
        k_w = (k_t[hd * dk:(hd + 1) * dk, :] * jnp.exp(dec - m_new)).astype(BF16)
        carry_w = jnp.exp(b_last + m_prev - m_new)
        m_ref[hd] = m_new
        return s, k_w, q_state, w_inter, mt, carry_w

    def outputs(hd, s, k_w, q_state, w_inter, mt, carry_w):
        pair, odd = hd // 2, hd % 2
        v_aug = jnp.concatenate([v_all[:, hd * dv:(hd + 1) * dv], ones], axis=1)
        s_v = jnp.dot(s, v_aug, preferred_element_type=F32)
        num = w_inter * q_state[:, :dv] + s_v[:, :dv]
        den = w_inter * q_state[:, dv:] + s_v[:, dv:]
        h_out = num / jnp.maximum(jnp.abs(den), jnp.exp(-mt))
        rows = slice(odd * dk, (odd + 1) * dk)
        cn_ref[pair, rows, :] = (carry_w * states[pair][rows, :]
                                 + jnp.dot(k_w, v_aug, preferred_element_type=F32))
        cols = slice(hd * dv, (hd + 1) * dv)
        h_norm = _rms(h_out, on_ref[hd:hd + 1, :])
        out_ref[0, :, cols] = (jax.nn.sigmoid(o_gate[:, cols]) * h_norm).astype(BF16)

    _staged(MLSTM_HEADS, products, gate_weights, outputs)


def _mlstm(qk, v, o_gate, gates, conv_w, conv_b, b_i, b_f, out_norm):
    b, s, _ = qk.shape
    c = min(MLSTM_CHUNK, s)
    gate_bias = jnp.concatenate([b_i, b_f, jnp.zeros((GATE_COLS - 2 * MLSTM_HEADS,), F32)]).reshape(1, GATE_COLS)
    seq_blk = lambda bi, ci: (bi, ci, 0)
    half_spec = pl.BlockSpec((1, c, HALF), seq_blk)
    return pl.pallas_call(
        functools.partial(_mlstm_body, chunk=c),
        grid=(b, s // c),
        in_specs=[half_spec, half_spec, half_spec, pl.BlockSpec((1, c, GATE_COLS), seq_blk),
                  _resident((MLSTM_CONV, HALF)), _resident((1, HALF)), _resident((1, GATE_COLS)),
                  _resident((MLSTM_HEADS, MLSTM_V_DIM))],
        out_specs=half_spec,
        out_shape=jax.ShapeDtypeStruct((b, s, HALF), BF16),
        scratch_shapes=[pltpu.VMEM((c + SUBLANES, HALF), F32),
                        pltpu.VMEM((MLSTM_HEADS // 2, 2 * MLSTM_QK_DIM, 2 * MLSTM_V_DIM), F32),
                        pltpu.VMEM((MLSTM_HEADS, 1, 1), F32)],
        compiler_params=_params("parallel", "arbitrary"),
        name="mlstm",
    )(qk, v, o_gate, gates, conv_w, conv_b.reshape(1, HALF), gate_bias, out_norm)


FOX_GATE_LANE = 2 * MLSTM_HEADS
F_TERMS = 3


def _fox_cumsum_body(gt_ref, gb_ref, sel_ref, frow_ref, fk_ref):
    f = _cumsum_rows(_log_sigmoid(gt_ref[0] + gb_ref[...])) * LOG2_E
    hi = f.astype(BF16)
    rest = f - hi.astype(F32)
    mid = rest.astype(BF16)
    lo = (rest - mid.astype(F32)).astype(BF16)
    terms = jnp.concatenate([hi, mid, lo], axis=1)
    fk_ref[0] = jnp.dot(terms, sel_ref[...], preferred_element_type=F32).astype(BF16)
    frow_ref[0] = f.T[FOX_GATE_LANE:FOX_GATE_LANE + FOX_HEADS, :]


def _fox_cumsum(gates, fox_b_f):
    b, s, _ = gates.shape
    bias = jnp.concatenate([jnp.zeros((FOX_GATE_LANE,), F32), fox_b_f,
                            jnp.zeros((GATE_COLS - FOX_GATE_LANE - FOX_HEADS,), F32)]).reshape(1, GATE_COLS)
    src, dst = [], []
    for head in range(FOX_HEADS):
        for term in range(F_TERMS):
            src.append(term * GATE_COLS + FOX_GATE_LANE + head)
            dst.append((head // 2) * LANES + F_TERMS * (head % 2) + term)
    sel = jnp.zeros((F_TERMS * GATE_COLS, HALF), BF16).at[jnp.asarray(src), jnp.asarray(dst)].set(-1.0)
    return pl.pallas_call(
        _fox_cumsum_body,
        grid=(b,),
        in_specs=[pl.BlockSpec((1, s, GATE_COLS), lambda bi: (bi, 0, 0)), _resident((1, GATE_COLS)),
                  _resident((F_TERMS * GATE_COLS, HALF))],
        out_specs=[pl.BlockSpec((1, FOX_HEADS, s), lambda bi: (bi, 0, 0)),
                   pl.BlockSpec((1, s, HALF), lambda bi: (bi, 0, 0))],
        out_shape=[jax.ShapeDtypeStruct((b, FOX_HEADS, s), F32),
                   jax.ShapeDtypeStruct((b, s, HALF), BF16)],
        compiler_params=_params("parallel"),
        name="fox_cumsum",
    )(gates, bias, sel)


def _fox_attn_body(qt_ref, k_ref, fk_ref, vt_ref, frow_ref, o_ref, m_ref, acc_ref, *, tile):
    i = pl.program_id(1)
    dim = lax.broadcasted_iota(jnp.int32, (LANES, 1), 0)
    dh = FOX_HEAD_DIM
    m_ref[...] = jnp.full_like(m_ref, -jnp.inf)
    acc_ref[...] = jnp.zeros_like(acc_ref)
    ones = jnp.ones((ONES_ROWS, tile), BF16)

    n_pairs = FOX_HEADS // 2

    def sweep(blocks):
        if any(diagonal for _, diagonal in blocks):
            query = lax.broadcasted_iota(jnp.int32, (tile, 2 * tile), 1)
            query = jnp.where(query >= tile, query - tile, query)
            keep = lax.broadcasted_iota(jnp.int32, (tile, 2 * tile), 0) <= query

        def scores(u):
            blk, pair = divmod(u, n_pairs)
            key0 = pl.multiple_of(blocks[blk][0] * tile, tile)
            rows = slice(pair * LANES, (pair + 1) * LANES)
            q_t = qt_ref[0, rows, :]
            k_aug = jnp.concatenate([k_ref[0, pl.ds(key0, tile), rows],
                                     fk_ref[0, pl.ds(key0, tile), rows]], axis=1)
            q_aug = []
            for c in range(2):
                ones_rows = jnp.logical_and(dim >= F_TERMS * c, dim < F_TERMS * (c + 1))
                q_aug.append(jnp.concatenate(
                    [_keep_half_rows(q_t, c),
                     jnp.broadcast_to(jnp.where(ones_rows, 1.0, 0.0), q_t.shape).astype(BF16)], axis=0))
            return jnp.dot(k_aug, jnp.concatenate(q_aug, axis=1), preferred_element_type=F32)

        def weights(u, s):
            blk, pair = divmod(u, n_pairs)
            s = s + jnp.concatenate([frow_ref[0, 2 * pair:2 * pair + 1, :],
                                     frow_ref[0, 2 * pair + 1:2 * pair + 2, :]], axis=1)
            if blocks[blk][1]:
                s = jnp.where(keep, s, -jnp.inf)
            m_old = m_ref[pair]
            m_new = jnp.maximum(m_old, jnp.max(s, axis=0, keepdims=True))
            m_ref[pair] = m_new
            return jnp.exp2(m_old - m_new), jnp.exp2(s - m_new).astype(BF16)

        def accumulate(u, alpha, p):
            blk, pair = divmod(u, n_pairs)
            for c in range(2):
                hd = 2 * pair + c
                cols = slice(c * tile, (c + 1) * tile)
                v_aug = jnp.concatenate([vt_ref[0, blocks[blk][0], hd * dh:(hd + 1) * dh, :], ones], axis=0)
                acc_ref[hd] = alpha[:, cols] * acc_ref[hd] + jnp.dot(v_aug, p[:, cols], preferred_element_type=F32)

        _staged(n_pairs * len(blocks), scores, weights, accumulate)

    def two_blocks(jj, carry):
        sweep([(2 * jj, False), (2 * jj + 1, False)])
        return carry

    lax.fori_loop(0, lax.shift_right_logical(i, 1), two_blocks, 0)
    odd = lax.bitwise_and(i, 1) == 1

    @pl.when(odd)
    def _():
        sweep([(i - 1, False), (i, True)])

    @pl.when(jnp.logical_not(odd))
    def _():
        sweep([(i, True)])

    for pair in range(FOX_HEADS // 2):
        a0, a1 = acc_ref[2 * pair], acc_ref[2 * pair + 1]
        o_t = jnp.concatenate([a0[:dh] / a0[dh:dh + 1], a1[:dh] / a1[dh:dh + 1]], axis=0)
        o_ref[0, :, pair * LANES:(pair + 1) * LANES] = o_t.T.astype(BF16)


def _fox_attention(q_t, k, fk, v_t, frow):
    b, s, _ = k.shape
    n_blk, _, t = v_t.shape[1:]
    whole = lambda bi, i: (bi, 0, 0)
    return pl.pallas_call(
        functools.partial(_fox_attn_body, tile=t),
        grid=(b, n_blk),
        in_specs=[pl.BlockSpec((1, HALF, t), lambda bi, i: (bi, 0, i)),
                  pl.BlockSpec((1, s, HALF), whole), pl.BlockSpec((1, s, HALF), whole),
                  pl.BlockSpec((1, n_blk, HALF, t), lambda bi, i: (bi, 0, 0, 0)),
                  pl.BlockSpec((1, FOX_HEADS, t), lambda bi, i: (bi, 0, i))],
        out_specs=pl.BlockSpec((1, t, HALF), lambda bi, i: (bi, i, 0)),
        out_shape=jax.ShapeDtypeStruct((b, s, HALF), BF16),
        scratch_shapes=[pltpu.VMEM((FOX_HEADS // 2, 1, 2 * t), F32),
                        pltpu.VMEM((FOX_HEADS, FOX_HEAD_DIM + ONES_ROWS, t), F32)],
        compiler_params=_params("parallel", "arbitrary"),
        name="fox_attn",
    )(q_t, k, fk, v_t, frow)


def kernel(x, l0_ffn1_norm, l0_ffn1_w_gate, l0_ffn1_w_up, l0_ffn1_w_down, l0_mix_norm, l0_w_in, l0_diff_lambda_q1, l0_diff_lambda_k1, l0_diff_lambda_q2, l0_diff_lambda_k2, l0_diff_subln, l0_s5_a_re, l0_s5_a_im, l0_s5_log_dt, l0_s5_b_re, l0_s5_b_im, l0_s5_c_re, l0_s5_c_im, l0_s5_d, l0_s5_w_glu, l0_s5_b_glu, l0_w_out, l0_ffn2_norm, l0_ffn2_w_gate, l0_ffn2_w_up, l0_ffn2_w_down, l1_ffn1_norm, l1_ffn1_w_gate, l1_ffn1_w_up, l1_ffn1_w_down, l1_mix_norm, l1_w_in, l1_mlstm_conv_w, l1_mlstm_conv_b, l1_mlstm_b_i, l1_mlstm_b_f, l1_mlstm_out_norm, l1_fox_b_f, l1_w_out, l1_ffn2_norm, l1_ffn2_w_gate, l1_ffn2_w_up, l1_ffn2_w_down, final_norm):
    b, s, d = x.shape
    n = b * s
    seq = lambda a: a.reshape(b, s, a.shape[-1])
    flat = lambda a: a.reshape(n, a.shape[-1])

    x = _ffn(x.reshape(n, d), l0_ffn1_norm, l0_ffn1_w_gate, l0_ffn1_w_up, l0_ffn1_w_down)
    q_t, k, v_t, u = _proj0(x, l0_mix_norm, l0_w_in, s)
    lam_vecs = jnp.stack([l0_diff_lambda_q1, l0_diff_lambda_k1, l0_diff_lambda_q2, l0_diff_lambda_k2])
    ya = _diff_attention(q_t, seq(k), v_t, lam_vecs, l0_diff_subln)
    yb = _s5(seq(u), l0_s5_a_re, l0_s5_a_im, l0_s5_log_dt, l0_s5_b_re, l0_s5_b_im,
             l0_s5_c_re, l0_s5_c_im, l0_s5_d, l0_s5_w_glu, l0_s5_b_glu)
    x = _ffn(x, l0_ffn2_norm, l0_ffn2_w_gate, l0_ffn2_w_up, l0_ffn2_w_down,
             mix=(flat(ya), flat(yb), l0_w_out[:HALF], l0_w_out[HALF:]))

    x = _ffn(x, l1_ffn1_norm, l1_ffn1_w_gate, l1_ffn1_w_up, l1_ffn1_w_down)
    qk_m, v_m, o_m, qf_t, k_f, vf_t, gates = _proj1(x, l1_mix_norm, l1_w_in, s)
    hm = _mlstm(seq(qk_m), seq(v_m), seq(o_m), seq(gates), l1_mlstm_conv_w, l1_mlstm_conv_b,
                l1_mlstm_b_i, l1_mlstm_b_f, l1_mlstm_out_norm)
    frow, fk = _fox_cumsum(seq(gates), l1_fox_b_f)
    hf = _fox_attention(qf_t, seq(k_f), fk, vf_t, frow)
    x = _ffn(x, l1_ffn2_norm, l1_ffn2_w_gate, l1_ffn2_w_up, l1_ffn2_w_down,
             mix=(flat(hm), flat(hf), l1_w_out[:HALF], l1_w_out[HALF:]), final_norm=final_norm)
    return x.reshape(b, s, d)
```

```python
import functools
import math

import jax
import jax.numpy as jnp
import numpy as np
from jax import lax
from jax.experimental import pallas as pl
from jax.experimental.pallas import tpu as pltpu

F32 = jnp.float32
BF16 = jnp.bfloat16

D_MODEL = 1024
D_FF = 2816
NORM_EPS = 1e-6
ROPE_THETA = 500000.0
DIFF_HEADS = 4
DIFF_HEAD_DIM = 64
ROT_DIM = DIFF_HEAD_DIM // 4
DIFF_LAMBDA_INIT = 0.8 - 0.6 * math.exp(-0.3 * 0)
S5_GROUPS = 32
S5_GROUP = 16
S5_STATE = 64
S5_NSTATE = S5_GROUPS * S5_STATE
MLSTM_HEADS = 4
MLSTM_QK_DIM = 64
MLSTM_V_DIM = 128
MLSTM_CONV = 4
FOX_HEADS = 8
FOX_HEAD_DIM = 64
HALF = 512

LANES = 128
SUBLANES = 8
VMEM_LIMIT_BYTES = 60 * 1024 * 1024

TOKEN_TILE = 1024
FFN_TILE = 512
FF_CHUNK = 256
ATTN_TILE = 512
S5_TILE = 1024
S5_CHUNK = 16
MLSTM_CHUNK = 512
GATE_COLS = LANES

NT_DIMS = (((1,), (1,)), ((), ()))
LOG2_E = math.log2(math.e)
ONES_ROWS = 16


def _params(*semantics):
    return pltpu.CompilerParams(dimension_semantics=semantics, vmem_limit_bytes=VMEM_LIMIT_BYTES)


def _resident(shape):
    return pl.BlockSpec(shape, lambda *_: (0,) * len(shape), pipeline_mode=pl.Buffered(1))


def _rms(x, g):
    return x * lax.rsqrt(jnp.mean(x * x, axis=-1, keepdims=True) + NORM_EPS) * g


def _log_sigmoid(x):
    return jnp.minimum(x, 0.0) - jnp.log1p(jnp.exp(-jnp.abs(x)))


def _cumsum_rows(x):
    n = x.shape[0]
    row = lax.broadcasted_iota(jnp.int32, x.shape, 0)
    k = 1
    while k < n:
        x = x + jnp.where(row >= k, pltpu.roll(x, k, 0), 0.0)
        k *= 2
    return x


def _keep_half_rows(x, half):
    rows = x.shape[0] // 2
    zeros = jnp.zeros((rows,) + x.shape[1:], x.dtype)
    return jnp.concatenate([x[:rows], zeros] if half == 0 else [zeros, x[rows:]], axis=0)


def _staged(n, scores, weights, accumulate):
    s, w = {}, {}
    for step in range(n + 2):
        if step < n:
            s[step] = scores(step)
        if step >= 2:
            accumulate(step - 2, *w.pop(step - 2))
        if 1 <= step <= n:
            w[step - 1] = weights(step - 1, s.pop(step - 1))


def _store_key_blocks(ref, value):
    width = ref.shape[3]
    for blk in range(ref.shape[1]):
        ref[0, blk] = value[:, blk * width:(blk + 1) * width]


def _ffn_body(*refs, has_mix, has_final):
    refs = list(refs)
    x_ref = refs.pop(0)
    if has_mix:
        a_ref, b_ref, wa_ref, wb_ref = refs[:4]
        refs = refs[4:]
    g_ref, wg_hbm, wu_hbm, wd_hbm = refs[:4]
    refs = refs[4:]
    if has_final:
        gf_ref = refs.pop(0)
    o_ref, wg_ref, wu_ref, wd_ref, sem = refs

    n_chunks = D_FF // FF_CHUNK
    chunk = lambda c: slice(c * FF_CHUNK, (c + 1) * FF_CHUNK)

    def chunk_copies(c):
        return (pltpu.make_async_copy(wg_hbm.at[:, chunk(c)], wg_ref.at[:, chunk(c)], sem.at[0, c]),
                pltpu.make_async_copy(wu_hbm.at[:, chunk(c)], wu_ref.at[:, chunk(c)], sem.at[1, c]),
                pltpu.make_async_copy(wd_hbm.at[chunk(c), :], wd_ref.at[chunk(c), :], sem.at[2, c]))

    first = pl.program_id(0) == 0

    @pl.when(first)
    def _():
        for c in range(n_chunks):
            for copy in chunk_copies(c):
                copy.start()

    def tile(wait_for_weights):
        x = x_ref[...]
        if has_mix:
            x = x + jnp.dot(a_ref[...], wa_ref[...], preferred_element_type=F32)
            x = x + jnp.dot(b_ref[...], wb_ref[...], preferred_element_type=F32)
        h = _rms(x, g_ref[...]).astype(BF16)
        acc = jnp.zeros_like(x)
        for c in range(n_chunks):
            if wait_for_weights:
                for copy in chunk_copies(c):
                    copy.wait()
            gate = jnp.dot(h, wg_ref[:, chunk(c)].astype(BF16), preferred_element_type=F32)
            up = jnp.dot(h, wu_ref[:, chunk(c)].astype(BF16), preferred_element_type=F32)
            act = (gate * jax.nn.sigmoid(gate) * up).astype(BF16)
            acc = acc + jnp.dot(act, wd_ref[chunk(c), :].astype(BF16), preferred_element_type=F32)
        y = x + 0.5 * acc
        if has_final:
            y = _rms(y, gf_ref[...])
        o_ref[...] = y

    @pl.when(first)
    def _():
        tile(True)

    @pl.when(jnp.logical_not(first))
    def _():
        tile(False)


def _ffn(x, norm, wg, wu, wd, mix=None, final_norm=None):
    n, d = x.shape
    tm = min(FFN_TILE, n)
    row = lambda i: (i, 0)
    args = [x]
    specs = [pl.BlockSpec((tm, d), row)]
    if mix is not None:
        a, b, wa, wb = mix
        args += [a, b, wa.astype(BF16), wb.astype(BF16)]
        specs += [pl.BlockSpec((tm, HALF), row), pl.BlockSpec((tm, HALF), row),
                  _resident((HALF, d)), _resident((HALF, d))]
    args += [norm.reshape(1, d), wg, wu, wd]
    specs += [_resident((1, d))] + [pl.BlockSpec(memory_space=pl.ANY)] * 3
    if final_norm is not None:
        args.append(final_norm.reshape(1, d))
        specs.append(_resident((1, d)))
    body = functools.partial(_ffn_body, has_mix=mix is not None, has_final=final_norm is not None)
    return pl.pallas_call(
        body,
        grid=(n // tm,),
        in_specs=specs,
        out_specs=pl.BlockSpec((tm, d), row),
        out_shape=jax.ShapeDtypeStruct((n, d), F32),
        scratch_shapes=[pltpu.VMEM((d, D_FF), F32), pltpu.VMEM((d, D_FF), F32), pltpu.VMEM((D_FF, d), F32),
                        pltpu.SemaphoreType.DMA((3, D_FF // FF_CHUNK))],
        compiler_params=_params("arbitrary"),
        name="ffn",
    )(*args)


def _rope_angles(seq):
    pos = np.arange(seq, dtype=np.float32)
    inv = np.float32(ROPE_THETA) ** (-np.arange(0, ROT_DIM, 2, dtype=np.float32) / np.float32(ROT_DIM))
    ang = pos[:, None] * inv[None, :]
    return np.cos(ang), np.sin(ang)


def _rope_lane_tables(seq):
    half = ROT_DIM // 2
    cos, sin = _rope_angles(seq)
    one = np.ones((seq, DIFF_HEAD_DIM - ROT_DIM), np.float32)
    zero8 = np.zeros((seq, half), np.float32)
    zero48 = np.zeros((seq, DIFF_HEAD_DIM - ROT_DIM), np.float32)
    cos_t = np.concatenate([cos, cos, one] * 2, axis=1)
    sa_t = np.concatenate([-sin, zero8, zero48] * 2, axis=1)
    sb_t = np.concatenate([zero8, sin, zero48] * 2, axis=1)
    return cos_t, sa_t, sb_t


def _proj0_body(x_ref, g_ref, wt_ref, wk_ref, wu_ref, cos_ref, sa_ref, sb_ref, cosr_ref, sinr_ref,
                qt_ref, k_ref, vt_ref, u_ref):
    h = _rms(x_ref[...], g_ref[...]).astype(BF16)
    pt = lax.dot_general(wt_ref[...], h, NT_DIMS, preferred_element_type=F32)
    pk = jnp.dot(h, wk_ref[...].astype(BF16), preferred_element_type=F32)
    half = ROT_DIM // 2
    scale = DIFF_HEAD_DIM ** -0.5 * LOG2_E

    cos_r, sin_r = cosr_ref[...], sinr_ref[...]
    for comp in range(2 * DIFF_HEADS):
        r0 = comp * DIFF_HEAD_DIM
        x1, x2 = pt[r0:r0 + half], pt[r0 + half:r0 + ROT_DIM]
        qt_ref[0, r0:r0 + half, :] = ((x1 * cos_r - x2 * sin_r) * scale).astype(BF16)
        qt_ref[0, r0 + half:r0 + ROT_DIM, :] = ((x2 * cos_r + x1 * sin_r) * scale).astype(BF16)
        qt_ref[0, r0 + ROT_DIM:r0 + DIFF_HEAD_DIM, :] = (pt[r0 + ROT_DIM:r0 + DIFF_HEAD_DIM] * scale).astype(BF16)
    _store_key_blocks(vt_ref, pt[HALF:2 * HALF].astype(BF16))

    cos_t, sa_t, sb_t = cos_ref[...], sa_ref[...], sb_ref[...]
    for hd in range(DIFF_HEADS):
        sl = slice(hd * LANES, (hd + 1) * LANES)
        t = pk[:, sl]
        k_ref[:, sl] = (t * cos_t + pltpu.roll(t, LANES - half, 1) * sa_t
                        + pltpu.roll(t, half, 1) * sb_t).astype(BF16)
    u_ref[...] = jnp.dot(h, wu_ref[...].astype(BF16), preferred_element_type=F32)


def _proj0(x, norm, w_in, seq):
    n, d = x.shape
    b = n // seq
    tm = min(TOKEN_TILE, seq)
    tps = seq // tm
    row = lambda i: (i, 0)
    pos = lambda i: (i % tps, 0)
    pos_t = lambda i: (0, i % tps)
    seq_t = lambda i: (i // tps, 0, i % tps)
    ta = min(ATTN_TILE, seq)
    key_blk = lambda i: (i // tps, i % tps, 0, 0)
    cos, sin = _rope_angles(seq)
    w_t = jnp.concatenate([w_in[:, :HALF], w_in[:, 2 * HALF:3 * HALF]], axis=1).T.astype(BF16)
    w_cols = lambda blk: pl.BlockSpec((d, HALF), lambda i: (0, blk), pipeline_mode=pl.Buffered(1))
    t_shape = jax.ShapeDtypeStruct((b, HALF, seq), BF16)
    return pl.pallas_call(
        _proj0_body,
        grid=(n // tm,),
        in_specs=[pl.BlockSpec((tm, d), row), _resident((1, d)), _resident((2 * HALF, d)),
                  w_cols(1), w_cols(3),
                  pl.BlockSpec((tm, LANES), pos), pl.BlockSpec((tm, LANES), pos),
                  pl.BlockSpec((tm, LANES), pos),
                  pl.BlockSpec((ROT_DIM // 2, tm), pos_t), pl.BlockSpec((ROT_DIM // 2, tm), pos_t)],
        out_specs=[pl.BlockSpec((1, HALF, tm), seq_t), pl.BlockSpec((tm, HALF), row),
                   pl.BlockSpec((1, tm // ta, HALF, ta), key_blk), pl.BlockSpec((tm, HALF), row)],
        out_shape=[t_shape, jax.ShapeDtypeStruct((n, HALF), BF16),
                   jax.ShapeDtypeStruct((b, seq // ta, HALF, ta), BF16),
                   jax.ShapeDtypeStruct((n, HALF), F32)],
        compiler_params=_params("parallel"),
        name="proj0",
    )(x, norm.reshape(1, d), w_t, w_in, w_in, *_rope_lane_tables(seq), cos.T, sin.T)


def _diff_attn_body(lam_ref, subln_ref, qt_ref, k_ref, vt_ref, o_ref, m_ref, acc_ref, *, tile):
    i = pl.program_id(1)
    m_ref[...] = jnp.full_like(m_ref, -jnp.inf)
    acc_ref[...] = jnp.zeros_like(acc_ref)
    ones = jnp.ones((ONES_ROWS, tile), BF16)

    def sweep(blocks):
        if any(diagonal for _, diagonal in blocks):
            query = lax.broadcasted_iota(jnp.int32, (tile, 2 * tile), 1)
            query = jnp.where(query >= tile, query - tile, query)
            keep = lax.broadcasted_iota(jnp.int32, (tile, 2 * tile), 0) <= query

        def scores(u):
            blk, hd = divmod(u, DIFF_HEADS)
            key0 = pl.multiple_of(blocks[blk][0] * tile, tile)
            rows = slice(hd * LANES, (hd + 1) * LANES)
            q_t = qt_ref[0, rows, :]
            q_both = jnp.concatenate([_keep_half_rows(q_t, 0), _keep_half_rows(q_t, 1)], axis=1)
            return jnp.dot(k_ref[0, pl.ds(key0, tile), rows], q_both, preferred_element_type=F32)

        def weights(u, s):
            blk, hd = divmod(u, DIFF_HEADS)
            if blocks[blk][1]:
                s = jnp.where(keep, s, -jnp.inf)
            m_old = m_ref[hd]
            m_new = jnp.maximum(m_old, jnp.max(s, axis=0, keepdims=True))
            m_ref[hd] = m_new
            return jnp.exp2(m_old - m_new), jnp.exp2(s - m_new).astype(BF16)

        def accumulate(u, alpha, p):
            blk, hd = divmod(u, DIFF_HEADS)
            v_aug = jnp.concatenate([vt_ref[0, blocks[blk][0], hd * LANES:(hd + 1) * LANES, :], ones], axis=0)
            acc_ref[hd] = alpha * acc_ref[hd] + jnp.dot(v_aug, p, preferred_element_type=F32)

        _staged(DIFF_HEADS * len(blocks), scores, weights, accumulate)

    def two_blocks(jj, carry):
        sweep([(2 * jj, False), (2 * jj + 1, False)])
        return carry

    lax.fori_loop(0, lax.shift_right_logical(i, 1), two_blocks, 0)
    odd = lax.bitwise_and(i, 1) == 1

    @pl.when(odd)
    def _():
        sweep([(i - 1, False), (i, True)])

    @pl.when(jnp.logical_not(odd))
    def _():
        sweep([(i, True)])

    lam_v = lam_ref[...]
    lam = (jnp.exp(jnp.sum(lam_v[0:1] * lam_v[1:2], keepdims=True))
           - jnp.exp(jnp.sum(lam_v[2:3] * lam_v[3:4], keepdims=True)) + DIFF_LAMBDA_INIT)
    for hd in range(DIFF_HEADS):
        a1, a2 = acc_ref[hd, :, :tile], acc_ref[hd, :, tile:]
        o_t = (a1[:LANES] / a1[LANES:LANES + 1] - lam * (a2[:LANES] / a2[LANES:LANES + 1]))
        o_t = o_t * lax.rsqrt(jnp.mean(o_t * o_t, axis=0, keepdims=True) + NORM_EPS)
        o_t = o_t * subln_ref[...] * (1.0 - DIFF_LAMBDA_INIT)
        o_ref[0, :, hd * LANES:(hd + 1) * LANES] = o_t.T.astype(BF16)


def _diff_attention(q_t, k, v_t, lam_vecs, subln):
    b, s, _ = k.shape
    n_blk, _, t = v_t.shape[1:]
    const = lambda bi, i: (0, 0)
    return pl.pallas_call(
        functools.partial(_diff_attn_body, tile=t),
        grid=(b, n_blk),
        in_specs=[pl.BlockSpec((4, DIFF_HEAD_DIM), const), pl.BlockSpec((LANES, 1), const),
                  pl.BlockSpec((1, HALF, t), lambda bi, i: (bi, 0, i)),
                  pl.BlockSpec((1, s, HALF), lambda bi, i: (bi, 0, 0)),
                  pl.BlockSpec((1, n_blk, HALF, t), lambda bi, i: (bi, 0, 0, 0))],
        out_specs=pl.BlockSpec((1, t, HALF), lambda bi, i: (bi, i, 0)),
        out_shape=jax.ShapeDtypeStruct((b, s, HALF), BF16),
        scratch_shapes=[pltpu.VMEM((DIFF_HEADS, 1, 2 * t), F32),
                        pltpu.VMEM((DIFF_HEADS, LANES + ONES_ROWS, 2 * t), F32)],
        compiler_params=_params("parallel", "arbitrary"),
        name="diff_attn",
    )(lam_vecs, subln.reshape(LANES, 1), q_t, k, v_t)


def _s5_discretize_body(lr_ref, li_ref, ldt_ref, br_ref, bi_ref, k_ref, pr_ref, pi_ref, bbr_ref, bbi_ref):
    lr, li = lr_ref[...], li_ref[...]
    dt = jnp.exp(ldt_ref[...])
    mag = jnp.exp(lr * dt)
    ar, ai = mag * jnp.cos(li * dt), mag * jnp.sin(li * dt)
    den = lr * lr + li * li
    gr = ((ar - 1.0) * lr + ai * li) / den
    gi = (ai * lr - (ar - 1.0) * li) / den
    br, bi = br_ref[...], bi_ref[...]
    bbr_ref[...] = gr * br - gi * bi
    bbi_ref[...] = gr * bi + gi * br
    steps = k_ref[...]
    mag_k = jnp.exp(lr * dt * steps)
    pr_ref[...] = mag_k * jnp.cos(li * dt * steps)
    pi_ref[...] = mag_k * jnp.sin(li * dt * steps)


def _s5_discretize(a_re, a_im, log_dt, b_re, b_im):
    row = lambda a: a.reshape(1, S5_NSTATE)
    ldt = jnp.broadcast_to(log_dt[:, None], (S5_GROUPS, S5_STATE))
    b_t = lambda a: a.transpose(2, 0, 1).reshape(S5_GROUP, S5_NSTATE)
    steps = jnp.arange(1, S5_CHUNK + 1, dtype=F32).reshape(S5_CHUNK, 1)
    pshape = jax.ShapeDtypeStruct((S5_CHUNK, S5_NSTATE), F32)
    bshape = jax.ShapeDtypeStruct((S5_GROUP, S5_NSTATE), F32)
    return pl.pallas_call(
        _s5_discretize_body,
        out_shape=[pshape, pshape, bshape, bshape],
        name="s5_discretize",
    )(row(a_re), row(a_im), row(ldt), b_t(b_re), b_t(b_im), steps)


def _s5_body(u_ref, pr_ref, pi_ref, bbr_ref, bbi_ref, cr_ref, ci_ref, d_ref, wglu_ref, bglu_ref,
             o_ref, us_ref, ys_ref, xr_ref, xi_ref, hin_ref, carry_ref, *, tile):
    n_chunks = tile // S5_CHUNK
    slabs = HALF // LANES
    gw = S5_NSTATE // slabs

    @pl.when(pl.program_id(1) == 0)
    def _():
        carry_ref[...] = jnp.zeros_like(carry_ref)

    u = u_ref[0]
    for g in range(slabs):
        us_ref[g] = u[:, g * LANES:(g + 1) * LANES]
    u_perm = jnp.concatenate(
        [jnp.concatenate([us_ref[g, pl.ds(st, n_chunks, stride=S5_CHUNK), :] for g in range(slabs)], axis=1)
         for st in range(S5_CHUNK)], axis=0).astype(BF16)

    for g in range(slabs):
        ug = u_perm[:, g * LANES:(g + 1) * LANES]
        xr_ref[:, g * gw:(g + 1) * gw] = jnp.dot(ug, bbr_ref[g], preferred_element_type=F32)
        xi_ref[:, g * gw:(g + 1) * gw] = jnp.dot(ug, bbi_ref[g], preferred_element_type=F32)

    ar, ai = pr_ref[0:1, :], pi_ref[0:1, :]
    hr, hi = xr_ref[0:n_chunks, :], xi_ref[0:n_chunks, :]
    for st in range(1, S5_CHUNK):
        rows = slice(st * n_chunks, (st + 1) * n_chunks)
        hr, hi = ar * hr - ai * hi + xr_ref[rows, :], ar * hi + ai * hr + xi_ref[rows, :]
        xr_ref[rows, :] = hr
        xi_ref[rows, :] = hi

    last = (S5_CHUNK - 1) * n_chunks
    ac_r, ac_i = pr_ref[S5_CHUNK - 1:S5_CHUNK, :], pi_ref[S5_CHUNK - 1:S5_CHUNK, :]

    def carry_step(c, carry):
        cr, ci = carry
        hin_ref[0, pl.ds(c, 1), :] = cr
        hin_ref[1, pl.ds(c, 1), :] = ci
        zr, zi = xr_ref[pl.ds(last + c, 1), :], xi_ref[pl.ds(last + c, 1), :]
        return ac_r * cr - ac_i * ci + zr, ac_r * ci + ac_i * cr + zi

    cr, ci = lax.fori_loop(0, n_chunks, carry_step, (carry_ref[0:1, :], carry_ref[1:2, :]), unroll=4)
    carry_ref[0:1, :] = cr
    carry_ref[1:2, :] = ci

    hin_r, hin_i = hin_ref[0], hin_ref[1]
    for st in range(S5_CHUNK):
        rows = slice(st * n_chunks, (st + 1) * n_chunks)
        p_r, p_i = pr_ref[st:st + 1, :], pi_ref[st:st + 1, :]
        xr_ref[rows, :] = xr_ref[rows, :] + (p_r * hin_r - p_i * hin_i)
        xi_ref[rows, :] = xi_ref[rows, :] + (p_r * hin_i + p_i * hin_r)
    for g in range(slabs):
        cols = slice(g * gw, (g + 1) * gw)
        yg = (jnp.dot(xr_ref[:, cols].astype(BF16), cr_ref[g], preferred_element_type=F32)
              - jnp.dot(xi_ref[:, cols].astype(BF16), ci_ref[g], preferred_element_type=F32))
        for st in range(S5_CHUNK):
            ys_ref[g, pl.ds(st, n_chunks, stride=S5_CHUNK), :] = yg[st * n_chunks:(st + 1) * n_chunks]

    y = jnp.concatenate([ys_ref[g] for g in range(slabs)], axis=1) + d_ref[...] * u
    z = jax.nn.gelu(y, approximate=True)
    gate = jnp.dot(z.astype(BF16), wglu_ref[...], preferred_element_type=F32) + bglu_ref[...]
    o_ref[0] = (z * jax.nn.sigmoid(gate)).astype(BF16)


def _s5(u, a_re, a_im, log_dt, b_re, b_im, c_re, c_im, d_skip, w_glu, b_glu):
    b, s, w = u.shape
    t = min(S5_TILE, s)
    slabs = w // LANES
    per = S5_GROUPS // slabs
    gw = S5_NSTATE // slabs
    pw_r, pw_i, bbr, bbi = _s5_discretize(a_re, a_im, log_dt, b_re, b_im)
    eye = jnp.eye(per, dtype=F32)

    def block_in(m):
        m = m.reshape(S5_GROUP, slabs, per, S5_STATE)
        return jnp.einsum('csgp,gh->sgchp', m, eye).reshape(slabs, LANES, gw).astype(BF16)

    def block_out(m):
        m = m.reshape(slabs, per, S5_GROUP, S5_STATE)
        return jnp.einsum('sgcp,gh->sgphc', m, eye).reshape(slabs, gw, LANES).astype(BF16)

    seq_blk = lambda bi, ti: (bi, ti, 0)
    n_chunks = t // S5_CHUNK
    return pl.pallas_call(
        functools.partial(_s5_body, tile=t),
        grid=(b, s // t),
        in_specs=[pl.BlockSpec((1, t, w), seq_blk),
                  _resident((S5_CHUNK, S5_NSTATE)), _resident((S5_CHUNK, S5_NSTATE)),
                  _resident((slabs, LANES, gw)), _resident((slabs, LANES, gw)),
                  _resident((slabs, gw, LANES)), _resident((slabs, gw, LANES)),
                  _resident((1, w)), _resident((w, w)), _resident((1, w))],
        out_specs=pl.BlockSpec((1, t, w), seq_blk),
        out_shape=jax.ShapeDtypeStruct((b, s, w), BF16),
        scratch_shapes=[pltpu.VMEM((slabs, t, LANES), F32), pltpu.VMEM((slabs, t, LANES), F32),
                        pltpu.VMEM((t, S5_NSTATE), F32), pltpu.VMEM((t, S5_NSTATE), F32),
                        pltpu.VMEM((2, n_chunks, S5_NSTATE), F32), pltpu.VMEM((2, S5_NSTATE), F32)],
        compiler_params=_params("parallel", "arbitrary"),
        name="s5",
    )(u, pw_r, pw_i, block_in(bbr), block_in(bbi),
      block_out(c_re), block_out(c_im), d_skip.reshape(1, w), w_glu.astype(BF16), b_glu.reshape(1, w))


def _proj1_body(x_ref, g_ref, wt_ref, wm_ref, wo_ref, wk_ref, wg_ref,
                qk_ref, vm_ref, om_ref, qft_ref, kf_ref, vft_ref, gt_ref):
    h = _rms(x_ref[...], g_ref[...]).astype(BF16)
    pt = lax.dot_general(wt_ref[...], h, NT_DIMS, preferred_element_type=F32)
    qft_ref[0] = (pt[:HALF] * (FOX_HEAD_DIM ** -0.5 * LOG2_E)).astype(BF16)
    _store_key_blocks(vft_ref, pt[HALF:].astype(BF16))
    pm = jnp.dot(h, wm_ref[...].astype(BF16), preferred_element_type=F32)
    qk_ref[...] = pm[:, 0:HALF]
    vm_ref[...] = pm[:, HALF:2 * HALF].astype(BF16)
    om_ref[...] = jnp.dot(h, wo_ref[...].astype(BF16), preferred_element_type=F32)
    kf_ref[...] = jnp.dot(h, wk_ref[...].astype(BF16), preferred_element_type=F32).astype(BF16)
    gt_ref[...] = jnp.dot(h, wg_ref[...].astype(BF16), preferred_element_type=F32)


def _proj1(x, norm, w_in, seq):
    n, d = x.shape
    b = n // seq
    tm = min(TOKEN_TILE, seq)
    tps = seq // tm
    row = lambda i: (i, 0)
    seq_t = lambda i: (i // tps, 0, i % tps)
    ta = min(ATTN_TILE, seq)
    key_blk = lambda i: (i // tps, i % tps, 0, 0)
    o = 2 * HALF
    g = 2 * MLSTM_HEADS
    om = o + g
    w_o = w_in[:, om:om + HALF]
    w_k = w_in[:, om + 2 * HALF:om + 3 * HALF]
    w_g = jnp.concatenate([w_in[:, o:om], w_in[:, om + 4 * HALF:],
                           jnp.zeros((d, GATE_COLS - g - FOX_HEADS), w_in.dtype)], axis=1)
    w_t = jnp.concatenate([w_in[:, om + HALF:om + 2 * HALF], w_in[:, om + 3 * HALF:om + 4 * HALF]],
                          axis=1).T.astype(BF16)
    f32_half = jax.ShapeDtypeStruct((n, HALF), F32)
    bf_half = jax.ShapeDtypeStruct((n, HALF), BF16)
    t_shape = jax.ShapeDtypeStruct((b, HALF, seq), BF16)
    half_spec = pl.BlockSpec((tm, HALF), row)
    t_spec = pl.BlockSpec((1, HALF, tm), seq_t)
    return pl.pallas_call(
        _proj1_body,
        grid=(n // tm,),
        in_specs=[pl.BlockSpec((tm, d), row), _resident((1, d)), _resident((2 * HALF, d)),
                  pl.BlockSpec((d, o), lambda i: (0, 0), pipeline_mode=pl.Buffered(1)),
                  _resident((d, HALF)), _resident((d, HALF)), _resident((d, GATE_COLS))],
        out_specs=[half_spec, half_spec, half_spec, t_spec, half_spec,
                   pl.BlockSpec((1, tm // ta, HALF, ta), key_blk), pl.BlockSpec((tm, GATE_COLS), row)],
        out_shape=[f32_half, bf_half, f32_half, t_shape, bf_half,
                   jax.ShapeDtypeStruct((b, seq // ta, HALF, ta), BF16),
                   jax.ShapeDtypeStruct((n, GATE_COLS), F32)],
        compiler_params=_params("parallel"),
        name="proj1",
    )(x, norm.reshape(1, d), w_t, w_in, w_o, w_k, w_g)


def _mlstm_body(qk_ref, v_ref, og_ref, gt_ref, cw_ref, cb_ref, gb_ref, on_ref, out_ref,
                xbuf_ref, cn_ref, m_ref, *, chunk):
    pad = SUBLANES
    dk, dv = MLSTM_QK_DIM, MLSTM_V_DIM

    @pl.when(pl.program_id(1) == 0)
    def _():
        xbuf_ref[0:pad, :] = jnp.zeros((pad, HALF), F32)
        cn_ref[...] = jnp.zeros_like(cn_ref)
        m_ref[...] = jnp.zeros_like(m_ref)

    x = qk_ref[0]
    xbuf_ref[pad:pad + chunk, :] = x
    a = jnp.zeros_like(x) + cb_ref[...]
    for tap in range(MLSTM_CONV):
        a = a + cw_ref[tap:tap + 1, :] * xbuf_ref[pl.ds(pad - (MLSTM_CONV - 1) + tap, chunk), :]
    xbuf_ref[0:pad, :] = x[chunk - pad:chunk, :]
    a = a * jax.nn.sigmoid(a)
    q = a[:, :MLSTM_HEADS * dk] * dk ** -0.5
    k = a[:, MLSTM_HEADS * dk:]
    k_t = k.T

    gi = gt_ref[0] + gb_ref[...]
    bcum = _cumsum_rows(_log_sigmoid(gi))
    gi_t = gi.T
    bcum_t = bcum.T
    tri = (lax.broadcasted_iota(jnp.int32, (chunk, chunk), 1)
           <= lax.broadcasted_iota(jnp.int32, (chunk, chunk), 0))
    lane = lax.broadcasted_iota(jnp.int32, (1, LANES), 1)
    v_all = v_ref[0]
    o_gate = og_ref[0]
    ones = jnp.ones((chunk, dv), BF16)

    states = [cn_ref[pair] for pair in range(MLSTM_HEADS // 2)]

    def products(hd):
        pair, odd = hd // 2, hd % 2
        sel = (lane >= dk) if odd else (lane < dk)
        q2 = jnp.where(sel, q[:, pair * LANES:(pair + 1) * LANES], 0.0).astype(BF16)
        k2 = k[:, pair * LANES:(pair + 1) * LANES].astype(BF16)
        qk = lax.dot_general(q2, k2, NT_DIMS, preferred_element_type=F32)
        return qk, jnp.dot(q2, states[pair].astype(BF16), preferred_element_type=F32)

    def gate_weights(hd, prods):
        qk, q_state = prods
        f_lane = MLSTM_HEADS + hd
        b_col = bcum[:, f_lane:f_lane + 1]
        b_row = bcum_t[f_lane:f_lane + 1, :]
        i_row = gi_t[hd:hd + 1, :]
        m_prev = m_ref[hd]
        dm = jnp.where(tri, b_col - b_row + i_row, -jnp.inf)
        g = b_col + m_prev
        mt = jnp.maximum(g, jnp.max(dm, axis=-1, keepdims=True))
        s = (qk * jnp.exp(dm - mt)).astype(BF16)
        w_inter = jnp.exp(g - mt)

        b_last = bcum[chunk - 1:chunk, f_lane:f_lane + 1]
        dec = b_last - b_row + i_row
        m_new = jnp.maximum(b_last + m_prev, jnp.max(dec, axis=-1, keepdims=True))
        k_w = (k_t[hd * dk:(hd + 1) * dk, :] * jnp.exp(dec - m_new)).astype(BF16)
        carry_w = jnp.exp(b_last + m_prev - m_new)
        m_ref[hd] = m_new
        return s, k_w, q_state, w_inter, mt, carry_w

    def outputs(hd, s, k_w, q_state, w_inter, mt, carry_w):
        pair, odd = hd // 2, hd % 2
        v_aug = jnp.concatenate([v_all[:, hd * dv:(hd + 1) * dv], ones], axis=1)
        s_v = jnp.dot(s, v_aug, preferred_element_type=F32)
        num = w_inter * q_state[:, :dv] + s_v[:, :dv]
        den = w_inter * q_state[:, dv:] + s_v[:, dv:]
        h_out = num / jnp.maximum(jnp.abs(den), jnp.exp(-mt))
        rows = slice(odd * dk, (odd + 1) * dk)
        cn_ref[pair, rows, :] = (carry_w * states[pair][rows, :]
                                 + jnp.dot(k_w, v_aug, preferred_element_type=F32))
        cols = slice(hd * dv, (hd + 1) * dv)
        h_norm = _rms(h_out, on_ref[hd:hd + 1, :])
        out_ref[0, :, cols] = (jax.nn.sigmoid(o_gate[:, cols]) * h_norm).astype(BF16)

    _staged(MLSTM_HEADS, products, gate_weights, outputs)


def _mlstm(qk, v, o_gate, gates, conv_w, conv_b, b_i, b_f, out_norm):
    b, s, _ = qk.shape
    c = min(MLSTM_CHUNK, s)
    gate_bias = jnp.concatenate([b_i, b_f, jnp.zeros((GATE_COLS - 2 * MLSTM_HEADS,), F32)]).reshape(1, GATE_COLS)
    seq_blk = lambda bi, ci: (bi, ci, 0)
    half_spec = pl.BlockSpec((1, c, HALF), seq_blk)
    return pl.pallas_call(
        functools.partial(_mlstm_body, chunk=c),
        grid=(b, s // c),
        in_specs=[half_spec, half_spec, half_spec, pl.BlockSpec((1, c, GATE_COLS), seq_blk),
                  _resident((MLSTM_CONV, HALF)), _resident((1, HALF)), _resident((1, GATE_COLS)),
                  _resident((MLSTM_HEADS, MLSTM_V_DIM))],
        out_specs=half_spec,
        out_shape=jax.ShapeDtypeStruct((b, s, HALF), BF16),
        scratch_shapes=[pltpu.VMEM((c + SUBLANES, HALF), F32),
                        pltpu.VMEM((MLSTM_HEADS // 2, 2 * MLSTM_QK_DIM, 2 * MLSTM_V_DIM), F32),
                        pltpu.VMEM((MLSTM_HEADS, 1, 1), F32)],
        compiler_params=_params("parallel", "arbitrary"),
        name="mlstm",
    )(qk, v, o_gate, gates, conv_w, conv_b.reshape(1, HALF), gate_bias, out_norm)


FOX_GATE_LANE = 2 * MLSTM_HEADS
F_TERMS = 3
FQ_ROWS = 16


def _fox_placements():
    sel_k = np.zeros((F_TERMS * GATE_COLS, HALF), np.float32)
    sel_q = np.zeros((FOX_HEADS * FQ_ROWS, F_TERMS * GATE_COLS), np.float32)
    ones_k = np.zeros((1, HALF), np.float32)
    ones_q = np.zeros((FOX_HEADS * FQ_ROWS, 1), np.float32)
    for head in range(FOX_HEADS):
        pair, c = divmod(head, 2)
        for term in range(F_TERMS):
            src = term * GATE_COLS + FOX_GATE_LANE + head
            sel_k[src, pair * LANES + F_TERMS * c + term] = -1.0
            sel_q[head * FQ_ROWS + 2 * F_TERMS + term, src] = 1.0
            ones_k[0, pair * LANES + 2 * F_TERMS + term] = 1.0
            ones_q[head * FQ_ROWS + F_TERMS * c + term, 0] = 1.0
    return (jnp.asarray(sel_k, BF16), jnp.asarray(ones_k), jnp.asarray(sel_q, BF16), jnp.asarray(ones_q))


def _fox_cumsum_body(gt_ref, gb_ref, sel_k_ref, ones_k_ref, sel_q_ref, ones_q_ref, fq_ref, fk_ref):
    f = _cumsum_rows(_log_sigmoid(gt_ref[0] + gb_ref[...])) * LOG2_E
    hi = f.astype(BF16)
    rest = f - hi.astype(F32)
    mid = rest.astype(BF16)
    lo = (rest - mid.astype(F32)).astype(BF16)
    terms = jnp.concatenate([hi, mid, lo], axis=1)
    fk_ref[0] = (jnp.dot(terms, sel_k_ref[...], preferred_element_type=F32) + ones_k_ref[...]).astype(BF16)
    fq_t = lax.dot_general(sel_q_ref[...], terms, NT_DIMS, preferred_element_type=F32)
    fq_ref[0] = (fq_t + ones_q_ref[...]).astype(BF16)


def _fox_cumsum(gates, fox_b_f):
    b, s, _ = gates.shape
    bias = jnp.concatenate([jnp.zeros((FOX_GATE_LANE,), F32), fox_b_f,
                            jnp.zeros((GATE_COLS - FOX_GATE_LANE - FOX_HEADS,), F32)]).reshape(1, GATE_COLS)
    placements = _fox_placements()
    return pl.pallas_call(
        _fox_cumsum_body,
        grid=(b,),
        in_specs=[pl.BlockSpec((1, s, GATE_COLS), lambda bi: (bi, 0, 0)), _resident((1, GATE_COLS))]
                 + [_resident(p.shape) for p in placements],
        out_specs=[pl.BlockSpec((1, FOX_HEADS * FQ_ROWS, s), lambda bi: (bi, 0, 0)),
                   pl.BlockSpec((1, s, HALF), lambda bi: (bi, 0, 0))],
        out_shape=[jax.ShapeDtypeStruct((b, FOX_HEADS * FQ_ROWS, s), BF16),
                   jax.ShapeDtypeStruct((b, s, HALF), BF16)],
        compiler_params=_params("parallel"),
        name="fox_cumsum",
    )(gates, bias, *placements)


def _fox_attn_body(qt_ref, fq_ref, k_ref, fk_ref, vt_ref, o_ref, m_ref, acc_ref, *, tile):
    i = pl.program_id(1)
    dh = FOX_HEAD_DIM
    unused_rows = jnp.zeros((LANES - FQ_ROWS, tile), BF16)
    m_ref[...] = jnp.full_like(m_ref, -jnp.inf)
    acc_ref[...] = jnp.zeros_like(acc_ref)
    ones = jnp.ones((ONES_ROWS, tile), BF16)

    n_pairs = FOX_HEADS // 2

    def sweep(blocks):
        if any(diagonal for _, diagonal in blocks):
            query = lax.broadcasted_iota(jnp.int32, (tile, 2 * tile), 1)
            query = jnp.where(query >= tile, query - tile, query)
            keep = lax.broadcasted_iota(jnp.int32, (tile, 2 * tile), 0) <= query

        def scores(u):
            blk, pair = divmod(u, n_pairs)
            key0 = pl.multiple_of(blocks[blk][0] * tile, tile)
            rows = slice(pair * LANES, (pair + 1) * LANES)
            q_t = qt_ref[0, rows, :]
            k_aug = jnp.concatenate([k_ref[0, pl.ds(key0, tile), rows],
                                     fk_ref[0, pl.ds(key0, tile), rows]], axis=1)
            q_aug = []
            for c in range(2):
                hd = 2 * pair + c
                q_aug.append(jnp.concatenate(
                    [_keep_half_rows(q_t, c), fq_ref[0, hd * FQ_ROWS:(hd + 1) * FQ_ROWS, :], unused_rows], axis=0))
            return jnp.dot(k_aug, jnp.concatenate(q_aug, axis=1), preferred_element_type=F32)

        def weights(u, s):
            blk, pair = divmod(u, n_pairs)
            if blocks[blk][1]:
                s = jnp.where(keep, s, -jnp.inf)
            m_old = m_ref[pair]
            m_new = jnp.maximum(m_old, jnp.max(s, axis=0, keepdims=True))
            m_ref[pair] = m_new
            return jnp.exp2(m_old - m_new), jnp.exp2(s - m_new).astype(BF16)

        def accumulate(u, alpha, p):
            blk, pair = divmod(u, n_pairs)
            for c in range(2):
                hd = 2 * pair + c
                cols = slice(c * tile, (c + 1) * tile)
                v_aug = jnp.concatenate([vt_ref[0, blocks[blk][0], hd * dh:(hd + 1) * dh, :], ones], axis=0)
                acc_ref[hd] = alpha[:, cols] * acc_ref[hd] + jnp.dot(v_aug, p[:, cols], preferred_element_type=F32)

        _staged(n_pairs * len(blocks), scores, weights, accumulate)

    def two_blocks(jj, carry):
        sweep([(2 * jj, False), (2 * jj + 1, False)])
        return carry

    lax.fori_loop(0, lax.shift_right_logical(i, 1), two_blocks, 0)
    odd = lax.bitwise_and(i, 1) == 1

    @pl.when(odd)
    def _():
        sweep([(i - 1, False), (i, True)])

    @pl.when(jnp.logical_not(odd))
    def _():
        sweep([(i, True)])

    for pair in range(FOX_HEADS // 2):
        a0, a1 = acc_ref[2 * pair], acc_ref[2 * pair + 1]
        o_t = jnp.concatenate([a0[:dh] / a0[dh:dh + 1], a1[:dh] / a1[dh:dh + 1]], axis=0)
        o_ref[0, :, pair * LANES:(pair + 1) * LANES] = o_t.T.astype(BF16)


def _fox_attention(q_t, fq_t, k, fk, v_t):
    b, s, _ = k.shape
    n_blk, _, t = v_t.shape[1:]
    whole = lambda bi, i: (bi, 0, 0)
    return pl.pallas_call(
        functools.partial(_fox_attn_body, tile=t),
        grid=(b, n_blk),
        in_specs=[pl.BlockSpec((1, HALF, t), lambda bi, i: (bi, 0, i)),
                  pl.BlockSpec((1, FOX_HEADS * FQ_ROWS, t), lambda bi, i: (bi, 0, i)),
                  pl.BlockSpec((1, s, HALF), whole), pl.BlockSpec((1, s, HALF), whole),
                  pl.BlockSpec((1, n_blk, HALF, t), lambda bi, i: (bi, 0, 0, 0))],
        out_specs=pl.BlockSpec((1, t, HALF), lambda bi, i: (bi, i, 0)),
        out_shape=jax.ShapeDtypeStruct((b, s, HALF), BF16),
        scratch_shapes=[pltpu.VMEM((FOX_HEADS // 2, 1, 2 * t), F32),
                        pltpu.VMEM((FOX_HEADS, FOX_HEAD_DIM + ONES_ROWS, t), F32)],
        compiler_params=_params("parallel", "arbitrary"),
        name="fox_attn",
    )(q_t, fq_t, k, fk, v_t)


def kernel(x, l0_ffn1_norm, l0_ffn1_w_gate, l0_ffn1_w_up, l0_ffn1_w_down, l0_mix_norm, l0_w_in, l0_diff_lambda_q1, l0_diff_lambda_k1, l0_diff_lambda_q2, l0_diff_lambda_k2, l0_diff_subln, l0_s5_a_re, l0_s5_a_im, l0_s5_log_dt, l0_s5_b_re, l0_s5_b_im, l0_s5_c_re, l0_s5_c_im, l0_s5_d, l0_s5_w_glu, l0_s5_b_glu, l0_w_out, l0_ffn2_norm, l0_ffn2_w_gate, l0_ffn2_w_up, l0_ffn2_w_down, l1_ffn1_norm, l1_ffn1_w_gate, l1_ffn1_w_up, l1_ffn1_w_down, l1_mix_norm, l1_w_in, l1_mlstm_conv_w, l1_mlstm_conv_b, l1_mlstm_b_i, l1_mlstm_b_f, l1_mlstm_out_norm, l1_fox_b_f, l1_w_out, l1_ffn2_norm, l1_ffn2_w_gate, l1_ffn2_w_up, l1_ffn2_w_down, final_norm):
    b, s, d = x.shape
    n = b * s
    seq = lambda a: a.reshape(b, s, a.shape[-1])
    flat = lambda a: a.reshape(n, a.shape[-1])

    x = _ffn(x.reshape(n, d), l0_ffn1_norm, l0_ffn1_w_gate, l0_ffn1_w_up, l0_ffn1_w_down)
    q_t, k, v_t, u = _proj0(x, l0_mix_norm, l0_w_in, s)
    lam_vecs = jnp.stack([l0_diff_lambda_q1, l0_diff_lambda_k1, l0_diff_lambda_q2, l0_diff_lambda_k2])
    ya = _diff_attention(q_t, seq(k), v_t, lam_vecs, l0_diff_subln)
    yb = _s5(seq(u), l0_s5_a_re, l0_s5_a_im, l0_s5_log_dt, l0_s5_b_re, l0_s5_b_im,
             l0_s5_c_re, l0_s5_c_im, l0_s5_d, l0_s5_w_glu, l0_s5_b_glu)
    x = _ffn(x, l0_ffn2_norm, l0_ffn2_w_gate, l0_ffn2_w_up, l0_ffn2_w_down,
             mix=(flat(ya), flat(yb), l0_w_out[:HALF], l0_w_out[HALF:]))

    x = _ffn(x, l1_ffn1_norm, l1_ffn1_w_gate, l1_ffn1_w_up, l1_ffn1_w_down)
    qk_m, v_m, o_m, qf_t, k_f, vf_t, gates = _proj1(x, l1_mix_norm, l1_w_in, s)
    hm = _mlstm(seq(qk_m), seq(v_m), seq(o_m), seq(gates), l1_mlstm_conv_w, l1_mlstm_conv_b,
                l1_mlstm_b_i, l1_mlstm_b_f, l1_mlstm_out_norm)
    fq_t, fk = _fox_cumsum(seq(gates), l1_fox_b_f)
    hf = _fox_attention(qf_t, fq_t, seq(k_f), fk, vf_t)
    x = _ffn(x, l1_ffn2_norm, l1_ffn2_w_gate, l1_ffn2_w_up, l1_ffn2_w_down,
             mix=(flat(hm), flat(hf), l1_w_out[:HALF], l1_w_out[HALF:]), final_norm=final_norm)
    return x.reshape(b, s, d)
```

```python
import functools
import math

import jax
import jax.numpy as jnp
import numpy as np
from jax import lax
from jax.experimental import pallas as pl
from jax.experimental.pallas import tpu as pltpu

F32 = jnp.float32
BF16 = jnp.bfloat16

D_MODEL = 1024
D_FF = 2816
NORM_EPS = 1e-6
ROPE_THETA = 500000.0
DIFF_HEADS = 4
DIFF_HEAD_DIM = 64
ROT_DIM = DIFF_HEAD_DIM // 4
DIFF_LAMBDA_INIT = 0.8 - 0.6 * math.exp(-0.3 * 0)
S5_GROUPS = 32
S5_GROUP = 16
S5_STATE = 64
S5_NSTATE = S5_GROUPS * S5_STATE
MLSTM_HEADS = 4
MLSTM_QK_DIM = 64
MLSTM_V_DIM = 128
MLSTM_CONV = 4
FOX_HEADS = 8
FOX_HEAD_DIM = 64
HALF = 512

LANES = 128
SUBLANES = 8
VMEM_LIMIT_BYTES = 60 * 1024 * 1024

TOKEN_TILE = 1024
FFN_TILE = 512
FF_CHUNK = 256
ATTN_TILE = 512
BLOCKS_PER_SWEEP = 3
S5_TILE = 1024
S5_CHUNK = 16
MLSTM_CHUNK = 512
GATE_COLS = LANES

NT_DIMS = (((1,), (1,)), ((), ()))
LOG2_E = math.log2(math.e)
ONES_ROWS = 16


def _params(*semantics):
    return pltpu.CompilerParams(dimension_semantics=semantics, vmem_limit_bytes=VMEM_LIMIT_BYTES)


def _resident(shape):
    return pl.BlockSpec(shape, lambda *_: (0,) * len(shape), pipeline_mode=pl.Buffered(1))


def _rms(x, g):
    return x * lax.rsqrt(jnp.mean(x * x, axis=-1, keepdims=True) + NORM_EPS) * g


def _log_sigmoid(x):
    return jnp.minimum(x, 0.0) - jnp.log1p(jnp.exp(-jnp.abs(x)))


def _cumsum_rows(x):
    n = x.shape[0]
    row = lax.broadcasted_iota(jnp.int32, x.shape, 0)
    k = 1
    while k < n:
        x = x + jnp.where(row >= k, pltpu.roll(x, k, 0), 0.0)
        k *= 2
    return x


def _keep_half_rows(x, half):
    rows = x.shape[0] // 2
    zeros = jnp.zeros((rows,) + x.shape[1:], x.dtype)
    return jnp.concatenate([x[:rows], zeros] if half == 0 else [zeros, x[rows:]], axis=0)


def _staged(n, scores, weights, accumulate):
    s, w = {}, {}
    for step in range(n + 2):
        if step < n:
            s[step] = scores(step)
        if step >= 2:
            accumulate(step - 2, *w.pop(step - 2))
        if 1 <= step <= n:
            w[step - 1] = weights(step - 1, s.pop(step - 1))


def _sweep_causal_blocks(i, sweep):
    n = BLOCKS_PER_SWEEP

    def full_group(g, carry):
        sweep([(n * g + j, False) for j in range(n)])
        return carry

    groups = i // n
    lax.fori_loop(0, groups, full_group, 0)
    left = i - n * groups
    for r in range(n):
        @pl.when(left == r)
        def _(r=r):
            sweep([(i - r + j, False) for j in range(r)] + [(i, True)])


def _store_key_blocks(ref, value):
    width = ref.shape[3]
    for blk in range(ref.shape[1]):
        ref[0, blk] = value[:, blk * width:(blk + 1) * width]


def _ffn_body(*refs, has_mix, has_final):
    refs = list(refs)
    x_ref = refs.pop(0)
    if has_mix:
        a_ref, b_ref, wa_ref, wb_ref = refs[:4]
        refs = refs[4:]
    g_ref, wg_hbm, wu_hbm, wd_hbm = refs[:4]
    refs = refs[4:]
    if has_final:
        gf_ref = refs.pop(0)
    o_ref, wg_ref, wu_ref, wd_ref, sem = refs

    n_chunks = D_FF // FF_CHUNK
    chunk = lambda c: slice(c * FF_CHUNK, (c + 1) * FF_CHUNK)

    def chunk_copies(c):
        return (pltpu.make_async_copy(wg_hbm.at[:, chunk(c)], wg_ref.at[:, chunk(c)], sem.at[0, c]),
                pltpu.make_async_copy(wu_hbm.at[:, chunk(c)], wu_ref.at[:, chunk(c)], sem.at[1, c]),
                pltpu.make_async_copy(wd_hbm.at[chunk(c), :], wd_ref.at[chunk(c), :], sem.at[2, c]))

    first = pl.program_id(0) == 0

    @pl.when(first)
    def _():
        for c in range(n_chunks):
            for copy in chunk_copies(c):
                copy.start()

    def tile(wait_for_weights):
        x = x_ref[...]
        if has_mix:
            x = x + jnp.dot(a_ref[...], wa_ref[...], preferred_element_type=F32)
            x = x + jnp.dot(b_ref[...], wb_ref[...], preferred_element_type=F32)
        h = _rms(x, g_ref[...]).astype(BF16)
        acc = jnp.zeros_like(x)
        for c in range(n_chunks):
            if wait_for_weights:
                for copy in chunk_copies(c):
                    copy.wait()
            gate = jnp.dot(h, wg_ref[:, chunk(c)].astype(BF16), preferred_element_type=F32)
            up = jnp.dot(h, wu_ref[:, chunk(c)].astype(BF16), preferred_element_type=F32)
            act = (gate * jax.nn.sigmoid(gate) * up).astype(BF16)
            acc = acc + jnp.dot(act, wd_ref[chunk(c), :].astype(BF16), preferred_element_type=F32)
        y = x + 0.5 * acc
        if has_final:
            y = _rms(y, gf_ref[...])
        o_ref[...] = y

    @pl.when(first)
    def _():
        tile(True)

    @pl.when(jnp.logical_not(first))
    def _():
        tile(False)


def _ffn(x, norm, wg, wu, wd, mix=None, final_norm=None):
    n, d = x.shape
    tm = min(FFN_TILE, n)
    row = lambda i: (i, 0)
    args = [x]
    specs = [pl.BlockSpec((tm, d), row)]
    if mix is not None:
        a, b, wa, wb = mix
        args += [a, b, wa.astype(BF16), wb.astype(BF16)]
        specs += [pl.BlockSpec((tm, HALF), row), pl.BlockSpec((tm, HALF), row),
                  _resident((HALF, d)), _resident((HALF, d))]
    args += [norm.reshape(1, d), wg, wu, wd]
    specs += [_resident((1, d))] + [pl.BlockSpec(memory_space=pl.ANY)] * 3
    if final_norm is not None:
        args.append(final_norm.reshape(1, d))
        specs.append(_resident((1, d)))
    body = functools.partial(_ffn_body, has_mix=mix is not None, has_final=final_norm is not None)
    return pl.pallas_call(
        body,
        grid=(n // tm,),
        in_specs=specs,
        out_specs=pl.BlockSpec((tm, d), row),
        out_shape=jax.ShapeDtypeStruct((n, d), F32),
        scratch_shapes=[pltpu.VMEM((d, D_FF), F32), pltpu.VMEM((d, D_FF), F32), pltpu.VMEM((D_FF, d), F32),
                        pltpu.SemaphoreType.DMA((3, D_FF // FF_CHUNK))],
        compiler_params=_params("arbitrary"),
        name="ffn",
    )(*args)


def _rope_angles(seq):
    pos = np.arange(seq, dtype=np.float32)
    inv = np.float32(ROPE_THETA) ** (-np.arange(0, ROT_DIM, 2, dtype=np.float32) / np.float32(ROT_DIM))
    ang = pos[:, None] * inv[None, :]
    return np.cos(ang), np.sin(ang)


def _rope_lane_tables(seq):
    half = ROT_DIM // 2
    cos, sin = _rope_angles(seq)
    one = np.ones((seq, DIFF_HEAD_DIM - ROT_DIM), np.float32)
    zero8 = np.zeros((seq, half), np.float32)
    zero48 = np.zeros((seq, DIFF_HEAD_DIM - ROT_DIM), np.float32)
    cos_t = np.concatenate([cos, cos, one] * 2, axis=1)
    sa_t = np.concatenate([-sin, zero8, zero48] * 2, axis=1)
    sb_t = np.concatenate([zero8, sin, zero48] * 2, axis=1)
    return cos_t, sa_t, sb_t


def _proj0_body(x_ref, g_ref, wt_ref, wk_ref, wu_ref, cos_ref, sa_ref, sb_ref, cosr_ref, sinr_ref,
                qt_ref, k_ref, vt_ref, u_ref):
    h = _rms(x_ref[...], g_ref[...]).astype(BF16)
    pt = lax.dot_general(wt_ref[...], h, NT_DIMS, preferred_element_type=F32)
    pk = jnp.dot(h, wk_ref[...].astype(BF16), preferred_element_type=F32)
    half = ROT_DIM // 2
    scale = DIFF_HEAD_DIM ** -0.5 * LOG2_E

    cos_r, sin_r = cosr_ref[...], sinr_ref[...]
    for comp in range(2 * DIFF_HEADS):
        r0 = comp * DIFF_HEAD_DIM
        x1, x2 = pt[r0:r0 + half], pt[r0 + half:r0 + ROT_DIM]
        qt_ref[0, r0:r0 + half, :] = ((x1 * cos_r - x2 * sin_r) * scale).astype(BF16)
        qt_ref[0, r0 + half:r0 + ROT_DIM, :] = ((x2 * cos_r + x1 * sin_r) * scale).astype(BF16)
        qt_ref[0, r0 + ROT_DIM:r0 + DIFF_HEAD_DIM, :] = (pt[r0 + ROT_DIM:r0 + DIFF_HEAD_DIM] * scale).astype(BF16)
    _store_key_blocks(vt_ref, pt[HALF:2 * HALF].astype(BF16))

    cos_t, sa_t, sb_t = cos_ref[...], sa_ref[...], sb_ref[...]
    for hd in range(DIFF_HEADS):
        sl = slice(hd * LANES, (hd + 1) * LANES)
        t = pk[:, sl]
        k_ref[:, sl] = (t * cos_t + pltpu.roll(t, LANES - half, 1) * sa_t
                        + pltpu.roll(t, half, 1) * sb_t).astype(BF16)
    u_ref[...] = jnp.dot(h, wu_ref[...].astype(BF16), preferred_element_type=F32)


def _proj0(x, norm, w_in, seq):
    n, d = x.shape
    b = n // seq
    tm = min(TOKEN_TILE, seq)
    tps = seq // tm
    row = lambda i: (i, 0)
    pos = lambda i: (i % tps, 0)
    pos_t = lambda i: (0, i % tps)
    seq_t = lambda i: (i // tps, 0, i % tps)
    ta = min(ATTN_TILE, seq)
    key_blk = lambda i: (i // tps, i % tps, 0, 0)
    cos, sin = _rope_angles(seq)
    w_t = jnp.concatenate([w_in[:, :HALF], w_in[:, 2 * HALF:3 * HALF]], axis=1).T.astype(BF16)
    w_cols = lambda blk: pl.BlockSpec((d, HALF), lambda i: (0, blk), pipeline_mode=pl.Buffered(1))
    t_shape = jax.ShapeDtypeStruct((b, HALF, seq), BF16)
    return pl.pallas_call(
        _proj0_body,
        grid=(n // tm,),
        in_specs=[pl.BlockSpec((tm, d), row), _resident((1, d)), _resident((2 * HALF, d)),
                  w_cols(1), w_cols(3),
                  pl.BlockSpec((tm, LANES), pos), pl.BlockSpec((tm, LANES), pos),
                  pl.BlockSpec((tm, LANES), pos),
                  pl.BlockSpec((ROT_DIM // 2, tm), pos_t), pl.BlockSpec((ROT_DIM // 2, tm), pos_t)],
        out_specs=[pl.BlockSpec((1, HALF, tm), seq_t), pl.BlockSpec((tm, HALF), row),
                   pl.BlockSpec((1, tm // ta, HALF, ta), key_blk), pl.BlockSpec((tm, HALF), row)],
        out_shape=[t_shape, jax.ShapeDtypeStruct((n, HALF), BF16),
                   jax.ShapeDtypeStruct((b, seq // ta, HALF, ta), BF16),
                   jax.ShapeDtypeStruct((n, HALF), F32)],
        compiler_params=_params("parallel"),
        name="proj0",
    )(x, norm.reshape(1, d), w_t, w_in, w_in, *_rope_lane_tables(seq), cos.T, sin.T)


def _diff_attn_body(lam_ref, subln_ref, qt_ref, k_ref, vt_ref, o_ref, m_ref, acc_ref, *, tile):
    i = pl.program_id(1)
    m_ref[...] = jnp.full_like(m_ref, -jnp.inf)
    acc_ref[...] = jnp.zeros_like(acc_ref)
    ones = jnp.ones((ONES_ROWS, tile), BF16)

    def sweep(blocks):
        if any(diagonal for _, diagonal in blocks):
            query = lax.broadcasted_iota(jnp.int32, (tile, 2 * tile), 1)
            query = jnp.where(query >= tile, query - tile, query)
            keep = lax.broadcasted_iota(jnp.int32, (tile, 2 * tile), 0) <= query

        def scores(u):
            blk, hd = divmod(u, DIFF_HEADS)
            key0 = pl.multiple_of(blocks[blk][0] * tile, tile)
            rows = slice(hd * LANES, (hd + 1) * LANES)
            q_t = qt_ref[0, rows, :]
            q_both = jnp.concatenate([_keep_half_rows(q_t, 0), _keep_half_rows(q_t, 1)], axis=1)
            return jnp.dot(k_ref[0, pl.ds(key0, tile), rows], q_both, preferred_element_type=F32)

        def weights(u, s):
            blk, hd = divmod(u, DIFF_HEADS)
            if blocks[blk][1]:
                s = jnp.where(keep, s, -jnp.inf)
            m_old = m_ref[hd]
            m_new = jnp.maximum(m_old, jnp.max(s, axis=0, keepdims=True))
            m_ref[hd] = m_new
            return jnp.exp2(m_old - m_new), jnp.exp2(s - m_new).astype(BF16)

        def accumulate(u, alpha, p):
            blk, hd = divmod(u, DIFF_HEADS)
            v_aug = jnp.concatenate([vt_ref[0, blocks[blk][0], hd * LANES:(hd + 1) * LANES, :], ones], axis=0)
            acc_ref[hd] = alpha * acc_ref[hd] + jnp.dot(v_aug, p, preferred_element_type=F32)

        _staged(DIFF_HEADS * len(blocks), scores, weights, accumulate)

    _sweep_causal_blocks(i, sweep)

    lam_v = lam_ref[...]
    lam = (jnp.exp(jnp.sum(lam_v[0:1] * lam_v[1:2], keepdims=True))
           - jnp.exp(jnp.sum(lam_v[2:3] * lam_v[3:4], keepdims=True)) + DIFF_LAMBDA_INIT)
    for hd in range(DIFF_HEADS):
        a1, a2 = acc_ref[hd, :, :tile], acc_ref[hd, :, tile:]
        o_t = (a1[:LANES] / a1[LANES:LANES + 1] - lam * (a2[:LANES] / a2[LANES:LANES + 1]))
        o_t = o_t * lax.rsqrt(jnp.mean(o_t * o_t, axis=0, keepdims=True) + NORM_EPS)
        o_t = o_t * subln_ref[...] * (1.0 - DIFF_LAMBDA_INIT)
        o_ref[0, :, hd * LANES:(hd + 1) * LANES] = o_t.T.astype(BF16)


def _diff_attention(q_t, k, v_t, lam_vecs, subln):
    b, s, _ = k.shape
    n_blk, _, t = v_t.shape[1:]
    const = lambda bi, i: (0, 0)
    return pl.pallas_call(
        functools.partial(_diff_attn_body, tile=t),
        grid=(b, n_blk),
        in_specs=[pl.BlockSpec((4, DIFF_HEAD_DIM), const), pl.BlockSpec((LANES, 1), const),
                  pl.BlockSpec((1, HALF, t), lambda bi, i: (bi, 0, i)),
                  pl.BlockSpec((1, s, HALF), lambda bi, i: (bi, 0, 0)),
                  pl.BlockSpec((1, n_blk, HALF, t), lambda bi, i: (bi, 0, 0, 0))],
        out_specs=pl.BlockSpec((1, t, HALF), lambda bi, i: (bi, i, 0)),
        out_shape=jax.ShapeDtypeStruct((b, s, HALF), BF16),
        scratch_shapes=[pltpu.VMEM((DIFF_HEADS, 1, 2 * t), F32),
                        pltpu.VMEM((DIFF_HEADS, LANES + ONES_ROWS, 2 * t), F32)],
        compiler_params=_params("parallel", "arbitrary"),
        name="diff_attn",
    )(lam_vecs, subln.reshape(LANES, 1), q_t, k, v_t)


def _s5_discretize_body(lr_ref, li_ref, ldt_ref, br_ref, bi_ref, k_ref, pr_ref, pi_ref, bbr_ref, bbi_ref):
    lr, li = lr_ref[...], li_ref[...]
    dt = jnp.exp(ldt_ref[...])
    mag = jnp.exp(lr * dt)
    ar, ai = mag * jnp.cos(li * dt), mag * jnp.sin(li * dt)
    den = lr * lr + li * li
    gr = ((ar - 1.0) * lr + ai * li) / den
    gi = (ai * lr - (ar - 1.0) * li) / den
    br, bi = br_ref[...], bi_ref[...]
    bbr_ref[...] = gr * br - gi * bi
    bbi_ref[...] = gr * bi + gi * br
    steps = k_ref[...]
    mag_k = jnp.exp(lr * dt * steps)
    pr_ref[...] = mag_k * jnp.cos(li * dt * steps)
    pi_ref[...] = mag_k * jnp.sin(li * dt * steps)


def _s5_discretize(a_re, a_im, log_dt, b_re, b_im):
    row = lambda a: a.reshape(1, S5_NSTATE)
    ldt = jnp.broadcast_to(log_dt[:, None], (S5_GROUPS, S5_STATE))
    b_t = lambda a: a.transpose(2, 0, 1).reshape(S5_GROUP, S5_NSTATE)
    steps = jnp.arange(1, S5_CHUNK + 1, dtype=F32).reshape(S5_CHUNK, 1)
    pshape = jax.ShapeDtypeStruct((S5_CHUNK, S5_NSTATE), F32)
    bshape = jax.ShapeDtypeStruct((S5_GROUP, S5_NSTATE), F32)
    return pl.pallas_call(
        _s5_discretize_body,
        out_shape=[pshape, pshape, bshape, bshape],
        name="s5_discretize",
    )(row(a_re), row(a_im), row(ldt), b_t(b_re), b_t(b_im), steps)


def _s5_body(u_ref, pr_ref, pi_ref, bbr_ref, bbi_ref, cr_ref, ci_ref, d_ref, wglu_ref, bglu_ref,
             o_ref, us_ref, ys_ref, xr_ref, xi_ref, hin_ref, carry_ref, *, tile):
    n_chunks = tile // S5_CHUNK
    slabs = HALF // LANES
    gw = S5_NSTATE // slabs

    @pl.when(pl.program_id(1) == 0)
    def _():
        carry_ref[...] = jnp.zeros_like(carry_ref)

    u = u_ref[0]
    for g in range(slabs):
        us_ref[g] = u[:, g * LANES:(g + 1) * LANES]
    u_perm = jnp.concatenate(
        [jnp.concatenate([us_ref[g, pl.ds(st, n_chunks, stride=S5_CHUNK), :] for g in range(slabs)], axis=1)
         for st in range(S5_CHUNK)], axis=0).astype(BF16)

    for g in range(slabs):
        ug = u_perm[:, g * LANES:(g + 1) * LANES]
        xr_ref[:, g * gw:(g + 1) * gw] = jnp.dot(ug, bbr_ref[g], preferred_element_type=F32)
        xi_ref[:, g * gw:(g + 1) * gw] = jnp.dot(ug, bbi_ref[g], preferred_element_type=F32)

    ar, ai = pr_ref[0:1, :], pi_ref[0:1, :]
    hr, hi = xr_ref[0:n_chunks, :], xi_ref[0:n_chunks, :]
    for st in range(1, S5_CHUNK):
        rows = slice(st * n_chunks, (st + 1) * n_chunks)
        hr, hi = ar * hr - ai * hi + xr_ref[rows, :], ar * hi + ai * hr + xi_ref[rows, :]
        xr_ref[rows, :] = hr
        xi_ref[rows, :] = hi

    last = (S5_CHUNK - 1) * n_chunks
    ac_r, ac_i = pr_ref[S5_CHUNK - 1:S5_CHUNK, :], pi_ref[S5_CHUNK - 1:S5_CHUNK, :]

    def carry_step(c, carry):
        cr, ci = carry
        hin_ref[0, pl.ds(c, 1), :] = cr
        hin_ref[1, pl.ds(c, 1), :] = ci
        zr, zi = xr_ref[pl.ds(last + c, 1), :], xi_ref[pl.ds(last + c, 1), :]
        return ac_r * cr - ac_i * ci + zr, ac_r * ci + ac_i * cr + zi

    cr, ci = lax.fori_loop(0, n_chunks, carry_step, (carry_ref[0:1, :], carry_ref[1:2, :]), unroll=4)
    carry_ref[0:1, :] = cr
    carry_ref[1:2, :] = ci

    hin_r, hin_i = hin_ref[0], hin_ref[1]
    for st in range(S5_CHUNK):
        rows = slice(st * n_chunks, (st + 1) * n_chunks)
        p_r, p_i = pr_ref[st:st + 1, :], pi_ref[st:st + 1, :]
        xr_ref[rows, :] = xr_ref[rows, :] + (p_r * hin_r - p_i * hin_i)
        xi_ref[rows, :] = xi_ref[rows, :] + (p_r * hin_i + p_i * hin_r)
    for g in range(slabs):
        cols = slice(g * gw, (g + 1) * gw)
        yg = (jnp.dot(xr_ref[:, cols].astype(BF16), cr_ref[g], preferred_element_type=F32)
              - jnp.dot(xi_ref[:, cols].astype(BF16), ci_ref[g], preferred_element_type=F32))
        for st in range(S5_CHUNK):
            ys_ref[g, pl.ds(st, n_chunks, stride=S5_CHUNK), :] = yg[st * n_chunks:(st + 1) * n_chunks]

    y = jnp.concatenate([ys_ref[g] for g in range(slabs)], axis=1) + d_ref[...] * u
    z = jax.nn.gelu(y, approximate=True)
    gate = jnp.dot(z.astype(BF16), wglu_ref[...], preferred_element_type=F32) + bglu_ref[...]
    o_ref[0] = (z * jax.nn.sigmoid(gate)).astype(BF16)


def _s5(u, a_re, a_im, log_dt, b_re, b_im, c_re, c_im, d_skip, w_glu, b_glu):
    b, s, w = u.shape
    t = min(S5_TILE, s)
    slabs = w // LANES
    per = S5_GROUPS // slabs
    gw = S5_NSTATE // slabs
    pw_r, pw_i, bbr, bbi = _s5_discretize(a_re, a_im, log_dt, b_re, b_im)
    eye = jnp.eye(per, dtype=F32)

    def block_in(m):
        m = m.reshape(S5_GROUP, slabs, per, S5_STATE)
        return jnp.einsum('csgp,gh->sgchp', m, eye).reshape(slabs, LANES, gw).astype(BF16)

    def block_out(m):
        m = m.reshape(slabs, per, S5_GROUP, S5_STATE)
        return jnp.einsum('sgcp,gh->sgphc', m, eye).reshape(slabs, gw, LANES).astype(BF16)

    seq_blk = lambda bi, ti: (bi, ti, 0)
    n_chunks = t // S5_CHUNK
    return pl.pallas_call(
        functools.partial(_s5_body, tile=t),
        grid=(b, s // t),
        in_specs=[pl.BlockSpec((1, t, w), seq_blk),
                  _resident((S5_CHUNK, S5_NSTATE)), _resident((S5_CHUNK, S5_NSTATE)),
                  _resident((slabs, LANES, gw)), _resident((slabs, LANES, gw)),
                  _resident((slabs, gw, LANES)), _resident((slabs, gw, LANES)),
                  _resident((1, w)), _resident((w, w)), _resident((1, w))],
        out_specs=pl.BlockSpec((1, t, w), seq_blk),
        out_shape=jax.ShapeDtypeStruct((b, s, w), BF16),
        scratch_shapes=[pltpu.VMEM((slabs, t, LANES), F32), pltpu.VMEM((slabs, t, LANES), F32),
                        pltpu.VMEM((t, S5_NSTATE), F32), pltpu.VMEM((t, S5_NSTATE), F32),
                        pltpu.VMEM((2, n_chunks, S5_NSTATE), F32), pltpu.VMEM((2, S5_NSTATE), F32)],
        compiler_params=_params("parallel", "arbitrary"),
        name="s5",
    )(u, pw_r, pw_i, block_in(bbr), block_in(bbi),
      block_out(c_re), block_out(c_im), d_skip.reshape(1, w), w_glu.astype(BF16), b_glu.reshape(1, w))


def _proj1_body(x_ref, g_ref, wt_ref, wm_ref, wo_ref, wk_ref, wg_ref,
                qk_ref, vm_ref, om_ref, qft_ref, kf_ref, vft_ref, gt_ref):
    h = _rms(x_ref[...], g_ref[...]).astype(BF16)
    pt = lax.dot_general(wt_ref[...], h, NT_DIMS, preferred_element_type=F32)
    qft_ref[0] = (pt[:HALF] * (FOX_HEAD_DIM ** -0.5 * LOG2_E)).astype(BF16)
    _store_key_blocks(vft_ref, pt[HALF:].astype(BF16))
    pm = jnp.dot(h, wm_ref[...].astype(BF16), preferred_element_type=F32)
    qk_ref[...] = pm[:, 0:HALF]
    vm_ref[...] = pm[:, HALF:2 * HALF].astype(BF16)
    om_ref[...] = jnp.dot(h, wo_ref[...].astype(BF16), preferred_element_type=F32)
    kf_ref[...] = jnp.dot(h, wk_ref[...].astype(BF16), preferred_element_type=F32).astype(BF16)
    gt_ref[...] = jnp.dot(h, wg_ref[...].astype(BF16), preferred_element_type=F32)


def _proj1(x, norm, w_in, seq):
    n, d = x.shape
    b = n // seq
    tm = min(TOKEN_TILE, seq)
    tps = seq // tm
    row = lambda i: (i, 0)
    seq_t = lambda i: (i // tps, 0, i % tps)
    ta = min(ATTN_TILE, seq)
    key_blk = lambda i: (i // tps, i % tps, 0, 0)
    o = 2 * HALF
    g = 2 * MLSTM_HEADS
    om = o + g
    w_o = w_in[:, om:om + HALF]
    w_k = w_in[:, om + 2 * HALF:om + 3 * HALF]
    w_g = jnp.concatenate([w_in[:, o:om], w_in[:, om + 4 * HALF:],
                           jnp.zeros((d, GATE_COLS - g - FOX_HEADS), w_in.dtype)], axis=1)
    w_t = jnp.concatenate([w_in[:, om + HALF:om + 2 * HALF], w_in[:, om + 3 * HALF:om + 4 * HALF]],
                          axis=1).T.astype(BF16)
    f32_half = jax.ShapeDtypeStruct((n, HALF), F32)
    bf_half = jax.ShapeDtypeStruct((n, HALF), BF16)
    t_shape = jax.ShapeDtypeStruct((b, HALF, seq), BF16)
    half_spec = pl.BlockSpec((tm, HALF), row)
    t_spec = pl.BlockSpec((1, HALF, tm), seq_t)
    return pl.pallas_call(
        _proj1_body,
        grid=(n // tm,),
        in_specs=[pl.BlockSpec((tm, d), row), _resident((1, d)), _resident((2 * HALF, d)),
                  pl.BlockSpec((d, o), lambda i: (0, 0), pipeline_mode=pl.Buffered(1)),
                  _resident((d, HALF)), _resident((d, HALF)), _resident((d, GATE_COLS))],
        out_specs=[half_spec, half_spec, half_spec, t_spec, half_spec,
                   pl.BlockSpec((1, tm // ta, HALF, ta), key_blk), pl.BlockSpec((tm, GATE_COLS), row)],
        out_shape=[f32_half, bf_half, f32_half, t_shape, bf_half,
                   jax.ShapeDtypeStruct((b, seq // ta, HALF, ta), BF16),
                   jax.ShapeDtypeStruct((n, GATE_COLS), F32)],
        compiler_params=_params("parallel"),
        name="proj1",
    )(x, norm.reshape(1, d), w_t, w_in, w_o, w_k, w_g)


def _mlstm_body(qk_ref, v_ref, og_ref, gt_ref, cw_ref, cb_ref, gb_ref, on_ref, out_ref,
                xbuf_ref, cn_ref, m_ref, *, chunk):
    pad = SUBLANES
    dk, dv = MLSTM_QK_DIM, MLSTM_V_DIM

    @pl.when(pl.program_id(1) == 0)
    def _():
        xbuf_ref[0:pad, :] = jnp.zeros((pad, HALF), F32)
        cn_ref[...] = jnp.zeros_like(cn_ref)
        m_ref[...] = jnp.zeros_like(m_ref)

    x = qk_ref[0]
    xbuf_ref[pad:pad + chunk, :] = x
    a = jnp.zeros_like(x) + cb_ref[...]
    for tap in range(MLSTM_CONV):
        a = a + cw_ref[tap:tap + 1, :] * xbuf_ref[pl.ds(pad - (MLSTM_CONV - 1) + tap, chunk), :]
    xbuf_ref[0:pad, :] = x[chunk - pad:chunk, :]
    a = a * jax.nn.sigmoid(a)
    q = a[:, :MLSTM_HEADS * dk] * dk ** -0.5
    k = a[:, MLSTM_HEADS * dk:]
    k_t = k.T

    gi = gt_ref[0] + gb_ref[...]
    bcum = _cumsum_rows(_log_sigmoid(gi))
    gi_t = gi.T
    bcum_t = bcum.T
    tri = (lax.broadcasted_iota(jnp.int32, (chunk, chunk), 1)
           <= lax.broadcasted_iota(jnp.int32, (chunk, chunk), 0))
    lane = lax.broadcasted_iota(jnp.int32, (1, LANES), 1)
    v_all = v_ref[0]
    o_gate = og_ref[0]
    ones = jnp.ones((chunk, dv), BF16)

    states = [cn_ref[pair] for pair in range(MLSTM_HEADS // 2)]

    def products(hd):
        pair, odd = hd // 2, hd % 2
        sel = (lane >= dk) if odd else (lane < dk)
        q2 = jnp.where(sel, q[:, pair * LANES:(pair + 1) * LANES], 0.0).astype(BF16)
        k2 = k[:, pair * LANES:(pair + 1) * LANES].astype(BF16)
        qk = lax.dot_general(q2, k2, NT_DIMS, preferred_element_type=F32)
        return qk, jnp.dot(q2, states[pair].astype(BF16), preferred_element_type=F32)

    def gate_weights(hd, prods):
        qk, q_state = prods
        f_lane = MLSTM_HEADS + hd
        b_col = bcum[:, f_lane:f_lane + 1]
        b_row = bcum_t[f_lane:f_lane + 1, :]
        i_row = gi_t[hd:hd + 1, :]
        m_prev = m_ref[hd]
        dm = jnp.where(tri, b_col - b_row + i_row, -jnp.inf)
        g = b_col + m_prev
        mt = jnp.maximum(g, jnp.max(dm, axis=-1, keepdims=True))
        s = (qk * jnp.exp(dm - mt)).astype(BF16)
        w_inter = jnp.exp(g - mt)

        b_last = bcum[chunk - 1:chunk, f_lane:f_lane + 1]
        dec = b_last - b_row + i_row
        m_new = jnp.maximum(b_last + m_prev, jnp.max(dec, axis=-1, keepdims=True))
        k_w = (k_t[hd * dk:(hd + 1) * dk, :] * jnp.exp(dec - m_new)).astype(BF16)
        carry_w = jnp.exp(b_last + m_prev - m_new)
        m_ref[hd] = m_new
        return s, k_w, q_state, w_inter, mt, carry_w

    def outputs(hd, s, k_w, q_state, w_inter, mt, carry_w):
        pair, odd = hd // 2, hd % 2
        v_aug = jnp.concatenate([v_all[:, hd * dv:(hd + 1) * dv], ones], axis=1)
        s_v = jnp.dot(s, v_aug, preferred_element_type=F32)
        num = w_inter * q_state[:, :dv] + s_v[:, :dv]
        den = w_inter * q_state[:, dv:] + s_v[:, dv:]
        h_out = num / jnp.maximum(jnp.abs(den), jnp.exp(-mt))
        rows = slice(odd * dk, (odd + 1) * dk)
        cn_ref[pair, rows, :] = (carry_w * states[pair][rows, :]
                                 + jnp.dot(k_w, v_aug, preferred_element_type=F32))
        cols = slice(hd * dv, (hd + 1) * dv)
        h_norm = _rms(h_out, on_ref[hd:hd + 1, :])
        out_ref[0, :, cols] = (jax.nn.sigmoid(o_gate[:, cols]) * h_norm).astype(BF16)

    _staged(MLSTM_HEADS, products, gate_weights, outputs)


def _mlstm(qk, v, o_gate, gates, conv_w, conv_b, b_i, b_f, out_norm):
    b, s, _ = qk.shape
    c = min(MLSTM_CHUNK, s)
    gate_bias = jnp.concatenate([b_i, b_f, jnp.zeros((GATE_COLS - 2 * MLSTM_HEADS,), F32)]).reshape(1, GATE_COLS)
    seq_blk = lambda bi, ci: (bi, ci, 0)
    half_spec = pl.BlockSpec((1, c, HALF), seq_blk)
    return pl.pallas_call(
        functools.partial(_mlstm_body, chunk=c),
        grid=(b, s // c),
        in_specs=[half_spec, half_spec, half_spec, pl.BlockSpec((1, c, GATE_COLS), seq_blk),
                  _resident((MLSTM_CONV, HALF)), _resident((1, HALF)), _resident((1, GATE_COLS)),
                  _resident((MLSTM_HEADS, MLSTM_V_DIM))],
        out_specs=half_spec,
        out_shape=jax.ShapeDtypeStruct((b, s, HALF), BF16),
        scratch_shapes=[pltpu.VMEM((c + SUBLANES, HALF), F32),
                        pltpu.VMEM((MLSTM_HEADS // 2, 2 * MLSTM_QK_DIM, 2 * MLSTM_V_DIM), F32),
                        pltpu.VMEM((MLSTM_HEADS, 1, 1), F32)],
        compiler_params=_params("parallel", "arbitrary"),
        name="mlstm",
    )(qk, v, o_gate, gates, conv_w, conv_b.reshape(1, HALF), gate_bias, out_norm)


FOX_GATE_LANE = 2 * MLSTM_HEADS
F_TERMS = 3
FQ_ROWS = 16


def _fox_placements():
    sel_k = np.zeros((F_TERMS * GATE_COLS, HALF), np.float32)
    sel_q = np.zeros((FOX_HEADS * FQ_ROWS, F_TERMS * GATE_COLS), np.float32)
    ones_k = np.zeros((1, HALF), np.float32)
    ones_q = np.zeros((FOX_HEADS * FQ_ROWS, 1), np.float32)
    for head in range(FOX_HEADS):
        pair, c = divmod(head, 2)
        for term in range(F_TERMS):
            src = term * GATE_COLS + FOX_GATE_LANE + head
            sel_k[src, pair * LANES + F_TERMS * c + term] = -1.0
            sel_q[head * FQ_ROWS + 2 * F_TERMS + term, src] = 1.0
            ones_k[0, pair * LANES + 2 * F_TERMS + term] = 1.0
            ones_q[head * FQ_ROWS + F_TERMS * c + term, 0] = 1.0
    return (jnp.asarray(sel_k, BF16), jnp.asarray(ones_k), jnp.asarray(sel_q, BF16), jnp.asarray(ones_q))


def _fox_cumsum_body(gt_ref, gb_ref, sel_k_ref, ones_k_ref, sel_q_ref, ones_q_ref, fq_ref, fk_ref):
    f = _cumsum_rows(_log_sigmoid(gt_ref[0] + gb_ref[...])) * LOG2_E
    hi = f.astype(BF16)
    rest = f - hi.astype(F32)
    mid = rest.astype(BF16)
    lo = (rest - mid.astype(F32)).astype(BF16)
    terms = jnp.concatenate([hi, mid, lo], axis=1)
    fk_ref[0] = (jnp.dot(terms, sel_k_ref[...], preferred_element_type=F32) + ones_k_ref[...]).astype(BF16)
    fq_t = lax.dot_general(sel_q_ref[...], terms, NT_DIMS, preferred_element_type=F32)
    fq_ref[0] = (fq_t + ones_q_ref[...]).astype(BF16)


def _fox_cumsum(gates, fox_b_f):
    b, s, _ = gates.shape
    bias = jnp.concatenate([jnp.zeros((FOX_GATE_LANE,), F32), fox_b_f,
                            jnp.zeros((GATE_COLS - FOX_GATE_LANE - FOX_HEADS,), F32)]).reshape(1, GATE_COLS)
    placements = _fox_placements()
    return pl.pallas_call(
        _fox_cumsum_body,
        grid=(b,),
        in_specs=[pl.BlockSpec((1, s, GATE_COLS), lambda bi: (bi, 0, 0)), _resident((1, GATE_COLS))]
                 + [_resident(p.shape) for p in placements],
        out_specs=[pl.BlockSpec((1, FOX_HEADS * FQ_ROWS, s), lambda bi: (bi, 0, 0)),
                   pl.BlockSpec((1, s, HALF), lambda bi: (bi, 0, 0))],
        out_shape=[jax.ShapeDtypeStruct((b, FOX_HEADS * FQ_ROWS, s), BF16),
                   jax.ShapeDtypeStruct((b, s, HALF), BF16)],
        compiler_params=_params("parallel"),
        name="fox_cumsum",
    )(gates, bias, *placements)


def _fox_attn_body(qt_ref, fq_ref, k_ref, fk_ref, vt_ref, o_ref, m_ref, acc_ref, *, tile):
    i = pl.program_id(1)
    dh = FOX_HEAD_DIM
    unused_rows = jnp.zeros((LANES - FQ_ROWS, tile), BF16)
    m_ref[...] = jnp.full_like(m_ref, -jnp.inf)
    acc_ref[...] = jnp.zeros_like(acc_ref)
    ones = jnp.ones((ONES_ROWS, tile), BF16)

    n_pairs = FOX_HEADS // 2

    def sweep(blocks):
        if any(diagonal for _, diagonal in blocks):
            query = lax.broadcasted_iota(jnp.int32, (tile, 2 * tile), 1)
            query = jnp.where(query >= tile, query - tile, query)
            keep = lax.broadcasted_iota(jnp.int32, (tile, 2 * tile), 0) <= query

        def scores(u):
            blk, pair = divmod(u, n_pairs)
            key0 = pl.multiple_of(blocks[blk][0] * tile, tile)
            rows = slice(pair * LANES, (pair + 1) * LANES)
            q_t = qt_ref[0, rows, :]
            k_aug = jnp.concatenate([k_ref[0, pl.ds(key0, tile), rows],
                                     fk_ref[0, pl.ds(key0, tile), rows]], axis=1)
            q_aug = []
            for c in range(2):
                hd = 2 * pair + c
                q_aug.append(jnp.concatenate(
                    [_keep_half_rows(q_t, c), fq_ref[0, hd * FQ_ROWS:(hd + 1) * FQ_ROWS, :], unused_rows], axis=0))
            return jnp.dot(k_aug, jnp.concatenate(q_aug, axis=1), preferred_element_type=F32)

        def weights(u, s):
            blk, pair = divmod(u, n_pairs)
            if blocks[blk][1]:
                s = jnp.where(keep, s, -jnp.inf)
            m_old = m_ref[pair]
            m_new = jnp.maximum(m_old, jnp.max(s, axis=0, keepdims=True))
            m_ref[pair] = m_new
            return jnp.exp2(m_old - m_new), jnp.exp2(s - m_new).astype(BF16)

        def accumulate(u, alpha, p):
            blk, pair = divmod(u, n_pairs)
            for c in range(2):
                hd = 2 * pair + c
                cols = slice(c * tile, (c + 1) * tile)
                v_aug = jnp.concatenate([vt_ref[0, blocks[blk][0], hd * dh:(hd + 1) * dh, :], ones], axis=0)
                acc_ref[hd] = alpha[:, cols] * acc_ref[hd] + jnp.dot(v_aug, p[:, cols], preferred_element_type=F32)

        _staged(n_pairs * len(blocks), scores, weights, accumulate)

    _sweep_causal_blocks(i, sweep)

    for pair in range(FOX_HEADS // 2):
        a0, a1 = acc_ref[2 * pair], acc_ref[2 * pair + 1]
        o_t = jnp.concatenate([a0[:dh] / a0[dh:dh + 1], a1[:dh] / a1[dh:dh + 1]], axis=0)
        o_ref[0, :, pair * LANES:(pair + 1) * LANES] = o_t.T.astype(BF16)


def _fox_attention(q_t, fq_t, k, fk, v_t):
    b, s, _ = k.shape
    n_blk, _, t = v_t.shape[1:]
    whole = lambda bi, i: (bi, 0, 0)
    return pl.pallas_call(
        functools.partial(_fox_attn_body, tile=t),
        grid=(b, n_blk),
        in_specs=[pl.BlockSpec((1, HALF, t), lambda bi, i: (bi, 0, i)),
                  pl.BlockSpec((1, FOX_HEADS * FQ_ROWS, t), lambda bi, i: (bi, 0, i)),
                  pl.BlockSpec((1, s, HALF), whole), pl.BlockSpec((1, s, HALF), whole),
                  pl.BlockSpec((1, n_blk, HALF, t), lambda bi, i: (bi, 0, 0, 0))],
        out_specs=pl.BlockSpec((1, t, HALF), lambda bi, i: (bi, i, 0)),
        out_shape=jax.ShapeDtypeStruct((b, s, HALF), BF16),
        scratch_shapes=[pltpu.VMEM((FOX_HEADS // 2, 1, 2 * t), F32),
                        pltpu.VMEM((FOX_HEADS, FOX_HEAD_DIM + ONES_ROWS, t), F32)],
        compiler_params=_params("parallel", "arbitrary"),
        name="fox_attn",
    )(q_t, fq_t, k, fk, v_t)


def kernel(x, l0_ffn1_norm, l0_ffn1_w_gate, l0_ffn1_w_up, l0_ffn1_w_down, l0_mix_norm, l0_w_in, l0_diff_lambda_q1, l0_diff_lambda_k1, l0_diff_lambda_q2, l0_diff_lambda_k2, l0_diff_subln, l0_s5_a_re, l0_s5_a_im, l0_s5_log_dt, l0_s5_b_re, l0_s5_b_im, l0_s5_c_re, l0_s5_c_im, l0_s5_d, l0_s5_w_glu, l0_s5_b_glu, l0_w_out, l0_ffn2_norm, l0_ffn2_w_gate, l0_ffn2_w_up, l0_ffn2_w_down, l1_ffn1_norm, l1_ffn1_w_gate, l1_ffn1_w_up, l1_ffn1_w_down, l1_mix_norm, l1_w_in, l1_mlstm_conv_w, l1_mlstm_conv_b, l1_mlstm_b_i, l1_mlstm_b_f, l1_mlstm_out_norm, l1_fox_b_f, l1_w_out, l1_ffn2_norm, l1_ffn2_w_gate, l1_ffn2_w_up, l1_ffn2_w_down, final_norm):
    b, s, d = x.shape
    n = b * s
    seq = lambda a: a.reshape(b, s, a.shape[-1])
    flat = lambda a: a.reshape(n, a.shape[-1])

    x = _ffn(x.reshape(n, d), l0_ffn1_norm, l0_ffn1_w_gate, l0_ffn1_w_up, l0_ffn1_w_down)
    q_t, k, v_t, u = _proj0(x, l0_mix_norm, l0_w_in, s)
    lam_vecs = jnp.stack([l0_diff_lambda_q1, l0_diff_lambda_k1, l0_diff_lambda_q2, l0_diff_lambda_k2])
    ya = _diff_attention(q_t, seq(k), v_t, lam_vecs, l0_diff_subln)
    yb = _s5(seq(u), l0_s5_a_re, l0_s5_a_im, l0_s5_log_dt, l0_s5_b_re, l0_s5_b_im,
             l0_s5_c_re, l0_s5_c_im, l0_s5_d, l0_s5_w_glu, l0_s5_b_glu)
    x = _ffn(x, l0_ffn2_norm, l0_ffn2_w_gate, l0_ffn2_w_up, l0_ffn2_w_down,
             mix=(flat(ya), flat(yb), l0_w_out[:HALF], l0_w_out[HALF:]))

    x = _ffn(x, l1_ffn1_norm, l1_ffn1_w_gate, l1_ffn1_w_up, l1_ffn1_w_down)
    qk_m, v_m, o_m, qf_t, k_f, vf_t, gates = _proj1(x, l1_mix_norm, l1_w_in, s)
    hm = _mlstm(seq(qk_m), seq(v_m), seq(o_m), seq(gates), l1_mlstm_conv_w, l1_mlstm_conv_b,
                l1_mlstm_b_i, l1_mlstm_b_f, l1_mlstm_out_norm)
    fq_t, fk = _fox_cumsum(seq(gates), l1_fox_b_f)
    hf = _fox_attention(qf_t, fq_t, seq(k_f), fk, vf_t)
    x = _ffn(x, l1_ffn2_norm, l1_ffn2_w_gate, l1_ffn2_w_up, l1_ffn2_w_down,
             mix=(flat(hm), flat(hf), l1_w_out[:HALF], l1_w_out[HALF:]), final_norm=final_norm)
    return x.reshape(b, s, d)
```

```python
import functools
import math

import jax
import jax.numpy as jnp
import numpy as np
from jax import lax
from jax.experimental import pallas as pl
from jax.experimental.pallas import tpu as pltpu

F32 = jnp.float32
BF16 = jnp.bfloat16

D_MODEL = 1024
D_FF = 2816
NORM_EPS = 1e-6
ROPE_THETA = 500000.0
DIFF_HEADS = 4
DIFF_HEAD_DIM = 64
ROT_DIM = DIFF_HEAD_DIM // 4
DIFF_LAMBDA_INIT = 0.8 - 0.6 * math.exp(-0.3 * 0)
S5_GROUPS = 32
S5_GROUP = 16
S5_STATE = 64
S5_NSTATE = S5_GROUPS * S5_STATE
MLSTM_HEADS = 4
MLSTM_QK_DIM = 64
MLSTM_V_DIM = 128
MLSTM_CONV = 4
FOX_HEADS = 8
FOX_HEAD_DIM = 64
HALF = 512

LANES = 128
SUBLANES = 8
VMEM_LIMIT_BYTES = 60 * 1024 * 1024

TOKEN_TILE = 1024
FFN_TILE = 512
FF_CHUNK = 256
ATTN_TILE = 512
BLOCKS_PER_SWEEP = 3
S5_TILE = 1024
S5_CHUNK = 16
MLSTM_CHUNK = 512
GATE_COLS = LANES

NT_DIMS = (((1,), (1,)), ((), ()))
LOG2_E = math.log2(math.e)
ONES_ROWS = 16


def _params(*semantics):
    return pltpu.CompilerParams(dimension_semantics=semantics, vmem_limit_bytes=VMEM_LIMIT_BYTES)


def _resident(shape):
    return pl.BlockSpec(shape, lambda *_: (0,) * len(shape), pipeline_mode=pl.Buffered(1))


def _rms(x, g):
    return x * lax.rsqrt(jnp.mean(x * x, axis=-1, keepdims=True) + NORM_EPS) * g


def _log_sigmoid(x):
    return jnp.minimum(x, 0.0) - jnp.log1p(jnp.exp(-jnp.abs(x)))


def _cumsum_rows(x):
    n = x.shape[0]
    row = lax.broadcasted_iota(jnp.int32, x.shape, 0)
    k = 1
    while k < n:
        x = x + jnp.where(row >= k, pltpu.roll(x, k, 0), 0.0)
        k *= 2
    return x


def _keep_half_rows(x, half):
    rows = x.shape[0] // 2
    zeros = jnp.zeros((rows,) + x.shape[1:], x.dtype)
    return jnp.concatenate([x[:rows], zeros] if half == 0 else [zeros, x[rows:]], axis=0)


def _staged(n, scores, weights, accumulate):
    s, w = {}, {}
    for step in range(n + 2):
        if step < n:
            s[step] = scores(step)
        if step >= 2:
            accumulate(step - 2, *w.pop(step - 2))
        if 1 <= step <= n:
            w[step - 1] = weights(step - 1, s.pop(step - 1))


def _sweep_causal_blocks(i, sweep):
    n = BLOCKS_PER_SWEEP

    def full_group(g, carry):
        sweep([(n * g + j, False) for j in range(n)])
        return carry

    groups = i // n
    lax.fori_loop(0, groups, full_group, 0)
    left = i - n * groups
    for r in range(n):
        @pl.when(left == r)
        def _(r=r):
            sweep([(i - r + j, False) for j in range(r)] + [(i, True)])


def _store_key_blocks(ref, value):
    width = ref.shape[3]
    for blk in range(ref.shape[1]):
        ref[0, blk] = value[:, blk * width:(blk + 1) * width]


def _ffn_body(*refs, has_mix, has_final):
    refs = list(refs)
    x_ref = refs.pop(0)
    if has_mix:
        a_ref, b_ref, wa_ref, wb_ref = refs[:4]
        refs = refs[4:]
    g_ref, wg_hbm, wu_hbm, wd_hbm = refs[:4]
    refs = refs[4:]
    if has_final:
        gf_ref = refs.pop(0)
    o_ref, wg_ref, wu_ref, wd_ref, sem = refs

    n_chunks = D_FF // FF_CHUNK
    chunk = lambda c: slice(c * FF_CHUNK, (c + 1) * FF_CHUNK)

    def chunk_copies(c):
        return (pltpu.make_async_copy(wg_hbm.at[:, chunk(c)], wg_ref.at[:, chunk(c)], sem.at[0, c]),
                pltpu.make_async_copy(wu_hbm.at[:, chunk(c)], wu_ref.at[:, chunk(c)], sem.at[1, c]),
                pltpu.make_async_copy(wd_hbm.at[chunk(c), :], wd_ref.at[chunk(c), :], sem.at[2, c]))

    first = pl.program_id(0) == 0

    @pl.when(first)
    def _():
        for c in range(n_chunks):
            for copy in chunk_copies(c):
                copy.start()

    def tile(wait_for_weights):
        x = x_ref[...]
        if has_mix:
            x = x + jnp.dot(a_ref[...], wa_ref[...], preferred_element_type=F32)
            x = x + jnp.dot(b_ref[...], wb_ref[...], preferred_element_type=F32)
        h = _rms(x, g_ref[...]).astype(BF16)
        acc = jnp.zeros_like(x)
        for c in range(n_chunks):
            if wait_for_weights:
                for copy in chunk_copies(c):
                    copy.wait()
            gate = jnp.dot(h, wg_ref[:, chunk(c)].astype(BF16), preferred_element_type=F32)
            up = jnp.dot(h, wu_ref[:, chunk(c)].astype(BF16), preferred_element_type=F32)
            act = (gate * jax.nn.sigmoid(gate) * up).astype(BF16)
            acc = acc + jnp.dot(act, wd_ref[chunk(c), :].astype(BF16), preferred_element_type=F32)
        y = x + 0.5 * acc
        if has_final:
            y = _rms(y, gf_ref[...])
        o_ref[...] = y

    @pl.when(first)
    def _():
        tile(True)

    @pl.when(jnp.logical_not(first))
    def _():
        tile(False)


def _ffn(x, norm, wg, wu, wd, mix=None, final_norm=None):
    n, d = x.shape
    tm = min(FFN_TILE, n)
    row = lambda i: (i, 0)
    args = [x]
    specs = [pl.BlockSpec((tm, d), row)]
    if mix is not None:
        a, b, wa, wb = mix
        args += [a, b, wa.astype(BF16), wb.astype(BF16)]
        specs += [pl.BlockSpec((tm, HALF), row), pl.BlockSpec((tm, HALF), row),
                  _resident((HALF, d)), _resident((HALF, d))]
    args += [norm.reshape(1, d), wg, wu, wd]
    specs += [_resident((1, d))] + [pl.BlockSpec(memory_space=pl.ANY)] * 3
    if final_norm is not None:
        args.append(final_norm.reshape(1, d))
        specs.append(_resident((1, d)))
    body = functools.partial(_ffn_body, has_mix=mix is not None, has_final=final_norm is not None)
    return pl.pallas_call(
        body,
        grid=(n // tm,),
        in_specs=specs,
        out_specs=pl.BlockSpec((tm, d), row),
        out_shape=jax.ShapeDtypeStruct((n, d), F32),
        scratch_shapes=[pltpu.VMEM((d, D_FF), F32), pltpu.VMEM((d, D_FF), F32), pltpu.VMEM((D_FF, d), F32),
                        pltpu.SemaphoreType.DMA((3, D_FF // FF_CHUNK))],
        compiler_params=_params("arbitrary"),
        name="ffn",
    )(*args)


def _rope_angles(seq):
    pos = np.arange(seq, dtype=np.float32)
    inv = np.float32(ROPE_THETA) ** (-np.arange(0, ROT_DIM, 2, dtype=np.float32) / np.float32(ROT_DIM))
    ang = pos[:, None] * inv[None, :]
    return np.cos(ang), np.sin(ang)


def _rope_lane_tables(seq):
    half = ROT_DIM // 2
    cos, sin = _rope_angles(seq)
    one = np.ones((seq, DIFF_HEAD_DIM - ROT_DIM), np.float32)
    zero8 = np.zeros((seq, half), np.float32)
    zero48 = np.zeros((seq, DIFF_HEAD_DIM - ROT_DIM), np.float32)
    cos_t = np.concatenate([cos, cos, one] * 2, axis=1)
    sa_t = np.concatenate([-sin, zero8, zero48] * 2, axis=1)
    sb_t = np.concatenate([zero8, sin, zero48] * 2, axis=1)
    return cos_t, sa_t, sb_t


def _proj0_body(x_ref, g_ref, wt_ref, wk_ref, wu_ref, cos_ref, sa_ref, sb_ref, cosr_ref, sinr_ref,
                qt_ref, k_ref, vt_ref, u_ref):
    h = _rms(x_ref[...], g_ref[...]).astype(BF16)
    pt = lax.dot_general(wt_ref[...], h, NT_DIMS, preferred_element_type=F32)
    pk = jnp.dot(h, wk_ref[...].astype(BF16), preferred_element_type=F32)
    half = ROT_DIM // 2
    scale = DIFF_HEAD_DIM ** -0.5 * LOG2_E

    cos_r, sin_r = cosr_ref[...], sinr_ref[...]
    for comp in range(2 * DIFF_HEADS):
        r0 = comp * DIFF_HEAD_DIM
        x1, x2 = pt[r0:r0 + half], pt[r0 + half:r0 + ROT_DIM]
        qt_ref[0, r0:r0 + half, :] = ((x1 * cos_r - x2 * sin_r) * scale).astype(BF16)
        qt_ref[0, r0 + half:r0 + ROT_DIM, :] = ((x2 * cos_r + x1 * sin_r) * scale).astype(BF16)
        qt_ref[0, r0 + ROT_DIM:r0 + DIFF_HEAD_DIM, :] = (pt[r0 + ROT_DIM:r0 + DIFF_HEAD_DIM] * scale).astype(BF16)
    _store_key_blocks(vt_ref, pt[HALF:2 * HALF].astype(BF16))

    cos_t, sa_t, sb_t = cos_ref[...], sa_ref[...], sb_ref[...]
    for hd in range(DIFF_HEADS):
        sl = slice(hd * LANES, (hd + 1) * LANES)
        t = pk[:, sl]
        k_ref[:, sl] = (t * cos_t + pltpu.roll(t, LANES - half, 1) * sa_t
                        + pltpu.roll(t, half, 1) * sb_t).astype(BF16)
    u_ref[...] = jnp.dot(h, wu_ref[...].astype(BF16), preferred_element_type=F32)


def _proj0(x, norm, w_in, seq):
    n, d = x.shape
    b = n // seq
    tm = min(TOKEN_TILE, seq)
    tps = seq // tm
    row = lambda i: (i, 0)
    pos = lambda i: (i % tps, 0)
    pos_t = lambda i: (0, i % tps)
    seq_t = lambda i: (i // tps, 0, i % tps)
    ta = min(ATTN_TILE, seq)
    key_blk = lambda i: (i // tps, i % tps, 0, 0)
    cos, sin = _rope_angles(seq)
    w_t = jnp.concatenate([w_in[:, :HALF], w_in[:, 2 * HALF:3 * HALF]], axis=1).T.astype(BF16)
    w_cols = lambda blk: pl.BlockSpec((d, HALF), lambda i: (0, blk), pipeline_mode=pl.Buffered(1))
    t_shape = jax.ShapeDtypeStruct((b, HALF, seq), BF16)
    return pl.pallas_call(
        _proj0_body,
        grid=(n // tm,),
        in_specs=[pl.BlockSpec((tm, d), row), _resident((1, d)), _resident((2 * HALF, d)),
                  w_cols(1), w_cols(3),
                  pl.BlockSpec((tm, LANES), pos), pl.BlockSpec((tm, LANES), pos),
                  pl.BlockSpec((tm, LANES), pos),
                  pl.BlockSpec((ROT_DIM // 2, tm), pos_t), pl.BlockSpec((ROT_DIM // 2, tm), pos_t)],
        out_specs=[pl.BlockSpec((1, HALF, tm), seq_t), pl.BlockSpec((tm, HALF), row),
                   pl.BlockSpec((1, tm // ta, HALF, ta), key_blk), pl.BlockSpec((tm, HALF), row)],
        out_shape=[t_shape, jax.ShapeDtypeStruct((n, HALF), BF16),
                   jax.ShapeDtypeStruct((b, seq // ta, HALF, ta), BF16),
                   jax.ShapeDtypeStruct((n, HALF), F32)],
        compiler_params=_params("parallel"),
        name="proj0",
    )(x, norm.reshape(1, d), w_t, w_in, w_in, *_rope_lane_tables(seq), cos.T, sin.T)


def _diff_attn_body(lam_ref, subln_ref, qt_ref, k_ref, vt_ref, o_ref, m_ref, acc_ref, *, tile):
    i = pl.program_id(1)
    m_ref[...] = jnp.full_like(m_ref, -jnp.inf)
    acc_ref[...] = jnp.zeros_like(acc_ref)
    ones = jnp.ones((ONES_ROWS, tile), BF16)

    def sweep(blocks):
        if any(diagonal for _, diagonal in blocks):
            query = lax.broadcasted_iota(jnp.int32, (tile, 2 * tile), 1)
            query = jnp.where(query >= tile, query - tile, query)
            keep = lax.broadcasted_iota(jnp.int32, (tile, 2 * tile), 0) <= query

        def scores(u):
            blk, hd = divmod(u, DIFF_HEADS)
            key0 = pl.multiple_of(blocks[blk][0] * tile, tile)
            rows = slice(hd * LANES, (hd + 1) * LANES)
            q_t = qt_ref[0, rows, :]
            q_both = jnp.concatenate([_keep_half_rows(q_t, 0), _keep_half_rows(q_t, 1)], axis=1)
            return jnp.dot(k_ref[0, pl.ds(key0, tile), rows], q_both, preferred_element_type=F32)

        def weights(u, s):
            blk, hd = divmod(u, DIFF_HEADS)
            if blocks[blk][1]:
                s = jnp.where(keep, s, -jnp.inf)
            m_old = m_ref[hd]
            m_new = jnp.maximum(m_old, jnp.max(s, axis=0, keepdims=True))
            m_ref[hd] = m_new
            return jnp.exp2(m_old - m_new), jnp.exp2(s - m_new).astype(BF16)

        def accumulate(u, alpha, p):
            blk, hd = divmod(u, DIFF_HEADS)
            v_aug = jnp.concatenate([vt_ref[0, blocks[blk][0], hd * LANES:(hd + 1) * LANES, :], ones], axis=0)
            acc_ref[hd] = alpha * acc_ref[hd] + jnp.dot(v_aug, p, preferred_element_type=F32)

        _staged(DIFF_HEADS * len(blocks), scores, weights, accumulate)

    _sweep_causal_blocks(i, sweep)

    lam_v = lam_ref[...]
    lam = (jnp.exp(jnp.sum(lam_v[0:1] * lam_v[1:2], keepdims=True))
           - jnp.exp(jnp.sum(lam_v[2:3] * lam_v[3:4], keepdims=True)) + DIFF_LAMBDA_INIT)
    for hd in range(DIFF_HEADS):
        a1, a2 = acc_ref[hd, :, :tile], acc_ref[hd, :, tile:]
        o_t = (a1[:LANES] / a1[LANES:LANES + 1] - lam * (a2[:LANES] / a2[LANES:LANES + 1]))
        o_t = o_t * lax.rsqrt(jnp.mean(o_t * o_t, axis=0, keepdims=True) + NORM_EPS)
        o_t = o_t * subln_ref[...] * (1.0 - DIFF_LAMBDA_INIT)
        o_ref[0, :, hd * LANES:(hd + 1) * LANES] = o_t.T.astype(BF16)


def _diff_attention(q_t, k, v_t, lam_vecs, subln):
    b, s, _ = k.shape
    n_blk, _, t = v_t.shape[1:]
    const = lambda bi, i: (0, 0)
    return pl.pallas_call(
        functools.partial(_diff_attn_body, tile=t),
        grid=(b, n_blk),
        in_specs=[pl.BlockSpec((4, DIFF_HEAD_DIM), const), pl.BlockSpec((LANES, 1), const),
                  pl.BlockSpec((1, HALF, t), lambda bi, i: (bi, 0, i)),
                  pl.BlockSpec((1, s, HALF), lambda bi, i: (bi, 0, 0)),
                  pl.BlockSpec((1, n_blk, HALF, t), lambda bi, i: (bi, 0, 0, 0))],
        out_specs=pl.BlockSpec((1, t, HALF), lambda bi, i: (bi, i, 0)),
        out_shape=jax.ShapeDtypeStruct((b, s, HALF), BF16),
        scratch_shapes=[pltpu.VMEM((DIFF_HEADS, 1, 2 * t), F32),
                        pltpu.VMEM((DIFF_HEADS, LANES + ONES_ROWS, 2 * t), F32)],
        compiler_params=_params("parallel", "arbitrary"),
        name="diff_attn",
    )(lam_vecs, subln.reshape(LANES, 1), q_t, k, v_t)


def _s5_discretize_body(lr_ref, li_ref, ldt_ref, br_ref, bi_ref, k_ref, pr_ref, pi_ref, bbr_ref, bbi_ref):
    lr, li = lr_ref[...], li_ref[...]
    dt = jnp.exp(ldt_ref[...])
    mag = jnp.exp(lr * dt)
    ar, ai = mag * jnp.cos(li * dt), mag * jnp.sin(li * dt)
    den = lr * lr + li * li
    gr = ((ar - 1.0) * lr + ai * li) / den
    gi = (ai * lr - (ar - 1.0) * li) / den
    br, bi = br_ref[...], bi_ref[...]
    bbr_ref[...] = gr * br - gi * bi
    bbi_ref[...] = gr * bi + gi * br
    steps = k_ref[...]
    mag_k = jnp.exp(lr * dt * steps)
    pr_ref[...] = mag_k * jnp.cos(li * dt * steps)
    pi_ref[...] = mag_k * jnp.sin(li * dt * steps)


def _s5_discretize(a_re, a_im, log_dt, b_re, b_im):
    row = lambda a: a.reshape(1, S5_NSTATE)
    ldt = jnp.broadcast_to(log_dt[:, None], (S5_GROUPS, S5_STATE))
    b_t = lambda a: a.transpose(2, 0, 1).reshape(S5_GROUP, S5_NSTATE)
    steps = jnp.arange(1, S5_CHUNK + 1, dtype=F32).reshape(S5_CHUNK, 1)
    pshape = jax.ShapeDtypeStruct((S5_CHUNK, S5_NSTATE), F32)
    bshape = jax.ShapeDtypeStruct((S5_GROUP, S5_NSTATE), F32)
    return pl.pallas_call(
        _s5_discretize_body,
        out_shape=[pshape, pshape, bshape, bshape],
        name="s5_discretize",
    )(row(a_re), row(a_im), row(ldt), b_t(b_re), b_t(b_im), steps)


def _s5_body(u_ref, pr_ref, pi_ref, bbr_ref, bbi_ref, cr_ref, ci_ref, d_ref, wglu_ref, bglu_ref,
             o_ref, us_ref, ys_ref, xr_ref, xi_ref, hin_ref, carry_ref, *, tile):
    n_chunks = tile // S5_CHUNK
    slabs = HALF // LANES
    gw = S5_NSTATE // slabs

    @pl.when(pl.program_id(1) == 0)
    def _():
        carry_ref[...] = jnp.zeros_like(carry_ref)

    u = u_ref[0]
    for g in range(slabs):
        us_ref[g] = u[:, g * LANES:(g + 1) * LANES]
    u_perm = jnp.concatenate(
        [jnp.concatenate([us_ref[g, pl.ds(st, n_chunks, stride=S5_CHUNK), :] for g in range(slabs)], axis=1)
         for st in range(S5_CHUNK)], axis=0).astype(BF16)

    for g in range(slabs):
        ug = u_perm[:, g * LANES:(g + 1) * LANES]
        xr_ref[:, g * gw:(g + 1) * gw] = jnp.dot(ug, bbr_ref[g], preferred_element_type=F32)
        xi_ref[:, g * gw:(g + 1) * gw] = jnp.dot(ug, bbi_ref[g], preferred_element_type=F32)

    ar, ai = pr_ref[0:1, :], pi_ref[0:1, :]
    hr, hi = xr_ref[0:n_chunks, :], xi_ref[0:n_chunks, :]
    for st in range(1, S5_CHUNK):
        rows = slice(st * n_chunks, (st + 1) * n_chunks)
        hr, hi = ar * hr - ai * hi + xr_ref[rows, :], ar * hi + ai * hr + xi_ref[rows, :]
        xr_ref[rows, :] = hr
        xi_ref[rows, :] = hi

    last = (S5_CHUNK - 1) * n_chunks
    ac_r, ac_i = pr_ref[S5_CHUNK - 1:S5_CHUNK, :], pi_ref[S5_CHUNK - 1:S5_CHUNK, :]

    def carry_step(c, carry):
        cr, ci = carry
        hin_ref[0, pl.ds(c, 1), :] = cr
        hin_ref[1, pl.ds(c, 1), :] = ci
        zr, zi = xr_ref[pl.ds(last + c, 1), :], xi_ref[pl.ds(last + c, 1), :]
        return ac_r * cr - ac_i * ci + zr, ac_r * ci + ac_i * cr + zi

    cr, ci = lax.fori_loop(0, n_chunks, carry_step, (carry_ref[0:1, :], carry_ref[1:2, :]), unroll=4)
    carry_ref[0:1, :] = cr
    carry_ref[1:2, :] = ci

    hin_r, hin_i = hin_ref[0], hin_ref[1]
    for st in range(S5_CHUNK):
        rows = slice(st * n_chunks, (st + 1) * n_chunks)
        p_r, p_i = pr_ref[st:st + 1, :], pi_ref[st:st + 1, :]
        xr_ref[rows, :] = xr_ref[rows, :] + (p_r * hin_r - p_i * hin_i)
        xi_ref[rows, :] = xi_ref[rows, :] + (p_r * hin_i + p_i * hin_r)
    for g in range(slabs):
        cols = slice(g * gw, (g + 1) * gw)
        yg = (jnp.dot(xr_ref[:, cols].astype(BF16), cr_ref[g], preferred_element_type=F32)
              - jnp.dot(xi_ref[:, cols].astype(BF16), ci_ref[g], preferred_element_type=F32))
        for st in range(S5_CHUNK):
            ys_ref[g, pl.ds(st, n_chunks, stride=S5_CHUNK), :] = yg[st * n_chunks:(st + 1) * n_chunks]

    y = jnp.concatenate([ys_ref[g] for g in range(slabs)], axis=1) + d_ref[...] * u
    z = jax.nn.gelu(y, approximate=True)
    gate = jnp.dot(z.astype(BF16), wglu_ref[...], preferred_element_type=F32) + bglu_ref[...]
    o_ref[0] = (z * jax.nn.sigmoid(gate)).astype(BF16)


def _s5(u, a_re, a_im, log_dt, b_re, b_im, c_re, c_im, d_skip, w_glu, b_glu):
    b, s, w = u.shape
    t = min(S5_TILE, s)
    slabs = w // LANES
    per = S5_GROUPS // slabs
    gw = S5_NSTATE // slabs
    pw_r, pw_i, bbr, bbi = _s5_discretize(a_re, a_im, log_dt, b_re, b_im)
    eye = jnp.eye(per, dtype=F32)

    def block_in(m):
        m = m.reshape(S5_GROUP, slabs, per, S5_STATE)
        return jnp.einsum('csgp,gh->sgchp', m, eye).reshape(slabs, LANES, gw).astype(BF16)

    def block_out(m):
        m = m.reshape(slabs, per, S5_GROUP, S5_STATE)
        return jnp.einsum('sgcp,gh->sgphc', m, eye).reshape(slabs, gw, LANES).astype(BF16)

    seq_blk = lambda bi, ti: (bi, ti, 0)
    n_chunks = t // S5_CHUNK
    return pl.pallas_call(
        functools.partial(_s5_body, tile=t),
        grid=(b, s // t),
        in_specs=[pl.BlockSpec((1, t, w), seq_blk),
                  _resident((S5_CHUNK, S5_NSTATE)), _resident((S5_CHUNK, S5_NSTATE)),
                  _resident((slabs, LANES, gw)), _resident((slabs, LANES, gw)),
                  _resident((slabs, gw, LANES)), _resident((slabs, gw, LANES)),
                  _resident((1, w)), _resident((w, w)), _resident((1, w))],
        out_specs=pl.BlockSpec((1, t, w), seq_blk),
        out_shape=jax.ShapeDtypeStruct((b, s, w), BF16),
        scratch_shapes=[pltpu.VMEM((slabs, t, LANES), F32), pltpu.VMEM((slabs, t, LANES), F32),
                        pltpu.VMEM((t, S5_NSTATE), F32), pltpu.VMEM((t, S5_NSTATE), F32),
                        pltpu.VMEM((2, n_chunks, S5_NSTATE), F32), pltpu.VMEM((2, S5_NSTATE), F32)],
        compiler_params=_params("parallel", "arbitrary"),
        name="s5",
    )(u, pw_r, pw_i, block_in(bbr), block_in(bbi),
      block_out(c_re), block_out(c_im), d_skip.reshape(1, w), w_glu.astype(BF16), b_glu.reshape(1, w))


def _proj1_body(x_ref, g_ref, wt_ref, wm_ref, wo_ref, wk_ref, wg_ref,
                qk_ref, vm_ref, om_ref, qft_ref, kf_ref, vft_ref, gt_ref):
    h = _rms(x_ref[...], g_ref[...]).astype(BF16)
    pt = lax.dot_general(wt_ref[...], h, NT_DIMS, preferred_element_type=F32)
    qft_ref[0] = (pt[:HALF] * (FOX_HEAD_DIM ** -0.5 * LOG2_E)).astype(BF16)
    _store_key_blocks(vft_ref, pt[HALF:].astype(BF16))
    pm = jnp.dot(h, wm_ref[...].astype(BF16), preferred_element_type=F32)
    qk_ref[...] = pm[:, 0:HALF]
    vm_ref[...] = pm[:, HALF:2 * HALF].astype(BF16)
    om_ref[...] = jnp.dot(h, wo_ref[...].astype(BF16), preferred_element_type=F32)
    kf_ref[...] = jnp.dot(h, wk_ref[...].astype(BF16), preferred_element_type=F32).astype(BF16)
    gt_ref[...] = jnp.dot(h, wg_ref[...].astype(BF16), preferred_element_type=F32)


def _proj1(x, norm, w_in, seq):
    n, d = x.shape
    b = n // seq
    tm = min(TOKEN_TILE, seq)
    tps = seq // tm
    row = lambda i: (i, 0)
    seq_t = lambda i: (i // tps, 0, i % tps)
    ta = min(ATTN_TILE, seq)
    key_blk = lambda i: (i // tps, i % tps, 0, 0)
    o = 2 * HALF
    g = 2 * MLSTM_HEADS
    om = o + g
    w_o = w_in[:, om:om + HALF]
    w_k = w_in[:, om + 2 * HALF:om + 3 * HALF]
    w_g = jnp.concatenate([w_in[:, o:om], w_in[:, om + 4 * HALF:],
                           jnp.zeros((d, GATE_COLS - g - FOX_HEADS), w_in.dtype)], axis=1)
    w_t = jnp.concatenate([w_in[:, om + HALF:om + 2 * HALF], w_in[:, om + 3 * HALF:om + 4 * HALF]],
                          axis=1).T.astype(BF16)
    f32_half = jax.ShapeDtypeStruct((n, HALF), F32)
    bf_half = jax.ShapeDtypeStruct((n, HALF), BF16)
    t_shape = jax.ShapeDtypeStruct((b, HALF, seq), BF16)
    half_spec = pl.BlockSpec((tm, HALF), row)
    t_spec = pl.BlockSpec((1, HALF, tm), seq_t)
    return pl.pallas_call(
        _proj1_body,
        grid=(n // tm,),
        in_specs=[pl.BlockSpec((tm, d), row), _resident((1, d)), _resident((2 * HALF, d)),
                  pl.BlockSpec((d, o), lambda i: (0, 0), pipeline_mode=pl.Buffered(1)),
                  _resident((d, HALF)), _resident((d, HALF)), _resident((d, GATE_COLS))],
        out_specs=[half_spec, half_spec, half_spec, t_spec, half_spec,
                   pl.BlockSpec((1, tm // ta, HALF, ta), key_blk), pl.BlockSpec((tm, GATE_COLS), row)],
        out_shape=[f32_half, bf_half, f32_half, t_shape, bf_half,
                   jax.ShapeDtypeStruct((b, seq // ta, HALF, ta), BF16),
                   jax.ShapeDtypeStruct((n, GATE_COLS), F32)],
        compiler_params=_params("parallel"),
        name="proj1",
    )(x, norm.reshape(1, d), w_t, w_in, w_o, w_k, w_g)


def _mlstm_body(qk_ref, v_ref, og_ref, gt_ref, cw_ref, cb_ref, gb_ref, on_ref, out_ref,
                xbuf_ref, cn_ref, m_ref, *, chunk):
    pad = SUBLANES
    dk, dv = MLSTM_QK_DIM, MLSTM_V_DIM

    @pl.when(pl.program_id(1) == 0)
    def _():
        xbuf_ref[0:pad, :] = jnp.zeros((pad, HALF), F32)
        cn_ref[...] = jnp.zeros_like(cn_ref)
        m_ref[...] = jnp.zeros_like(m_ref)

    x = qk_ref[0]
    xbuf_ref[pad:pad + chunk, :] = x
    a = jnp.zeros_like(x) + cb_ref[...]
    for tap in range(MLSTM_CONV):
        a = a + cw_ref[tap:tap + 1, :] * xbuf_ref[pl.ds(pad - (MLSTM_CONV - 1) + tap, chunk), :]
    xbuf_ref[0:pad, :] = x[chunk - pad:chunk, :]
    a = a * jax.nn.sigmoid(a)
    q = a[:, :MLSTM_HEADS * dk] * dk ** -0.5
    k = a[:, MLSTM_HEADS * dk:]
    k_t = k.T

    gi = gt_ref[0] + gb_ref[...]
    bcum = _cumsum_rows(_log_sigmoid(gi))
    gi_t = gi.T
    bcum_t = bcum.T
    tri = (lax.broadcasted_iota(jnp.int32, (chunk, chunk), 1)
           <= lax.broadcasted_iota(jnp.int32, (chunk, chunk), 0))
    lane = lax.broadcasted_iota(jnp.int32, (1, LANES), 1)
    v_all = v_ref[0]
    o_gate = og_ref[0]
    ones = jnp.ones((chunk, dv), BF16)

    states = [cn_ref[pair] for pair in range(MLSTM_HEADS // 2)]

    def products(hd):
        pair, odd = hd // 2, hd % 2
        sel = (lane >= dk) if odd else (lane < dk)
        q2 = jnp.where(sel, q[:, pair * LANES:(pair + 1) * LANES], 0.0).astype(BF16)
        k2 = k[:, pair * LANES:(pair + 1) * LANES].astype(BF16)
        qk = lax.dot_general(q2, k2, NT_DIMS, preferred_element_type=F32)
        return qk, jnp.dot(q2, states[pair].astype(BF16), preferred_element_type=F32)

    def gate_weights(hd, prods):
        qk, q_state = prods
        f_lane = MLSTM_HEADS + hd
        b_col = bcum[:, f_lane:f_lane + 1]
        b_row = bcum_t[f_lane:f_lane + 1, :]
        i_row = gi_t[hd:hd + 1, :]
        m_prev = m_ref[hd]
        gain_row = i_row - b_row
        dm = jnp.where(tri, b_col + gain_row, -jnp.inf)
        g = b_col + m_prev
        mt = jnp.maximum(g, jnp.max(dm, axis=-1, keepdims=True))
        s = (qk * jnp.exp(dm - mt)).astype(BF16)
        w_inter = jnp.exp(g - mt)

        b_last = bcum[chunk - 1:chunk, f_lane:f_lane + 1]
        dec = b_last + gain_row
        m_new = jnp.maximum(b_last + m_prev, jnp.max(dec, axis=-1, keepdims=True))
        k_w = (k_t[hd * dk:(hd + 1) * dk, :] * jnp.exp(dec - m_new)).astype(BF16)
        carry_w = jnp.exp(b_last + m_prev - m_new)
        m_ref[hd] = m_new
        return s, k_w, q_state, w_inter, mt, carry_w

    def outputs(hd, s, k_w, q_state, w_inter, mt, carry_w):
        pair, odd = hd // 2, hd % 2
        v_aug = jnp.concatenate([v_all[:, hd * dv:(hd + 1) * dv], ones], axis=1)
        s_v = jnp.dot(s, v_aug, preferred_element_type=F32)
        num = w_inter * q_state[:, :dv] + s_v[:, :dv]
        den = w_inter * q_state[:, dv:] + s_v[:, dv:]
        h_out = num / jnp.maximum(jnp.abs(den), jnp.exp(-mt))
        rows = slice(odd * dk, (odd + 1) * dk)
        cn_ref[pair, rows, :] = (carry_w * states[pair][rows, :]
                                 + jnp.dot(k_w, v_aug, preferred_element_type=F32))
        cols = slice(hd * dv, (hd + 1) * dv)
        h_norm = _rms(h_out, on_ref[hd:hd + 1, :])
        out_ref[0, :, cols] = (jax.nn.sigmoid(o_gate[:, cols]) * h_norm).astype(BF16)

    _staged(MLSTM_HEADS, products, gate_weights, outputs)


def _mlstm(qk, v, o_gate, gates, conv_w, conv_b, b_i, b_f, out_norm):
    b, s, _ = qk.shape
    c = min(MLSTM_CHUNK, s)
    gate_bias = jnp.concatenate([b_i, b_f, jnp.zeros((GATE_COLS - 2 * MLSTM_HEADS,), F32)]).reshape(1, GATE_COLS)
    seq_blk = lambda bi, ci: (bi, ci, 0)
    half_spec = pl.BlockSpec((1, c, HALF), seq_blk)
    return pl.pallas_call(
        functools.partial(_mlstm_body, chunk=c),
        grid=(b, s // c),
        in_specs=[half_spec, half_spec, half_spec, pl.BlockSpec((1, c, GATE_COLS), seq_blk),
                  _resident((MLSTM_CONV, HALF)), _resident((1, HALF)), _resident((1, GATE_COLS)),
                  _resident((MLSTM_HEADS, MLSTM_V_DIM))],
        out_specs=half_spec,
        out_shape=jax.ShapeDtypeStruct((b, s, HALF), BF16),
        scratch_shapes=[pltpu.VMEM((c + SUBLANES, HALF), F32),
                        pltpu.VMEM((MLSTM_HEADS // 2, 2 * MLSTM_QK_DIM, 2 * MLSTM_V_DIM), F32),
                        pltpu.VMEM((MLSTM_HEADS, 1, 1), F32)],
        compiler_params=_params("parallel", "arbitrary"),
        name="mlstm",
    )(qk, v, o_gate, gates, conv_w, conv_b.reshape(1, HALF), gate_bias, out_norm)


FOX_GATE_LANE = 2 * MLSTM_HEADS
F_TERMS = 3
FQ_ROWS = 16


def _fox_placements():
    sel_k = np.zeros((F_TERMS * GATE_COLS, HALF), np.float32)
    sel_q = np.zeros((FOX_HEADS * FQ_ROWS, F_TERMS * GATE_COLS), np.float32)
    ones_k = np.zeros((1, HALF), np.float32)
    ones_q = np.zeros((FOX_HEADS * FQ_ROWS, 1), np.float32)
    for head in range(FOX_HEADS):
        pair, c = divmod(head, 2)
        for term in range(F_TERMS):
            src = term * GATE_COLS + FOX_GATE_LANE + head
            sel_k[src, pair * LANES + F_TERMS * c + term] = -1.0
            sel_q[head * FQ_ROWS + 2 * F_TERMS + term, src] = 1.0
            ones_k[0, pair * LANES + 2 * F_TERMS + term] = 1.0
            ones_q[head * FQ_ROWS + F_TERMS * c + term, 0] = 1.0
    return (jnp.asarray(sel_k, BF16), jnp.asarray(ones_k), jnp.asarray(sel_q, BF16), jnp.asarray(ones_q))


def _fox_cumsum_body(gt_ref, gb_ref, sel_k_ref, ones_k_ref, sel_q_ref, ones_q_ref, fq_ref, fk_ref):
    f = _cumsum_rows(_log_sigmoid(gt_ref[0] + gb_ref[...])) * LOG2_E
    hi = f.astype(BF16)
    rest = f - hi.astype(F32)
    mid = rest.astype(BF16)
    lo = (rest - mid.astype(F32)).astype(BF16)
    terms = jnp.concatenate([hi, mid, lo], axis=1)
    fk_ref[0] = (jnp.dot(terms, sel_k_ref[...], preferred_element_type=F32) + ones_k_ref[...]).astype(BF16)
    fq_t = lax.dot_general(sel_q_ref[...], terms, NT_DIMS, preferred_element_type=F32)
    fq_ref[0] = (fq_t + ones_q_ref[...]).astype(BF16)


def _fox_cumsum(gates, fox_b_f):
    b, s, _ = gates.shape
    bias = jnp.concatenate([jnp.zeros((FOX_GATE_LANE,), F32), fox_b_f,
                            jnp.zeros((GATE_COLS - FOX_GATE_LANE - FOX_HEADS,), F32)]).reshape(1, GATE_COLS)
    placements = _fox_placements()
    return pl.pallas_call(
        _fox_cumsum_body,
        grid=(b,),
        in_specs=[pl.BlockSpec((1, s, GATE_COLS), lambda bi: (bi, 0, 0)), _resident((1, GATE_COLS))]
                 + [_resident(p.shape) for p in placements],
        out_specs=[pl.BlockSpec((1, FOX_HEADS * FQ_ROWS, s), lambda bi: (bi, 0, 0)),
                   pl.BlockSpec((1, s, HALF), lambda bi: (bi, 0, 0))],
        out_shape=[jax.ShapeDtypeStruct((b, FOX_HEADS * FQ_ROWS, s), BF16),
                   jax.ShapeDtypeStruct((b, s, HALF), BF16)],
        compiler_params=_params("parallel"),
        name="fox_cumsum",
    )(gates, bias, *placements)


def _fox_attn_body(qt_ref, fq_ref, k_ref, fk_ref, vt_ref, o_ref, m_ref, acc_ref, *, tile):
    i = pl.program_id(1)
    dh = FOX_HEAD_DIM
    unused_rows = jnp.zeros((LANES - FQ_ROWS, tile), BF16)
    m_ref[...] = jnp.full_like(m_ref, -jnp.inf)
    acc_ref[...] = jnp.zeros_like(acc_ref)
    ones = jnp.ones((ONES_ROWS, tile), BF16)

    n_pairs = FOX_HEADS // 2

    def sweep(blocks):
        if any(diagonal for _, diagonal in blocks):
            query = lax.broadcasted_iota(jnp.int32, (tile, 2 * tile), 1)
            query = jnp.where(query >= tile, query - tile, query)
            keep = lax.broadcasted_iota(jnp.int32, (tile, 2 * tile), 0) <= query

        def scores(u):
            blk, pair = divmod(u, n_pairs)
            key0 = pl.multiple_of(blocks[blk][0] * tile, tile)
            rows = slice(pair * LANES, (pair + 1) * LANES)
            q_t = qt_ref[0, rows, :]
            k_aug = jnp.concatenate([k_ref[0, pl.ds(key0, tile), rows],
                                     fk_ref[0, pl.ds(key0, tile), rows]], axis=1)
            q_aug = []
            for c in range(2):
                hd = 2 * pair + c
                q_aug.append(jnp.concatenate(
                    [_keep_half_rows(q_t, c), fq_ref[0, hd * FQ_ROWS:(hd + 1) * FQ_ROWS, :], unused_rows], axis=0))
            return jnp.dot(k_aug, jnp.concatenate(q_aug, axis=1), preferred_element_type=F32)

        def weights(u, s):
            blk, pair = divmod(u, n_pairs)
            if blocks[blk][1]:
                s = jnp.where(keep, s, -jnp.inf)
            m_old = m_ref[pair]
            m_new = jnp.maximum(m_old, jnp.max(s, axis=0, keepdims=True))
            m_ref[pair] = m_new
            return jnp.exp2(m_old - m_new), jnp.exp2(s - m_new).astype(BF16)

        def accumulate(u, alpha, p):
            blk, pair = divmod(u, n_pairs)
            for c in range(2):
                hd = 2 * pair + c
                cols = slice(c * tile, (c + 1) * tile)
                v_aug = jnp.concatenate([vt_ref[0, blocks[blk][0], hd * dh:(hd + 1) * dh, :], ones], axis=0)
                acc_ref[hd] = alpha[:, cols] * acc_ref[hd] + jnp.dot(v_aug, p[:, cols], preferred_element_type=F32)

        _staged(n_pairs * len(blocks), scores, weights, accumulate)

    _sweep_causal_blocks(i, sweep)

    for pair in range(FOX_HEADS // 2):
        a0, a1 = acc_ref[2 * pair], acc_ref[2 * pair + 1]
        o_t = jnp.concatenate([a0[:dh] / a0[dh:dh + 1], a1[:dh] / a1[dh:dh + 1]], axis=0)
        o_ref[0, :, pair * LANES:(pair + 1) * LANES] = o_t.T.astype(BF16)


def _fox_attention(q_t, fq_t, k, fk, v_t):
    b, s, _ = k.shape
    n_blk, _, t = v_t.shape[1:]
    whole = lambda bi, i: (bi, 0, 0)
    return pl.pallas_call(
        functools.partial(_fox_attn_body, tile=t),
        grid=(b, n_blk),
        in_specs=[pl.BlockSpec((1, HALF, t), lambda bi, i: (bi, 0, i)),
                  pl.BlockSpec((1, FOX_HEADS * FQ_ROWS, t), lambda bi, i: (bi, 0, i)),
                  pl.BlockSpec((1, s, HALF), whole), pl.BlockSpec((1, s, HALF), whole),
                  pl.BlockSpec((1, n_blk, HALF, t), lambda bi, i: (bi, 0, 0, 0))],
        out_specs=pl.BlockSpec((1, t, HALF), lambda bi, i: (bi, i, 0)),
        out_shape=jax.ShapeDtypeStruct((b, s, HALF), BF16),
        scratch_shapes=[pltpu.VMEM((FOX_HEADS // 2, 1, 2 * t), F32),
                        pltpu.VMEM((FOX_HEADS, FOX_HEAD_DIM + ONES_ROWS, t), F32)],
        compiler_params=_params("parallel", "arbitrary"),
        name="fox_attn",
    )(q_t, fq_t, k, fk, v_t)


def kernel(x, l0_ffn1_norm, l0_ffn1_w_gate, l0_ffn1_w_up, l0_ffn1_w_down, l0_mix_norm, l0_w_in, l0_diff_lambda_q1, l0_diff_lambda_k1, l0_diff_lambda_q2, l0_diff_lambda_k2, l0_diff_subln, l0_s5_a_re, l0_s5_a_im, l0_s5_log_dt, l0_s5_b_re, l0_s5_b_im, l0_s5_c_re, l0_s5_c_im, l0_s5_d, l0_s5_w_glu, l0_s5_b_glu, l0_w_out, l0_ffn2_norm, l0_ffn2_w_gate, l0_ffn2_w_up, l0_ffn2_w_down, l1_ffn1_norm, l1_ffn1_w_gate, l1_ffn1_w_up, l1_ffn1_w_down, l1_mix_norm, l1_w_in, l1_mlstm_conv_w, l1_mlstm_conv_b, l1_mlstm_b_i, l1_mlstm_b_f, l1_mlstm_out_norm, l1_fox_b_f, l1_w_out, l1_ffn2_norm, l1_ffn2_w_gate, l1_ffn2_w_up, l1_ffn2_w_down, final_norm):
    b, s, d = x.shape
    n = b * s
    seq = lambda a: a.reshape(b, s, a.shape[-1])
    flat = lambda a: a.reshape(n, a.shape[-1])

    x = _ffn(x.reshape(n, d), l0_ffn1_norm, l0_ffn1_w_gate, l0_ffn1_w_up, l0_ffn1_w_down)
    q_t, k, v_t, u = _proj0(x, l0_mix_norm, l0_w_in, s)
    lam_vecs = jnp.stack([l0_diff_lambda_q1, l0_diff_lambda_k1, l0_diff_lambda_q2, l0_diff_lambda_k2])
    ya = _diff_attention(q_t, seq(k), v_t, lam_vecs, l0_diff_subln)
    yb = _s5(seq(u), l0_s5_a_re, l0_s5_a_im, l0_s5_log_dt, l0_s5_b_re, l0_s5_b_im,
             l0_s5_c_re, l0_s5_c_im, l0_s5_d, l0_s5_w_glu, l0_s5_b_glu)
    x = _ffn(x, l0_ffn2_norm, l0_ffn2_w_gate, l0_ffn2_w_up, l0_ffn2_w_down,
             mix=(flat(ya), flat(yb), l0_w_out[:HALF], l0_w_out[HALF:]))

    x = _ffn(x, l1_ffn1_norm, l1_ffn1_w_gate, l1_ffn1_w_up, l1_ffn1_w_down)
    qk_m, v_m, o_m, qf_t, k_f, vf_t, gates = _proj1(x, l1_mix_norm, l1_w_in, s)
    hm = _mlstm(seq(qk_m), seq(v_m), seq(o_m), seq(gates), l1_mlstm_conv_w, l1_mlstm_conv_b,
                l1_mlstm_b_i, l1_mlstm_b_f, l1_mlstm_out_norm)
    fq_t, fk = _fox_cumsum(seq(gates), l1_fox_b_f)
    hf = _fox_attention(qf_t, fq_t, seq(k_f), fk, vf_t)
    x = _ffn(x, l1_ffn2_norm, l1_ffn2_w_gate, l1_ffn2_w_up, l1_ffn2_w_down,
             mix=(flat(hm), flat(hf), l1_w_out[:HALF], l1_w_out[HALF:]), final_norm=final_norm)
    return x.reshape(b, s, d)
```
